```python
import math
import jax, jax.numpy as jnp
from jax import lax
import numpy as np

D_MODEL = 1024
BATCH = 8
SEQ = 2048
DEPTH = 2

HEAD_DIM = 64
Q_BLOCK = 128
ROPE_THETA = 10000.0
LN_EPS = 1e-5
RMS_EPS = 1e-5
DEEPNORM_ALPHA = (2 * DEPTH) ** 0.25
DEEPNORM_BETA = (8 * DEPTH) ** -0.25

CONV_CH = D_MODEL // 2
CONV_WIDTH = 31
DIFF_VDIM = 2 * HEAD_DIM
DIFF_HEADS = (D_MODEL // 2) // DIFF_VDIM
EVEN_IN = 2 * CONV_CH + DIFF_HEADS * (4 * HEAD_DIM + DIFF_VDIM)
EVEN_MIX = CONV_CH + DIFF_HEADS * DIFF_VDIM

SB_HEADS = (D_MODEL // 2) // HEAD_DIM
SB_WIDTH = SB_HEADS * HEAD_DIM
POOL_WINDOWS = (2, 4, 8, 16)
POOL_GROUPS = len(POOL_WINDOWS)
POOL_CH = (D_MODEL // 2) // POOL_GROUPS
ODD_IN = 3 * SB_WIDTH + POOL_GROUPS * POOL_CH
ODD_MIX = SB_WIDTH + POOL_GROUPS * POOL_CH

N_EXPERTS = 16
N_GROUPS = 4
EXPERTS_PER_GROUP = N_EXPERTS // N_GROUPS
TOP_K = 2
D_FF_EXPERT = 512

kernel_name = 'hybrid_conv_diffattn_stickbreak_pool_moe'


def _layer_norm(x, g, b):
    xf = x.astype(jnp.float32)
    mu = xf.mean(-1, keepdims=True)
    var = jnp.square(xf - mu).mean(-1, keepdims=True)
    y = (xf - mu) * lax.rsqrt(var + LN_EPS)
    return (y * g.astype(jnp.float32) + b.astype(jnp.float32)).astype(x.dtype)


def _rope_tables(n_pos):
    inv = ROPE_THETA ** (-jnp.arange(0, HEAD_DIM, 2, dtype=jnp.float32) / HEAD_DIM)
    ang = jnp.arange(n_pos, dtype=jnp.float32)[:, None] * inv[None, :]
    ang = jnp.concatenate([ang, ang], axis=-1)
    return jnp.cos(ang), jnp.sin(ang)


def _rope(u, cos, sin):
    u1, u2 = jnp.split(u, 2, axis=-1)
    return u * cos + jnp.concatenate([-u2, u1], axis=-1) * sin


def _sweep_query_blocks(block_fn, n_pos):
    return jnp.concatenate(
        [block_fn(i * Q_BLOCK, (i + 1) * Q_BLOCK) for i in range(n_pos // Q_BLOCK)], axis=2)


def _differential_attention(q1, q2, k1, k2, v, lam, lam_init, subln_g):
    scale = HEAD_DIM ** -0.5

    def block(start, end):
        mask = jnp.arange(end)[None, :] <= (start + jnp.arange(Q_BLOCK))[:, None]

        def probs(q, k):
            s = jnp.einsum('bhqd,bhkd->bhqk', q[:, :, start:end], k[:, :, :end]) * scale
            return jax.nn.softmax(jnp.where(mask, s, -jnp.inf), axis=-1)

        w = probs(q1, k1) - lam * probs(q2, k2)
        return jnp.einsum('bhqk,bhkd->bhqd', w, v[:, :, :end])

    o = _sweep_query_blocks(block, q1.shape[2])
    o = o * lax.rsqrt(jnp.mean(jnp.square(o), -1, keepdims=True) + RMS_EPS)
    return o * subln_g.astype(jnp.float32) * (1.0 - lam_init)


def _stick_breaking_attention(q, k, v):
    scale = HEAD_DIM ** -0.5

    def block(start, end):
        mask = jnp.arange(end)[None, :] < (start + jnp.arange(Q_BLOCK))[:, None]
        z = jnp.einsum('bhqd,bhkd->bhqk', q[:, :, start:end], k[:, :, :end]) * scale
        log_stay = jnp.where(mask, jax.nn.log_sigmoid(-z), 0.0)
        suffix = lax.cumsum(log_stay, axis=3, reverse=True) - log_stay
        a = jnp.where(mask, jnp.exp(jax.nn.log_sigmoid(z) + suffix), 0.0)
        return jnp.einsum('bhqk,bhkd->bhqd', a, v[:, :, :end])

    return _sweep_query_blocks(block, q.shape[2])


def _causal_depthwise_conv(u, w, b):
    kern = w[:, None, :].astype(u.dtype)
    y = lax.conv_general_dilated(
        u, kern, window_strides=(1,), padding=[(CONV_WIDTH - 1, 0)],
        dimension_numbers=('NWC', 'WIO', 'NWC'), feature_group_count=u.shape[-1])
    return y + b.astype(u.dtype)


def _multiscale_pool(u, pool_w, pool_scale):
    bsz, n_pos, _ = u.shape
    ug = u.astype(jnp.float32).reshape(bsz, n_pos, POOL_GROUPS, POOL_CH)
    pos = jnp.arange(n_pos)
    outs = []
    for g, win in enumerate(POOL_WINDOWS):
        xg = ug[:, :, g]
        c_pad = jnp.pad(jnp.cumsum(xg, axis=1), ((0, 0), (win, 0), (0, 0)))
        window_sum = c_pad[:, win:] - c_pad[:, :n_pos]
        count = jnp.minimum(pos + 1, win).astype(jnp.float32)
        outs.append(window_sum / count[None, :, None] - xg)
    pooled = jnp.stack(outs, axis=2)
    mixed = jnp.einsum('bsgc,gcd->bsgd', pooled, pool_w.astype(jnp.float32))
    out = mixed.reshape(bsz, n_pos, POOL_GROUPS * POOL_CH) * pool_scale.astype(jnp.float32)
    return out.astype(u.dtype)


def _lambda_init(layer_idx):
    return 0.8 - 0.6 * math.exp(-0.3 * layer_idx)


def _even_mixer(x, in_proj, conv_w, conv_b, conv_ln_g, conv_ln_b,
                lambda_q1, lambda_k1, lambda_q2, lambda_k2, subln_g, out_proj,
                lam_init, cos, sin):
    bsz, n_pos, _ = x.shape
    p = x @ in_proj
    qk_w = DIFF_HEADS * 2 * HEAD_DIM
    a_val, a_gate, dq, dk, dv = jnp.split(
        p, [CONV_CH, 2 * CONV_CH, 2 * CONV_CH + qk_w, 2 * CONV_CH + 2 * qk_w], axis=-1)
    a = a_val * jax.nn.sigmoid(a_gate)
    a = _causal_depthwise_conv(a, conv_w, conv_b)
    a = jax.nn.silu(_layer_norm(a, conv_ln_g, conv_ln_b))
    def split_qk(t):
        return t.reshape(bsz, n_pos, DIFF_HEADS, 2, HEAD_DIM).transpose(3, 0, 2, 1, 4).astype(jnp.float32)
    q = split_qk(dq)
    k = split_qk(dk)
    v = dv.reshape(bsz, n_pos, DIFF_HEADS, DIFF_VDIM).transpose(0, 2, 1, 3).astype(jnp.float32)
    q1, q2 = _rope(q[0], cos, sin), _rope(q[1], cos, sin)
    k1, k2 = _rope(k[0], cos, sin), _rope(k[1], cos, sin)
    f32 = jnp.float32
    lam = (jnp.exp(jnp.sum(lambda_q1.astype(f32) * lambda_k1.astype(f32)))
           - jnp.exp(jnp.sum(lambda_q2.astype(f32) * lambda_k2.astype(f32))) + lam_init)
    o = _differential_attention(q1, q2, k1, k2, v, lam, lam_init, subln_g)
    o = o.transpose(0, 2, 1, 3).reshape(bsz, n_pos, DIFF_HEADS * DIFF_VDIM).astype(x.dtype)
    return jnp.concatenate([a, o], axis=-1) @ out_proj


def _odd_mixer(x, in_proj, pool_w, pool_scale, out_proj):
    bsz, n_pos, _ = x.shape
    p = x @ in_proj
    q, k, v, u = jnp.split(p, [SB_WIDTH, 2 * SB_WIDTH, 3 * SB_WIDTH], axis=-1)

    def heads(t):
        return t.reshape(bsz, n_pos, SB_HEADS, HEAD_DIM).transpose(0, 2, 1, 3).astype(jnp.float32)

    o = _stick_breaking_attention(heads(q), heads(k), heads(v))
    o = o.transpose(0, 2, 1, 3).reshape(bsz, n_pos, SB_WIDTH).astype(x.dtype)
    d = _multiscale_pool(u, pool_w, pool_scale)
    return jnp.concatenate([o, d], axis=-1) @ out_proj


def _grouped_moe(x, router_w, router_bias, w_gate, w_up, w_down):
    bsz, n_pos, dm = x.shape
    t = x.reshape(-1, dm)
    n_tok = t.shape[0]
    affinity = jax.nn.sigmoid(t.astype(jnp.float32) @ router_w.astype(jnp.float32))
    sel = affinity + router_bias.astype(jnp.float32)
    sel_g = sel.reshape(n_tok, N_GROUPS, EXPERTS_PER_GROUP)
    group_score = lax.top_k(sel_g, TOP_K)[0].sum(-1)
    g_best = jnp.argmax(group_score, axis=-1)
    idx = jnp.broadcast_to(g_best[:, None, None], (n_tok, 1, EXPERTS_PER_GROUP))
    in_group = jnp.take_along_axis(sel_g, idx, axis=1)[:, 0]
    _, local = lax.top_k(in_group, TOP_K)
    expert_idx = g_best[:, None] * EXPERTS_PER_GROUP + local
    gate = jnp.take_along_axis(affinity, expert_idx, axis=1)
    gate = gate / gate.sum(-1, keepdims=True)
    combine = (jax.nn.one_hot(expert_idx, N_EXPERTS, dtype=jnp.float32) * gate[..., None]).sum(1)
    h = jax.nn.silu(jnp.einsum('td,edf->tef', t, w_gate)) * jnp.einsum('td,edf->tef', t, w_up)
    h = h * combine[:, :, None].astype(h.dtype)
    y = jnp.einsum('tef,efd->td', h, w_down)
    return y.reshape(bsz, n_pos, dm)


def _normal(key, shape, scale):
    return jax.random.normal(key, shape, jnp.float32) * scale


def setup_inputs(seed: int = 0) -> dict:
    key = jax.random.key(seed)
    ks = iter(jax.random.split(key, 48))
    d = D_MODEL

    def gain(n):
        return 1.0 + _normal(next(ks), (n,), 0.02)

    def bias(n):
        return _normal(next(ks), (n,), 0.01)

    inp = {}
    inp['x'] = _normal(next(ks), (BATCH, SEQ, d), 1.0)
    inp['router_w'] = _normal(next(ks), (d, N_EXPERTS), d ** -0.5)
    inp['router_bias'] = _normal(next(ks), (N_EXPERTS,), 0.01)
    inp['l0_in_proj'] = _normal(next(ks), (d, EVEN_IN), d ** -0.5)
    inp['l0_conv_w'] = _normal(next(ks), (CONV_WIDTH, CONV_CH), CONV_WIDTH ** -0.5)
    inp['l0_conv_b'] = bias(CONV_CH)
    inp['l0_conv_ln_g'] = gain(CONV_CH)
    inp['l0_conv_ln_b'] = bias(CONV_CH)
    inp['l0_lambda_q1'] = _normal(next(ks), (HEAD_DIM,), 0.1)
    inp['l0_lambda_k1'] = _normal(next(ks), (HEAD_DIM,), 0.1)
    inp['l0_lambda_q2'] = _normal(next(ks), (HEAD_DIM,), 0.1)
    inp['l0_lambda_k2'] = _normal(next(ks), (HEAD_DIM,), 0.1)
    inp['l0_subln_g'] = gain(DIFF_VDIM)
    inp['l0_out_proj'] = _normal(next(ks), (EVEN_MIX, d), EVEN_MIX ** -0.5 * DEEPNORM_BETA)
    inp['l0_ln_mix_g'] = gain(d)
    inp['l0_ln_mix_b'] = bias(d)
    inp['l0_w_gate'] = _normal(next(ks), (N_EXPERTS, d, D_FF_EXPERT), d ** -0.5)
    inp['l0_w_up'] = _normal(next(ks), (N_EXPERTS, d, D_FF_EXPERT), d ** -0.5)
    inp['l0_w_down'] = _normal(next(ks), (N_EXPERTS, D_FF_EXPERT, d), D_FF_EXPERT ** -0.5 * DEEPNORM_BETA)
    inp['l0_ln_ffn_g'] = gain(d)
    inp['l0_ln_ffn_b'] = bias(d)
    inp['l1_in_proj'] = _normal(next(ks), (d, ODD_IN), d ** -0.5)
    inp['l1_pool_w'] = _normal(next(ks), (POOL_GROUPS, POOL_CH, POOL_CH), POOL_CH ** -0.5)
    inp['l1_pool_scale'] = 1.0 + _normal(next(ks), (POOL_GROUPS * POOL_CH,), 0.1)
    inp['l1_out_proj'] = _normal(next(ks), (ODD_MIX, d), ODD_MIX ** -0.5 * DEEPNORM_BETA)
    inp['l1_ln_mix_g'] = gain(d)
    inp['l1_ln_mix_b'] = bias(d)
    inp['l1_w_gate'] = _normal(next(ks), (N_EXPERTS, d, D_FF_EXPERT), d ** -0.5)
    inp['l1_w_up'] = _normal(next(ks), (N_EXPERTS, d, D_FF_EXPERT), d ** -0.5)
    inp['l1_w_down'] = _normal(next(ks), (N_EXPERTS, D_FF_EXPERT, d), D_FF_EXPERT ** -0.5 * DEEPNORM_BETA)
    inp['l1_ln_ffn_g'] = gain(d)
    inp['l1_ln_ffn_b'] = bias(d)
    return inp


def reference(x, router_w, router_bias,
              l0_in_proj, l0_conv_w, l0_conv_b, l0_conv_ln_g, l0_conv_ln_b,
              l0_lambda_q1, l0_lambda_k1, l0_lambda_q2, l0_lambda_k2, l0_subln_g,
              l0_out_proj, l0_ln_mix_g, l0_ln_mix_b, l0_w_gate, l0_w_up, l0_w_down,
              l0_ln_ffn_g, l0_ln_ffn_b,
              l1_in_proj, l1_pool_w, l1_pool_scale, l1_out_proj, l1_ln_mix_g, l1_ln_mix_b,
              l1_w_gate, l1_w_up, l1_w_down, l1_ln_ffn_g, l1_ln_ffn_b):
    cos, sin = _rope_tables(x.shape[1])
    mixer_params = [
        (l0_in_proj, l0_conv_w, l0_conv_b, l0_conv_ln_g, l0_conv_ln_b,
         l0_lambda_q1, l0_lambda_k1, l0_lambda_q2, l0_lambda_k2, l0_subln_g, l0_out_proj),
        (l1_in_proj, l1_pool_w, l1_pool_scale, l1_out_proj),
    ]
    mix_norms = [(l0_ln_mix_g, l0_ln_mix_b), (l1_ln_mix_g, l1_ln_mix_b)]
    experts = [(l0_w_gate, l0_w_up, l0_w_down), (l1_w_gate, l1_w_up, l1_w_down)]
    ffn_norms = [(l0_ln_ffn_g, l0_ln_ffn_b), (l1_ln_ffn_g, l1_ln_ffn_b)]
    for i in range(DEPTH):
        if i % 2 == 0:
            mix = _even_mixer(x, *mixer_params[i], lam_init=_lambda_init(i), cos=cos, sin=sin)
        else:
            mix = _odd_mixer(x, *mixer_params[i])
        x = _layer_norm(DEEPNORM_ALPHA * x + mix, *mix_norms[i])
        ffn = _grouped_moe(x, router_w, router_bias, *experts[i])
        x = _layer_norm(DEEPNORM_ALPHA * x + ffn, *ffn_norms[i])
    return x
```

```python
import functools
import math

import jax
import jax.numpy as jnp
from jax import lax
from jax.experimental import pallas as pl
from jax.experimental.pallas import tpu as pltpu

F32 = jnp.float32
BF16 = jnp.bfloat16

DEPTH = 2
HEAD_DIM = 64
ROPE_THETA = 10000.0
LN_EPS = 1e-5
RMS_EPS = 1e-5
DEEPNORM_ALPHA = (2 * DEPTH) ** 0.25
CONV_WIDTH = 31
POOL_WINDOWS = (2, 4, 8, 16)
POOL_CH = 128
N_EXPERTS = 16
N_GROUPS = 4
EXPERTS_PER_GROUP = N_EXPERTS // N_GROUPS
ATTN_SCALE = HEAD_DIM ** -0.5

LANES = 128
SUBLANES = 8
MIB = 1024 * 1024
VMEM_LIMIT = 52 * MIB

NEG_BIG = -1e30


def _params(semantics, vmem=VMEM_LIMIT):
    return pltpu.CompilerParams(dimension_semantics=semantics, vmem_limit_bytes=vmem)


def _layer_norm_rows(h, g, b):
    mu = jnp.mean(h, axis=-1, keepdims=True)
    d = h - mu
    var = jnp.mean(d * d, axis=-1, keepdims=True)
    return d * lax.rsqrt(var + LN_EPS) * g + b


def _inproj0_kernel(x_ref, w_ref, cos_ref, sins_ref, cost_ref, sint_ref,
                    glu_ref, qt_ref, k_ref, vt_ref, *, conv_ch, qk_w):
    x = x_ref[0].astype(BF16)
    c0, c1, c2, c3 = conv_ch, 2 * conv_ch, 2 * conv_ch + qk_w, 2 * conv_ch + 2 * qk_w

    a_val = jnp.dot(x, w_ref[:, 0:c0], preferred_element_type=F32)
    a_gate = jnp.dot(x, w_ref[:, c0:c1], preferred_element_type=F32)
    glu_ref[0] = a_val * jax.nn.sigmoid(a_gate)

    dq = jnp.dot(x, w_ref[:, c1:c2], preferred_element_type=F32)
    dqt = dq.T
    cost = cost_ref[...]
    sint = sint_ref[...]
    half = HEAD_DIM // 2
    pieces = []
    for blk in range(qk_w // HEAD_DIM):
        u = dqt[blk * HEAD_DIM:(blk + 1) * HEAD_DIM]
        rot = jnp.concatenate([-u[half:], u[:half]], axis=0)
        pieces.append((u * cost + rot * sint) * ATTN_SCALE)
    qt_ref[0] = jnp.concatenate(pieces, axis=0).astype(BF16)

    dk = jnp.dot(x, w_ref[:, c2:c3], preferred_element_type=F32)
    cos = cos_ref[...]
    sins = sins_ref[...]
    lane = lax.broadcasted_iota(jnp.int32, cos.shape, 1)
    first = (lane & (HEAD_DIM - 1)) < half
    kparts = []
    for h in range(qk_w // LANES):
        u = dk[:, h * LANES:(h + 1) * LANES]
        rot = jnp.where(first, pltpu.roll(u, LANES - half, 1), pltpu.roll(u, half, 1))
        kparts.append(u * cos + rot * sins)
    k_ref[0] = jnp.concatenate(kparts, axis=1).astype(BF16)

    dv = jnp.dot(x, w_ref[:, c3:], preferred_element_type=F32)
    vt_ref[0] = dv.T.astype(BF16)


def _inproj0(x, w_bf, cos128, sins128, cost, sint, *, conv_ch, qk_w, v_w, tm):
    bsz, n_pos, d = x.shape
    n_in = w_bf.shape[1]
    kern = functools.partial(_inproj0_kernel, conv_ch=conv_ch, qk_w=qk_w)
    return pl.pallas_call(
        kern,
        grid=(bsz, n_pos // tm),
        in_specs=[
            pl.BlockSpec((1, tm, d), lambda b, i: (b, i, 0)),
            pl.BlockSpec((d, n_in), lambda b, i: (0, 0)),
            pl.BlockSpec((tm, LANES), lambda b, i: (i, 0)),
            pl.BlockSpec((tm, LANES), lambda b, i: (i, 0)),
            pl.BlockSpec((HEAD_DIM, tm), lambda b, i: (0, i)),
            pl.BlockSpec((HEAD_DIM, tm), lambda b, i: (0, i)),
        ],
        out_specs=[
            pl.BlockSpec((1, tm, conv_ch), lambda b, i: (b, i, 0)),
            pl.BlockSpec((1, qk_w, tm), lambda b, i: (b, 0, i)),
            pl.BlockSpec((1, tm, qk_w), lambda b, i: (b, i, 0)),
            pl.BlockSpec((1, v_w, tm), lambda b, i: (b, 0, i)),
        ],
        out_shape=[
            jax.ShapeDtypeStruct((bsz, n_pos, conv_ch), F32),
            jax.ShapeDtypeStruct((bsz, qk_w, n_pos), BF16),
            jax.ShapeDtypeStruct((bsz, n_pos, qk_w), BF16),
            jax.ShapeDtypeStruct((bsz, v_w, n_pos), BF16),
        ],
        compiler_params=_params(("parallel", "arbitrary")),
        name="inproj0",
    )(x, w_bf, cos128, sins128, cost, sint)


CONV_HALO = 32


def _conv_kernel(a_ref, cw_ref, cb_ref, g_ref, b_ref, o_ref, *, ts):
    i = pl.program_id(1)
    t0 = pl.multiple_of(i * ts, ts)
    h0 = pl.multiple_of(jnp.maximum(t0 - CONV_HALO, 0), CONV_HALO)
    n_ch = a_ref.shape[2]
    outs = []
    for c in range(n_ch // LANES):
        cs = slice(c * LANES, (c + 1) * LANES)
        cur = a_ref[0, pl.ds(t0, ts), cs]
        halo = a_ref[0, pl.ds(h0, CONV_HALO), cs]
        halo = jnp.where(i > 0, halo, 0.0)
        xw = jnp.concatenate([halo, cur], axis=0)
        acc = jnp.zeros((ts, LANES), F32) + cb_ref[:, cs]
        for r in range(SUBLANES):
            xr = xw if r == 0 else xw[r:r + ts + CONV_HALO - SUBLANES]
            for a in range(CONV_HALO // SUBLANES + 1):
                w = SUBLANES * a + r - (CONV_HALO - CONV_WIDTH + 1)
                if 0 <= w < CONV_WIDTH and SUBLANES * a + ts <= xr.shape[0]:
                    acc = acc + xr[SUBLANES * a:SUBLANES * a + ts] * cw_ref[w:w + 1, cs]
        outs.append(acc)
    y = jnp.concatenate(outs, axis=1)
    y = _layer_norm_rows(y, g_ref[...], b_ref[...])
    o_ref[0] = (y * jax.nn.sigmoid(y)).astype(o_ref.dtype)


def _conv_module(a, conv_w, conv_b, ln_g, ln_b, *, ts):
    bsz, n_pos, n_ch = a.shape
    return pl.pallas_call(
        functools.partial(_conv_kernel, ts=ts),
        grid=(bsz, n_pos // ts),
        in_specs=[
            pl.BlockSpec((1, n_pos, n_ch), lambda b, i: (b, 0, 0)),
            pl.BlockSpec((CONV_WIDTH, n_ch), lambda b, i: (0, 0)),
            pl.BlockSpec((1, n_ch), lambda b, i: (0, 0)),
            pl.BlockSpec((1, n_ch), lambda b, i: (0, 0)),
            pl.BlockSpec((1, n_ch), lambda b, i: (0, 0)),
        ],
        out_specs=pl.BlockSpec((1, ts, n_ch), lambda b, i: (b, i, 0)),
        out_shape=jax.ShapeDtypeStruct((bsz, n_pos, n_ch), BF16),
        compiler_params=_params(("parallel", "arbitrary")),
        name="conv_module",
    )(a, conv_w, conv_b.reshape(1, n_ch), ln_g.reshape(1, n_ch), ln_b.reshape(1, n_ch))


def _diff_attn_kernel(lam_ref, qt_ref, k_ref, vt_ref, g_ref, o_ref, *, tq, tk, out_scale):
    i = pl.program_id(2)
    lam = lam_ref[0, 0]
    qt = qt_ref[0]
    zeros = jnp.zeros((HEAD_DIM, tq), BF16)
    qa = jnp.concatenate([qt[:HEAD_DIM], zeros], axis=0)
    qb = jnp.concatenate([zeros, qt[HEAD_DIM:]], axis=0)
    dv = vt_ref.shape[1]

    def online(s, vj, m, l, acc):
        m_new = jnp.maximum(m, jnp.max(s, axis=0, keepdims=True))
        alpha = jnp.exp(m - m_new)
        p = jnp.exp(s - m_new)
        l_new = alpha * l + jnp.sum(p, axis=0, keepdims=True)
        acc_new = alpha * acc + jnp.dot(vj, p.astype(BF16), preferred_element_type=F32)
        return m_new, l_new, acc_new

    def block(j, carry, masked):
        m1, l1, acc1, m2, l2, acc2 = carry
        k0 = pl.multiple_of(j * tk, tk)
        kj = k_ref[0, pl.ds(k0, tk), :]
        vj = vt_ref[0, :, pl.ds(k0, tk)]
        s1 = jnp.dot(kj, qa, preferred_element_type=F32)
        s2 = jnp.dot(kj, qb, preferred_element_type=F32)
        if masked:
            key = k0 + lax.broadcasted_iota(jnp.int32, (tk, tq), 0)
            qpos = i * tq + lax.broadcasted_iota(jnp.int32, (tk, tq), 1)
            keep = key <= qpos
            s1 = jnp.where(keep, s1, NEG_BIG)
            s2 = jnp.where(keep, s2, NEG_BIG)
        m1, l1, acc1 = online(s1, vj, m1, l1, acc1)
        m2, l2, acc2 = online(s2, vj, m2, l2, acc2)
        return m1, l1, acc1, m2, l2, acc2

    row = jnp.full((1, tq), NEG_BIG, F32)
    zrow = jnp.zeros((1, tq), F32)
    zacc = jnp.zeros((dv, tq), F32)
    carry = (row, zrow, zacc, row, zrow, zacc)
    carry = lax.fori_loop(0, i, lambda j, c: block(j, c, False), carry)
    m1, l1, acc1, m2, l2, acc2 = block(i, carry, True)

    o = acc1 / l1 - lam * (acc2 / l2)
    o = o * lax.rsqrt(jnp.mean(o * o, axis=0, keepdims=True) + RMS_EPS)
    o = o * g_ref[...] * out_scale
    o_ref[0] = o.T.astype(o_ref.dtype)


def _diff_attention(qt, k, vt, lam, subln_g, *, n_heads, lam_init, tq, tk):
    bsz, _, n_pos = qt.shape
    dv = vt.shape[1] // n_heads
    kern = functools.partial(_diff_attn_kernel, tq=tq, tk=tk, out_scale=1.0 - lam_init)
    return pl.pallas_call(
        kern,
        grid=(bsz, n_heads, n_pos // tq),
        in_specs=[
            pl.BlockSpec(memory_space=pltpu.SMEM),
            pl.BlockSpec((1, 2 * HEAD_DIM, tq), lambda b, h, i: (b, h, i)),
            pl.BlockSpec((1, n_pos, 2 * HEAD_DIM), lambda b, h, i: (b, 0, h)),
            pl.BlockSpec((1, dv, n_pos), lambda b, h, i: (b, h, 0)),
            pl.BlockSpec((dv, 1), lambda b, h, i: (0, 0)),
        ],
        out_specs=pl.BlockSpec((1, tq, dv), lambda b, h, i: (b, i, h)),
        out_shape=jax.ShapeDtypeStruct((bsz, n_pos, n_heads * dv), BF16),
        compiler_params=_params(("parallel", "parallel", "arbitrary")),
        name="diff_attention",
    )(lam.reshape(1, 1), qt, k, vt, subln_g.astype(F32).reshape(dv, 1))


def _masked_top2(vals, member, row):
    big = N_EXPERTS
    v = jnp.where(member, vals, -jnp.inf)
    v1 = jnp.max(v, axis=0, keepdims=True)
    i1 = jnp.min(jnp.where(v == v1, row, big), axis=0, keepdims=True)
    v = jnp.where(row == i1, -jnp.inf, v)
    v2 = jnp.max(v, axis=0, keepdims=True)
    i2 = jnp.min(jnp.where(v == v2, row, big), axis=0, keepdims=True)
    return v1, i1, v2, i2


def _top2_route(aff, sel):
    row = lax.broadcasted_iota(jnp.int32, sel.shape, 0)
    group = lax.shift_right_logical(row, EXPERTS_PER_GROUP.bit_length() - 1)
    gbest = None
    for g in range(N_GROUPS):
        v1, _, v2, _ = _masked_top2(sel, group == g, row)
        score = v1 + v2
        if gbest is None:
            gbest, gval = jnp.zeros_like(score, dtype=jnp.int32), score
        else:
            better = score > gval
            gbest = jnp.where(better, g, gbest)
            gval = jnp.where(better, score, gval)
    _, e1, _, e2 = _masked_top2(sel, group == gbest, row)
    a1 = jnp.sum(jnp.where(row == e1, aff, 0.0), axis=0, keepdims=True)
    a2 = jnp.sum(jnp.where(row == e2, aff, 0.0), axis=0, keepdims=True)
    denom = a1 + a2
    return e1, e2, a1 / denom, a2 / denom


def _outproj_kernel(pa_ref, pb_ref, wa_ref, wb_ref, x_ref, g_ref, b_ref, rw_ref, rb_ref,
                    y_ref, yb_ref, comb_ref):
    mix = jnp.dot(pa_ref[...], wa_ref[...], preferred_element_type=F32)
    mix = mix + jnp.dot(pb_ref[...], wb_ref[...], preferred_element_type=F32)
    y = _layer_norm_rows(DEEPNORM_ALPHA * x_ref[...] + mix, g_ref[...], b_ref[...])
    y_ref[...] = y
    yb_ref[...] = y.astype(BF16)
    logits = lax.dot_general(rw_ref[...], y, (((1,), (1,)), ((), ())),
                             precision=lax.Precision.HIGHEST, preferred_element_type=F32)
    aff = jax.nn.sigmoid(logits)
    e1, e2, g1, g2 = _top2_route(aff, aff + rb_ref[...])
    erow = lax.broadcasted_iota(jnp.int32, (LANES, y.shape[0]), 0)
    comb_t = jnp.where(erow == e1, g1, 0.0) + jnp.where(erow == e2, g2, 0.0)
    comb_ref[...] = comb_t.T


def _outproj_ln_route(pa, pb, w_out, x, ln_g, ln_b, router_w, router_bias, *, tm):
    n_tok, d = x.shape
    ka, kb = pa.shape[1], pb.shape[1]
    wa = w_out[:ka].astype(BF16)
    wb = w_out[ka:].astype(BF16)
    rows = lambda i: (i, 0)
    fixed = lambda i: (0, 0)
    return pl.pallas_call(
        _outproj_kernel,
        grid=(n_tok // tm,),
        in_specs=[
            pl.BlockSpec((tm, ka), rows),
            pl.BlockSpec((tm, kb), rows),
            pl.BlockSpec((ka, d), fixed),
            pl.BlockSpec((kb, d), fixed),
            pl.BlockSpec((tm, d), rows),
            pl.BlockSpec((1, d), fixed),
            pl.BlockSpec((1, d), fixed),
            pl.BlockSpec((N_EXPERTS, d), fixed),
            pl.BlockSpec((N_EXPERTS, 1), fixed),
        ],
        out_specs=[
            pl.BlockSpec((tm, d), rows),
            pl.BlockSpec((tm, d), rows),
            pl.BlockSpec((tm, LANES), rows),
        ],
        out_shape=[
            jax.ShapeDtypeStruct((n_tok, d), F32),
            jax.ShapeDtypeStruct((n_tok, d), BF16),
            jax.ShapeDtypeStruct((n_tok, LANES), F32),
        ],
        compiler_params=_params(("parallel",)),
        name="outproj_ln_route",
    )(pa, pb, wa, wb, x, ln_g.reshape(1, d), ln_b.reshape(1, d),
      router_w.astype(F32).T, router_bias.astype(F32).reshape(N_EXPERTS, 1))


def _moe_dense_kernel(xb_ref, comb_ref, wg_ref, wu_ref, wd_ref, xres_ref, g_ref, b_ref,
                      o_ref, acc_ref):
    e = pl.program_id(1)

    @pl.when(e == 0)
    def _():
        acc_ref[...] = jnp.zeros_like(acc_ref)

    x = xb_ref[...]
    hg = jnp.dot(x, wg_ref[0], preferred_element_type=F32)
    hu = jnp.dot(x, wu_ref[0], preferred_element_type=F32)
    comb = comb_ref[...]
    lane = lax.broadcasted_iota(jnp.int32, comb.shape, 1)
    c = jnp.sum(jnp.where(lane == e, comb, 0.0), axis=1, keepdims=True)
    h = (hg * jax.nn.sigmoid(hg)) * hu * c
    acc_ref[...] += jnp.dot(h.astype(BF16), wd_ref[0], preferred_element_type=F32)

    @pl.when(e == pl.num_programs(1) - 1)
    def _():
        o_ref[...] = _layer_norm_rows(DEEPNORM_ALPHA * xres_ref[...] + acc_ref[...],
                                      g_ref[...], b_ref[...])


def _moe_dense(xb, comb, w_gate, w_up, w_down, xres, ln_g, ln_b, *, tm):
    n_tok, d = xres.shape
    n_exp, _, dff = w_gate.shape
    rows = lambda i, e: (i, 0)
    fixed = lambda i, e: (0, 0)
    return pl.pallas_call(
        _moe_dense_kernel,
        grid=(n_tok // tm, n_exp),
        in_specs=[
            pl.BlockSpec((tm, d), rows),
            pl.BlockSpec((tm, LANES), rows),
            pl.BlockSpec((1, d, dff), lambda i, e: (e, 0, 0)),
            pl.BlockSpec((1, d, dff), lambda i, e: (e, 0, 0)),
            pl.BlockSpec((1, dff, d), lambda i, e: (e, 0, 0)),
            pl.BlockSpec((tm, d), rows),
            pl.BlockSpec((1, d), fixed),
            pl.BlockSpec((1, d), fixed),
        ],
        out_specs=pl.BlockSpec((tm, d), rows),
        out_shape=jax.ShapeDtypeStruct((n_tok, d), F32),
        scratch_shapes=[pltpu.VMEM((tm, d), F32)],
        compiler_params=_params(("parallel", "arbitrary")),
        name="moe_dense",
    )(xb, comb, w_gate.astype(BF16), w_up.astype(BF16), w_down.astype(BF16), xres,
      ln_g.reshape(1, d), ln_b.reshape(1, d))


def _inproj1_kernel(x_ref, w_ref, qt_ref, k_ref, vt_ref, u_ref, *, sb_w):
    x = x_ref[0].astype(BF16)
    q = jnp.dot(x, w_ref[:, 0:sb_w], preferred_element_type=F32) * ATTN_SCALE
    qt_ref[0] = q.T.astype(BF16)
    k_ref[0] = jnp.dot(x, w_ref[:, sb_w:2 * sb_w], preferred_element_type=F32).astype(BF16)
    v = jnp.dot(x, w_ref[:, 2 * sb_w:3 * sb_w], preferred_element_type=F32)
    vt_ref[0] = v.T.astype(BF16)
    u_ref[0] = jnp.dot(x, w_ref[:, 3 * sb_w:], preferred_element_type=F32)


def _inproj1(x, w_bf, *, sb_w, tm):
    bsz, n_pos, d = x.shape
    n_in = w_bf.shape[1]
    pool_w = n_in - 3 * sb_w
    return pl.pallas_call(
        functools.partial(_inproj1_kernel, sb_w=sb_w),
        grid=(bsz, n_pos // tm),
        in_specs=[
            pl.BlockSpec((1, tm, d), lambda b, i: (b, i, 0)),
            pl.BlockSpec((d, n_in), lambda b, i: (0, 0)),
        ],
        out_specs=[
            pl.BlockSpec((1, sb_w, tm), lambda b, i: (b, 0, i)),
            pl.BlockSpec((1, tm, sb_w), lambda b, i: (b, i, 0)),
            pl.BlockSpec((1, sb_w, tm), lambda b, i: (b, 0, i)),
            pl.BlockSpec((1, tm, pool_w), lambda b, i: (b, i, 0)),
        ],
        out_shape=[
            jax.ShapeDtypeStruct((bsz, sb_w, n_pos), BF16),
            jax.ShapeDtypeStruct((bsz, n_pos, sb_w), BF16),
            jax.ShapeDtypeStruct((bsz, sb_w, n_pos), BF16),
            jax.ShapeDtypeStruct((bsz, n_pos, pool_w), F32),
        ],
        compiler_params=_params(("parallel", "arbitrary")),
        name="inproj1",
    )(x, w_bf)


def _sb_attn_kernel(qt_ref, k_ref, vt_ref, o_ref, *, tq, tk):
    i = pl.program_id(2)
    qt = qt_ref[0]
    zeros = jnp.zeros((HEAD_DIM, tq), BF16)
    upper = (lax.broadcasted_iota(jnp.int32, (tk, tk), 1)
             > lax.broadcasted_iota(jnp.int32, (tk, tk), 0)).astype(BF16)

    outs = []
    for hh in range(2):
        if hh == 0:
            qpad = jnp.concatenate([qt[:HEAD_DIM], zeros], axis=0)
        else:
            qpad = jnp.concatenate([zeros, qt[HEAD_DIM:]], axis=0)

        def block(j, carry, masked, hh=hh, qpad=qpad):
            acc, later = carry
            k0 = pl.multiple_of(j * tk, tk)
            kj = k_ref[0, pl.ds(k0, tk), :]
            vj = vt_ref[0, hh * HEAD_DIM:(hh + 1) * HEAD_DIM, pl.ds(k0, tk)]
            z = jnp.dot(kj, qpad, preferred_element_type=F32)
            soft = jnp.log1p(jnp.exp(-jnp.abs(z)))
            log_stay = jnp.minimum(-z, 0.0) - soft
            log_beta = jnp.minimum(z, 0.0) - soft
            if masked:
                key = k0 + lax.broadcasted_iota(jnp.int32, (tk, tq), 0)
                qpos = i * tq + lax.broadcasted_iota(jnp.int32, (tk, tq), 1)
                keep = key < qpos
                log_stay = jnp.where(keep, log_stay, 0.0)
            hi = log_stay.astype(BF16)
            lo = (log_stay - hi.astype(F32)).astype(BF16)
            suffix = (jnp.dot(upper, hi, preferred_element_type=F32)
                      + jnp.dot(upper, lo, preferred_element_type=F32))
            a = jnp.exp(log_beta + suffix + later)
            if masked:
                a = jnp.where(keep, a, 0.0)
            acc = acc + jnp.dot(vj, a.astype(BF16), preferred_element_type=F32)
            later = later + jnp.sum(log_stay, axis=0, keepdims=True)
            return acc, later

        carry = (jnp.zeros((HEAD_DIM, tq), F32), jnp.zeros((1, tq), F32))
        carry = block(i, carry, True)
        acc, _ = lax.fori_loop(0, i, lambda t, c: block(i - 1 - t, c, False), carry)
        outs.append(acc)
    o_ref[0] = jnp.concatenate(outs, axis=0).T.astype(o_ref.dtype)


def _sb_attention(qt, k, vt, *, tq, tk):
    bsz, sb_w, n_pos = qt.shape
    return pl.pallas_call(
        functools.partial(_sb_attn_kernel, tq=tq, tk=tk),
        grid=(bsz, sb_w // LANES, n_pos // tq),
        in_specs=[
            pl.BlockSpec((1, LANES, tq), lambda b, h, i: (b, h, i)),
            pl.BlockSpec((1, n_pos, LANES), lambda b, h, i: (b, 0, h)),
            pl.BlockSpec((1, LANES, n_pos), lambda b, h, i: (b, h, 0)),
        ],
        out_specs=pl.BlockSpec((1, tq, LANES), lambda b, h, i: (b, i, h)),
        out_shape=jax.ShapeDtypeStruct((bsz, n_pos, sb_w), BF16),
        compiler_params=_params(("parallel", "parallel", "arbitrary")),
        name="sb_attention",
    )(qt, k, vt)


POOL_HALO = 32


def _pool_kernel(u_ref, pw_ref, ps_ref, o_ref, *, ts):
    i = pl.program_id(1)
    t0 = pl.multiple_of(i * ts, ts)
    h0 = pl.multiple_of(jnp.maximum(t0 - POOL_HALO, 0), POOL_HALO)
    tpos = t0 + lax.broadcasted_iota(jnp.int32, (ts, 1), 0)
    outs = []
    for g, win in enumerate(POOL_WINDOWS):
        cs = slice(g * POOL_CH, (g + 1) * POOL_CH)
        cur = u_ref[0, pl.ds(t0, ts), cs]
        halo = u_ref[0, pl.ds(h0, POOL_HALO), cs]
        halo = jnp.where(i > 0, halo, 0.0)
        run = jnp.concatenate([halo, cur], axis=0)
        span = 1
        while span < win:
            n = run.shape[0] - span
            run = run[span:span + n] + run[0:n]
            span *= 2
        off = POOL_HALO - (win - 1)
        wsum = run[off:off + ts]
        count = jnp.minimum(tpos + 1, win).astype(F32)
        pooled = wsum / count - cur
        mixed = jnp.dot(pooled.astype(BF16), pw_ref[g], preferred_element_type=F32)
        outs.append(mixed * ps_ref[:, cs])
    o_ref[0] = jnp.concatenate(outs, axis=1).astype(o_ref.dtype)


def _pool_module(u, pool_w, pool_scale, *, ts):
    bsz, n_pos, n_ch = u.shape
    n_g = len(POOL_WINDOWS)
    return pl.pallas_call(
        functools.partial(_pool_kernel, ts=ts),
        grid=(bsz, n_pos // ts),
        in_specs=[
            pl.BlockSpec((1, n_pos, n_ch), lambda b, i: (b, 0, 0)),
            pl.BlockSpec((n_g, POOL_CH, POOL_CH), lambda b, i: (0, 0, 0)),
            pl.BlockSpec((1, n_ch), lambda b, i: (0, 0)),
        ],
        out_specs=pl.BlockSpec((1, ts, n_ch), lambda b, i: (b, i, 0)),
        out_shape=jax.ShapeDtypeStruct((bsz, n_pos, n_ch), BF16),
        compiler_params=_params(("parallel", "arbitrary")),
        name="pool_module",
    )(u, pool_w.astype(BF16), pool_scale.astype(F32).reshape(1, n_ch))


def _rope_tables(n_pos):
    inv = ROPE_THETA ** (-jnp.arange(0, HEAD_DIM, 2, dtype=F32) / HEAD_DIM)
    ang = jnp.arange(n_pos, dtype=F32)[:, None] * inv[None, :]
    ang = jnp.concatenate([ang, ang], axis=-1)
    return jnp.cos(ang), jnp.sin(ang)


def _lambda_init(layer_idx):
    return 0.8 - 0.6 * math.exp(-0.3 * layer_idx)


def kernel(x, router_w, router_bias, l0_in_proj, l0_conv_w, l0_conv_b, l0_conv_ln_g, l0_conv_ln_b, l0_lambda_q1, l0_lambda_k1, l0_lambda_q2, l0_lambda_k2, l0_subln_g, l0_out_proj, l0_ln_mix_g, l0_ln_mix_b, l0_w_gate, l0_w_up, l0_w_down, l0_ln_ffn_g, l0_ln_ffn_b, l1_in_proj, l1_pool_w, l1_pool_scale, l1_out_proj, l1_ln_mix_g, l1_ln_mix_b, l1_w_gate, l1_w_up, l1_w_down, l1_ln_ffn_g, l1_ln_ffn_b):
    bsz, n_pos, d = x.shape
    n_tok = bsz * n_pos
    x = x.astype(F32)

    conv_ch = l0_conv_w.shape[1]
    dv = l0_subln_g.shape[0]
    n_diff = (l0_out_proj.shape[0] - conv_ch) // dv
    qk_w = n_diff * 2 * HEAD_DIM
    v_w = n_diff * dv

    tm = min(512, n_pos)
    tq = min(256, n_pos)
    ts = min(256, n_pos)
    tmoe = min(1024, n_tok)

    cos, sin = _rope_tables(n_pos)
    sign = jnp.where(jnp.arange(HEAD_DIM) < HEAD_DIM // 2, -1.0, 1.0).astype(F32)
    cos128 = jnp.concatenate([cos, cos], axis=-1)
    sins128 = jnp.concatenate([sin * sign, sin * sign], axis=-1)
    cost = cos.T
    sint = sin.T

    glu, qt, k, vt = _inproj0(x, l0_in_proj.astype(BF16), cos128, sins128, cost, sint,
                              conv_ch=conv_ch, qk_w=qk_w, v_w=v_w, tm=tm)
    a = _conv_module(glu, l0_conv_w.astype(F32), l0_conv_b.astype(F32),
                     l0_conv_ln_g.astype(F32), l0_conv_ln_b.astype(F32), ts=ts)
    lam_init = _lambda_init(0)
    lam = (jnp.exp(jnp.sum(l0_lambda_q1.astype(F32) * l0_lambda_k1.astype(F32)))
           - jnp.exp(jnp.sum(l0_lambda_q2.astype(F32) * l0_lambda_k2.astype(F32))) + lam_init)
    o = _diff_attention(qt, k, vt, lam, l0_subln_g, n_heads=n_diff, lam_init=lam_init,
                        tq=tq, tk=tq)
    xf = x.reshape(n_tok, d)
    y, yb, comb = _outproj_ln_route(a.reshape(n_tok, conv_ch), o.reshape(n_tok, v_w),
                                    l0_out_proj, xf, l0_ln_mix_g.astype(F32),
                                    l0_ln_mix_b.astype(F32), router_w, router_bias, tm=tm)
    xf = _moe_dense(yb, comb, l0_w_gate, l0_w_up, l0_w_down, y,
                    l0_ln_ffn_g.astype(F32), l0_ln_ffn_b.astype(F32), tm=tmoe)

    sb_w = l1_in_proj.shape[1] - len(POOL_WINDOWS) * POOL_CH
    sb_w = sb_w // 3
    qt, k, vt, u = _inproj1(xf.reshape(bsz, n_pos, d), l1_in_proj.astype(BF16), sb_w=sb_w, tm=tm)
    o = _sb_attention(qt, k, vt, tq=tq, tk=tq)
    dpool = _pool_module(u, l1_pool_w, l1_pool_scale, ts=ts)
    y, yb, comb = _outproj_ln_route(o.reshape(n_tok, sb_w), dpool.reshape(n_tok, -1),
                                    l1_out_proj, xf, l1_ln_mix_g.astype(F32),
                                    l1_ln_mix_b.astype(F32), router_w, router_bias, tm=tm)
    xf = _moe_dense(yb, comb, l1_w_gate, l1_w_up, l1_w_down, y,
                    l1_ln_ffn_g.astype(F32), l1_ln_ffn_b.astype(F32), tm=tmoe)
    return xf.reshape(bsz, n_pos, d)
```

```python
import functools
import math

import jax
import jax.numpy as jnp
from jax import lax
from jax.experimental import pallas as pl
from jax.experimental.pallas import tpu as pltpu

F32 = jnp.float32
BF16 = jnp.bfloat16

DEPTH = 2
HEAD_DIM = 64
ROPE_THETA = 10000.0
LN_EPS = 1e-5
RMS_EPS = 1e-5
DEEPNORM_ALPHA = (2 * DEPTH) ** 0.25
CONV_WIDTH = 31
POOL_WINDOWS = (2, 4, 8, 16)
POOL_CH = 128
N_EXPERTS = 16
N_GROUPS = 4
EXPERTS_PER_GROUP = N_EXPERTS // N_GROUPS
ATTN_SCALE = HEAD_DIM ** -0.5
LOG2E = math.log2(math.e)

LANES = 128
SUBLANES = 8
MIB = 1024 * 1024
VMEM_LIMIT = 52 * MIB

NEG_BIG = -1e30
SB_SCAN_ROWS = 128
SB_UNDERFLOW = 110.0


def _params(semantics, vmem=VMEM_LIMIT):
    return pltpu.CompilerParams(dimension_semantics=semantics, vmem_limit_bytes=vmem)


def _layer_norm_rows(h, g, b):
    mu = jnp.mean(h, axis=-1, keepdims=True)
    d = h - mu
    var = jnp.mean(d * d, axis=-1, keepdims=True)
    return d * lax.rsqrt(var + LN_EPS) * g + b


def _inproj0_kernel(x_ref, w_ref, cos_ref, sins_ref, cost_ref, sint_ref,
                    glu_ref, qt_ref, k_ref, vt_ref, *, conv_ch, qk_w):
    x = x_ref[0].astype(BF16)
    c0, c1, c2, c3 = conv_ch, 2 * conv_ch, 2 * conv_ch + qk_w, 2 * conv_ch + 2 * qk_w

    a_val = jnp.dot(x, w_ref[:, 0:c0], preferred_element_type=F32)
    a_gate = jnp.dot(x, w_ref[:, c0:c1], preferred_element_type=F32)
    glu_ref[0] = a_val * jax.nn.sigmoid(a_gate)

    dq = jnp.dot(x, w_ref[:, c1:c2], preferred_element_type=F32)
    dqt = dq.T
    cost = cost_ref[...]
    sint = sint_ref[...]
    half = HEAD_DIM // 2
    pieces = []
    for blk in range(qk_w // HEAD_DIM):
        u = dqt[blk * HEAD_DIM:(blk + 1) * HEAD_DIM]
        rot = jnp.concatenate([-u[half:], u[:half]], axis=0)
        pieces.append((u * cost + rot * sint) * (ATTN_SCALE * LOG2E))
    qt_ref[0] = jnp.concatenate(pieces, axis=0).astype(BF16)

    dk = jnp.dot(x, w_ref[:, c2:c3], preferred_element_type=F32)
    cos = cos_ref[...]
    sins = sins_ref[...]
    lane = lax.broadcasted_iota(jnp.int32, cos.shape, 1)
    first = (lane & (HEAD_DIM - 1)) < half
    kparts = []
    for h in range(qk_w // LANES):
        u = dk[:, h * LANES:(h + 1) * LANES]
        rot = jnp.where(first, pltpu.roll(u, LANES - half, 1), pltpu.roll(u, half, 1))
        kparts.append(u * cos + rot * sins)
    k_ref[0] = jnp.concatenate(kparts, axis=1).astype(BF16)

    dv = jnp.dot(x, w_ref[:, c3:], preferred_element_type=F32)
    vt_ref[0] = dv.T.astype(BF16)


def _inproj0(x, w_bf, cos128, sins128, cost, sint, *, conv_ch, qk_w, v_w, tm):
    bsz, n_pos, d = x.shape
    n_in = w_bf.shape[1]
    kern = functools.partial(_inproj0_kernel, conv_ch=conv_ch, qk_w=qk_w)
    return pl.pallas_call(
        kern,
        grid=(bsz, n_pos // tm),
        in_specs=[
            pl.BlockSpec((1, tm, d), lambda b, i: (b, i, 0)),
            pl.BlockSpec((d, n_in), lambda b, i: (0, 0)),
            pl.BlockSpec((tm, LANES), lambda b, i: (i, 0)),
            pl.BlockSpec((tm, LANES), lambda b, i: (i, 0)),
            pl.BlockSpec((HEAD_DIM, tm), lambda b, i: (0, i)),
            pl.BlockSpec((HEAD_DIM, tm), lambda b, i: (0, i)),
        ],
        out_specs=[
            pl.BlockSpec((1, tm, conv_ch), lambda b, i: (b, i, 0)),
            pl.BlockSpec((1, qk_w, tm), lambda b, i: (b, 0, i)),
            pl.BlockSpec((1, tm, qk_w), lambda b, i: (b, i, 0)),
            pl.BlockSpec((1, v_w, tm), lambda b, i: (b, 0, i)),
        ],
        out_shape=[
            jax.ShapeDtypeStruct((bsz, n_pos, conv_ch), F32),
            jax.ShapeDtypeStruct((bsz, qk_w, n_pos), BF16),
            jax.ShapeDtypeStruct((bsz, n_pos, qk_w), BF16),
            jax.ShapeDtypeStruct((bsz, v_w, n_pos), BF16),
        ],
        compiler_params=_params(("parallel", "arbitrary")),
        name="inproj0",
    )(x, w_bf, cos128, sins128, cost, sint)


CONV_HALO = 32


def _conv_kernel(a_ref, cw_ref, cb_ref, g_ref, b_ref, o_ref, *, ts):
    i = pl.program_id(1)
    t0 = pl.multiple_of(i * ts, ts)
    h0 = pl.multiple_of(jnp.maximum(t0 - CONV_HALO, 0), CONV_HALO)
    n_ch = a_ref.shape[2]
    outs = []
    for c in range(n_ch // LANES):
        cs = slice(c * LANES, (c + 1) * LANES)
        cur = a_ref[0, pl.ds(t0, ts), cs]
        halo = a_ref[0, pl.ds(h0, CONV_HALO), cs]
        halo = jnp.where(i > 0, halo, 0.0)
        xw = jnp.concatenate([halo, cur], axis=0)
        acc = jnp.zeros((ts, LANES), F32) + cb_ref[:, cs]
        for r in range(SUBLANES):
            xr = xw if r == 0 else xw[r:r + ts + CONV_HALO - SUBLANES]
            for a in range(CONV_HALO // SUBLANES + 1):
                w = SUBLANES * a + r - (CONV_HALO - CONV_WIDTH + 1)
                if 0 <= w < CONV_WIDTH and SUBLANES * a + ts <= xr.shape[0]:
                    acc = acc + xr[SUBLANES * a:SUBLANES * a + ts] * cw_ref[w:w + 1, cs]
        outs.append(acc)
    y = jnp.concatenate(outs, axis=1)
    y = _layer_norm_rows(y, g_ref[...], b_ref[...])
    o_ref[0] = (y * jax.nn.sigmoid(y)).astype(o_ref.dtype)


def _conv_module(a, conv_w, conv_b, ln_g, ln_b, *, ts):
    bsz, n_pos, n_ch = a.shape
    return pl.pallas_call(
        functools.partial(_conv_kernel, ts=ts),
        grid=(bsz, n_pos // ts),
        in_specs=[
            pl.BlockSpec((1, n_pos, n_ch), lambda b, i: (b, 0, 0)),
            pl.BlockSpec((CONV_WIDTH, n_ch), lambda b, i: (0, 0)),
            pl.BlockSpec((1, n_ch), lambda b, i: (0, 0)),
            pl.BlockSpec((1, n_ch), lambda b, i: (0, 0)),
            pl.BlockSpec((1, n_ch), lambda b, i: (0, 0)),
        ],
        out_specs=pl.BlockSpec((1, ts, n_ch), lambda b, i: (b, i, 0)),
        out_shape=jax.ShapeDtypeStruct((bsz, n_pos, n_ch), BF16),
        compiler_params=_params(("parallel", "arbitrary")),
        name="conv_module",
    )(a, conv_w, conv_b.reshape(1, n_ch), ln_g.reshape(1, n_ch), ln_b.reshape(1, n_ch))


def _diff_attn_kernel(lam_ref, qt_ref, k_ref, vt_ref, g_ref, o_ref, *, tq, tk, out_scale):
    i = pl.program_id(2)
    lam = lam_ref[0, 0]
    qt = qt_ref[0]
    zeros = jnp.zeros((HEAD_DIM, tq), BF16)
    qa = jnp.concatenate([qt[:HEAD_DIM], zeros], axis=0)
    qb = jnp.concatenate([zeros, qt[HEAD_DIM:]], axis=0)
    dv = vt_ref.shape[1]

    def online(s, vj, m, l, acc):
        m_new = jnp.maximum(m, jnp.max(s, axis=0, keepdims=True))
        alpha = jnp.exp2(m - m_new)
        p = jnp.exp2(s - m_new)
        l_new = alpha * l + jnp.sum(p, axis=0, keepdims=True)
        acc_new = alpha * acc + jnp.dot(vj, p.astype(BF16), preferred_element_type=F32)
        return m_new, l_new, acc_new

    def block(j, carry, masked):
        m1, l1, acc1, m2, l2, acc2 = carry
        k0 = pl.multiple_of(j * tk, tk)
        kj = k_ref[0, pl.ds(k0, tk), :]
        vj = vt_ref[0, :, pl.ds(k0, tk)]
        s1 = jnp.dot(kj, qa, preferred_element_type=F32)
        s2 = jnp.dot(kj, qb, preferred_element_type=F32)
        if masked:
            key = k0 + lax.broadcasted_iota(jnp.int32, (tk, tq), 0)
            qpos = i * tq + lax.broadcasted_iota(jnp.int32, (tk, tq), 1)
            keep = key <= qpos
            s1 = jnp.where(keep, s1, NEG_BIG)
            s2 = jnp.where(keep, s2, NEG_BIG)
        m1, l1, acc1 = online(s1, vj, m1, l1, acc1)
        m2, l2, acc2 = online(s2, vj, m2, l2, acc2)
        return m1, l1, acc1, m2, l2, acc2

    row = jnp.full((1, tq), NEG_BIG, F32)
    zrow = jnp.zeros((1, tq), F32)
    zacc = jnp.zeros((dv, tq), F32)
    carry = (row, zrow, zacc, row, zrow, zacc)
    carry = lax.fori_loop(0, i, lambda j, c: block(j, c, False), carry)
    m1, l1, acc1, m2, l2, acc2 = block(i, carry, True)

    o = acc1 / l1 - lam * (acc2 / l2)
    o = o * lax.rsqrt(jnp.mean(o * o, axis=0, keepdims=True) + RMS_EPS)
    o = o * g_ref[...] * out_scale
    o_ref[0] = o.T.astype(o_ref.dtype)


def _diff_attention(qt, k, vt, lam, subln_g, *, n_heads, lam_init, tq, tk):
    bsz, _, n_pos = qt.shape
    dv = vt.shape[1] // n_heads
    kern = functools.partial(_diff_attn_kernel, tq=tq, tk=tk, out_scale=1.0 - lam_init)
    return pl.pallas_call(
        kern,
        grid=(bsz, n_heads, n_pos // tq),
        in_specs=[
            pl.BlockSpec(memory_space=pltpu.SMEM),
            pl.BlockSpec((1, 2 * HEAD_DIM, tq), lambda b, h, i: (b, h, i)),
            pl.BlockSpec((1, n_pos, 2 * HEAD_DIM), lambda b, h, i: (b, 0, h)),
            pl.BlockSpec((1, dv, n_pos), lambda b, h, i: (b, h, 0)),
            pl.BlockSpec((dv, 1), lambda b, h, i: (0, 0)),
        ],
        out_specs=pl.BlockSpec((1, tq, dv), lambda b, h, i: (b, i, h)),
        out_shape=jax.ShapeDtypeStruct((bsz, n_pos, n_heads * dv), BF16),
        compiler_params=_params(("parallel", "parallel", "arbitrary")),
        name="diff_attention",
    )(lam.reshape(1, 1), qt, k, vt, subln_g.astype(F32).reshape(dv, 1))


def _masked_top2(vals, member, row):
    big = N_EXPERTS
    v = jnp.where(member, vals, -jnp.inf)
    v1 = jnp.max(v, axis=0, keepdims=True)
    i1 = jnp.min(jnp.where(v == v1, row, big), axis=0, keepdims=True)
    v = jnp.where(row == i1, -jnp.inf, v)
    v2 = jnp.max(v, axis=0, keepdims=True)
    i2 = jnp.min(jnp.where(v == v2, row, big), axis=0, keepdims=True)
    return v1, i1, v2, i2


def _top2_route(aff, sel):
    row = lax.broadcasted_iota(jnp.int32, sel.shape, 0)
    group = lax.shift_right_logical(row, EXPERTS_PER_GROUP.bit_length() - 1)
    gbest = None
    for g in range(N_GROUPS):
        v1, _, v2, _ = _masked_top2(sel, group == g, row)
        score = v1 + v2
        if gbest is None:
            gbest, gval = jnp.zeros_like(score, dtype=jnp.int32), score
        else:
            better = score > gval
            gbest = jnp.where(better, g, gbest)
            gval = jnp.where(better, score, gval)
    _, e1, _, e2 = _masked_top2(sel, group == gbest, row)
    a1 = jnp.sum(jnp.where(row == e1, aff, 0.0), axis=0, keepdims=True)
    a2 = jnp.sum(jnp.where(row == e2, aff, 0.0), axis=0, keepdims=True)
    denom = a1 + a2
    return e1, e2, a1 / denom, a2 / denom


def _outproj_kernel(pa_ref, pb_ref, wa_ref, wb_ref, x_ref, g_ref, b_ref, rw_ref, rb_ref,
                    y_ref, yb_ref, comb_ref):
    mix = jnp.dot(pa_ref[...], wa_ref[...], preferred_element_type=F32)
    mix = mix + jnp.dot(pb_ref[...], wb_ref[...], preferred_element_type=F32)
    y = _layer_norm_rows(DEEPNORM_ALPHA * x_ref[...] + mix, g_ref[...], b_ref[...])
    y_ref[...] = y
    yb_ref[...] = y.astype(BF16)
    logits = lax.dot_general(rw_ref[...], y, (((1,), (1,)), ((), ())),
                             precision=lax.Precision.HIGHEST, preferred_element_type=F32)
    aff = jax.nn.sigmoid(logits)
    e1, e2, g1, g2 = _top2_route(aff, aff + rb_ref[...])
    erow = lax.broadcasted_iota(jnp.int32, (LANES, y.shape[0]), 0)
    comb_t = jnp.where(erow == e1, g1, 0.0) + jnp.where(erow == e2, g2, 0.0)
    comb_ref[...] = comb_t.T


def _outproj_ln_route(pa, pb, w_out, x, ln_g, ln_b, router_w, router_bias, *, tm):
    n_tok, d = x.shape
    ka, kb = pa.shape[1], pb.shape[1]
    wa = w_out[:ka].astype(BF16)
    wb = w_out[ka:].astype(BF16)
    rows = lambda i: (i, 0)
    fixed = lambda i: (0, 0)
    return pl.pallas_call(
        _outproj_kernel,
        grid=(n_tok // tm,),
        in_specs=[
            pl.BlockSpec((tm, ka), rows),
            pl.BlockSpec((tm, kb), rows),
            pl.BlockSpec((ka, d), fixed),
            pl.BlockSpec((kb, d), fixed),
            pl.BlockSpec((tm, d), rows),
            pl.BlockSpec((1, d), fixed),
            pl.BlockSpec((1, d), fixed),
            pl.BlockSpec((N_EXPERTS, d), fixed),
            pl.BlockSpec((N_EXPERTS, 1), fixed),
        ],
        out_specs=[
            pl.BlockSpec((tm, d), rows),
            pl.BlockSpec((tm, d), rows),
            pl.BlockSpec((tm, LANES), rows),
        ],
        out_shape=[
            jax.ShapeDtypeStruct((n_tok, d), F32),
            jax.ShapeDtypeStruct((n_tok, d), BF16),
            jax.ShapeDtypeStruct((n_tok, LANES), F32),
        ],
        compiler_params=_params(("parallel",)),
        name="outproj_ln_route",
    )(pa, pb, wa, wb, x, ln_g.reshape(1, d), ln_b.reshape(1, d),
      router_w.astype(F32).T, router_bias.astype(F32).reshape(N_EXPERTS, 1))


def _moe_dense_kernel(xb_ref, comb_ref, wg_ref, wu_ref, wd_ref, xres_ref, g_ref, b_ref,
                      o_ref, acc_ref):
    e = pl.program_id(1)

    @pl.when(e == 0)
    def _():
        acc_ref[...] = jnp.zeros_like(acc_ref)

    x = xb_ref[...]
    hg = jnp.dot(x, wg_ref[0], preferred_element_type=F32)
    hu = jnp.dot(x, wu_ref[0], preferred_element_type=F32)
    comb = comb_ref[...]
    lane = lax.broadcasted_iota(jnp.int32, comb.shape, 1)
    c = jnp.sum(jnp.where(lane == e, comb, 0.0), axis=1, keepdims=True)
    h = (hg * jax.nn.sigmoid(hg)) * hu * c
    acc_ref[...] += jnp.dot(h.astype(BF16), wd_ref[0], preferred_element_type=F32)

    @pl.when(e == pl.num_programs(1) - 1)
    def _():
        o_ref[...] = _layer_norm_rows(DEEPNORM_ALPHA * xres_ref[...] + acc_ref[...],
                                      g_ref[...], b_ref[...])


def _moe_dense(xb, comb, w_gate, w_up, w_down, xres, ln_g, ln_b, *, tm):
    n_tok, d = xres.shape
    n_exp, _, dff = w_gate.shape
    rows = lambda i, e: (i, 0)
    fixed = lambda i, e: (0, 0)
    return pl.pallas_call(
        _moe_dense_kernel,
        grid=(n_tok // tm, n_exp),
        in_specs=[
            pl.BlockSpec((tm, d), rows),
            pl.BlockSpec((tm, LANES), rows),
            pl.BlockSpec((1, d, dff), lambda i, e: (e, 0, 0)),
            pl.BlockSpec((1, d, dff), lambda i, e: (e, 0, 0)),
            pl.BlockSpec((1, dff, d), lambda i, e: (e, 0, 0)),
            pl.BlockSpec((tm, d), rows),
            pl.BlockSpec((1, d), fixed),
            pl.BlockSpec((1, d), fixed),
        ],
        out_specs=pl.BlockSpec((tm, d), rows),
        out_shape=jax.ShapeDtypeStruct((n_tok, d), F32),
        scratch_shapes=[pltpu.VMEM((tm, d), F32)],
        compiler_params=_params(("parallel", "arbitrary")),
        name="moe_dense",
    )(xb, comb, w_gate.astype(BF16), w_up.astype(BF16), w_down.astype(BF16), xres,
      ln_g.reshape(1, d), ln_b.reshape(1, d))


def _inproj1_kernel(x_ref, w_ref, qt_ref, k_ref, vt_ref, u_ref, *, sb_w):
    x = x_ref[0].astype(BF16)
    q = jnp.dot(x, w_ref[:, 0:sb_w], preferred_element_type=F32) * ATTN_SCALE
    qt_ref[0] = q.T.astype(BF16)
    k_ref[0] = jnp.dot(x, w_ref[:, sb_w:2 * sb_w], preferred_element_type=F32).astype(BF16)
    v = jnp.dot(x, w_ref[:, 2 * sb_w:3 * sb_w], preferred_element_type=F32)
    vt_ref[0] = v.T.astype(BF16)
    u_ref[0] = jnp.dot(x, w_ref[:, 3 * sb_w:], preferred_element_type=F32)


def _inproj1(x, w_bf, *, sb_w, tm):
    bsz, n_pos, d = x.shape
    n_in = w_bf.shape[1]
    pool_w = n_in - 3 * sb_w
    return pl.pallas_call(
        functools.partial(_inproj1_kernel, sb_w=sb_w),
        grid=(bsz, n_pos // tm),
        in_specs=[
            pl.BlockSpec((1, tm, d), lambda b, i: (b, i, 0)),
            pl.BlockSpec((d, n_in), lambda b, i: (0, 0)),
        ],
        out_specs=[
            pl.BlockSpec((1, sb_w, tm), lambda b, i: (b, 0, i)),
            pl.BlockSpec((1, tm, sb_w), lambda b, i: (b, i, 0)),
            pl.BlockSpec((1, sb_w, tm), lambda b, i: (b, 0, i)),
            pl.BlockSpec((1, tm, pool_w), lambda b, i: (b, i, 0)),
        ],
        out_shape=[
            jax.ShapeDtypeStruct((bsz, sb_w, n_pos), BF16),
            jax.ShapeDtypeStruct((bsz, n_pos, sb_w), BF16),
            jax.ShapeDtypeStruct((bsz, sb_w, n_pos), BF16),
            jax.ShapeDtypeStruct((bsz, n_pos, pool_w), F32),
        ],
        compiler_params=_params(("parallel", "arbitrary")),
        name="inproj1",
    )(x, w_bf)


def _sb_attn_kernel(qt_ref, k_ref, vt_ref, o_ref, *, tq, tk):
    i = pl.program_id(2)
    qt = qt_ref[0]
    zeros = jnp.zeros((HEAD_DIM, tq), BF16)
    qpads = (jnp.concatenate([qt[:HEAD_DIM], zeros], axis=0),
             jnp.concatenate([zeros, qt[HEAD_DIM:]], axis=0))
    cb = SB_SCAN_ROWS
    upper2 = ((lax.broadcasted_iota(jnp.int32, (cb, 2 * cb), 1) & (cb - 1))
              > lax.broadcasted_iota(jnp.int32, (cb, 2 * cb), 0)).astype(BF16)

    def block(j, carry, masked):
        k0 = pl.multiple_of(j * tk, tk)
        kj = k_ref[0, pl.ds(k0, tk), :]
        if masked:
            key = k0 + lax.broadcasted_iota(jnp.int32, (tk, tq), 0)
            qpos = i * tq + lax.broadcasted_iota(jnp.int32, (tk, tq), 1)
            keep = key < qpos
        heads = range(2)
        zs = [jnp.dot(kj, qpads[hh], preferred_element_type=F32) for hh in heads]
        sps = [jnp.maximum(z, 0.0) + jnp.log(1.0 + jnp.exp2(jnp.abs(z) * (-LOG2E))) for z in zs]
        if masked:
            sps = [jnp.where(keep, sp, 0.0) for sp in sps]
        his = [sp.astype(BF16) for sp in sps]
        los = [(sp - hi.astype(F32)).astype(BF16) for sp, hi in zip(sps, his)]
        laters = [carry[hh][1] for hh in heads]
        pieces = [[None] * (tk // cb) for _ in heads]
        for c in reversed(range(tk // cb)):
            rs = slice(c * cb, (c + 1) * cb)
            sufs = [jnp.dot(upper2, jnp.concatenate([his[hh][rs], los[hh][rs]], axis=0),
                            preferred_element_type=F32) for hh in heads]
            for hh in heads:
                spc = sps[hh][rs]
                pieces[hh][c] = jnp.exp2((zs[hh][rs] - (spc + sufs[hh] + laters[hh])) * LOG2E)
                laters[hh] = laters[hh] + sufs[hh][0:1] + spc[0:1]
        out = []
        for hh in heads:
            a = jnp.concatenate(pieces[hh], axis=0)
            if masked:
                a = jnp.where(keep, a, 0.0)
            vj = vt_ref[0, hh * HEAD_DIM:(hh + 1) * HEAD_DIM, pl.ds(k0, tk)]
            acc = carry[hh][0] + jnp.dot(vj, a.astype(BF16), preferred_element_type=F32)
            out.append((acc, laters[hh]))
        return tuple(out)

    init = (jnp.zeros((HEAD_DIM, tq), F32), jnp.zeros((1, tq), F32))
    carry = (init, init)
    n_diag = tq // tk
    last = (i + 1) * n_diag - 1
    for d in range(n_diag):
        carry = block(last - d, carry, True)
    n_old = i * n_diag

    def reach(c):
        return jnp.min(jnp.minimum(c[0][1], c[1][1]))

    def live(state):
        t, r, _ = state
        return jnp.logical_and(t < n_old, r < SB_UNDERFLOW)

    def older(state):
        t, _, c = state
        c = block(n_old - 1 - t, c, False)
        return t + 1, reach(c), c

    _, _, carry = lax.while_loop(live, older, (jnp.int32(0), reach(carry), carry))
    o_ref[0] = jnp.concatenate([carry[0][0], carry[1][0]], axis=0).T.astype(o_ref.dtype)


def _sb_attention(qt, k, vt, *, tq, tk):
    bsz, sb_w, n_pos = qt.shape
    return pl.pallas_call(
        functools.partial(_sb_attn_kernel, tq=tq, tk=tk),
        grid=(bsz, sb_w // LANES, n_pos // tq),
        in_specs=[
            pl.BlockSpec((1, LANES, tq), lambda b, h, i: (b, h, i)),
            pl.BlockSpec((1, n_pos, LANES), lambda b, h, i: (b, 0, h)),
            pl.BlockSpec((1, LANES, n_pos), lambda b, h, i: (b, h, 0)),
        ],
        out_specs=pl.BlockSpec((1, tq, LANES), lambda b, h, i: (b, i, h)),
        out_shape=jax.ShapeDtypeStruct((bsz, n_pos, sb_w), BF16),
        compiler_params=_params(("parallel", "parallel", "arbitrary")),
        name="sb_attention",
    )(qt, k, vt)


POOL_HALO = 32


def _pool_kernel(u_ref, pw_ref, ps_ref, o_ref, *, ts):
    i = pl.program_id(1)
    t0 = pl.multiple_of(i * ts, ts)
    h0 = pl.multiple_of(jnp.maximum(t0 - POOL_HALO, 0), POOL_HALO)
    tpos = t0 + lax.broadcasted_iota(jnp.int32, (ts, 1), 0)
    outs = []
    for g, win in enumerate(POOL_WINDOWS):
        cs = slice(g * POOL_CH, (g + 1) * POOL_CH)
        cur = u_ref[0, pl.ds(t0, ts), cs]
        halo = u_ref[0, pl.ds(h0, POOL_HALO), cs]
        halo = jnp.where(i > 0, halo, 0.0)
        run = jnp.concatenate([halo, cur], axis=0)
        span = 1
        while span < win:
            n = run.shape[0] - span
            run = run[span:span + n] + run[0:n]
            span *= 2
        off = POOL_HALO - (win - 1)
        wsum = run[off:off + ts]
        count = jnp.minimum(tpos + 1, win).astype(F32)
        pooled = wsum / count - cur
        mixed = jnp.dot(pooled.astype(BF16), pw_ref[g], preferred_element_type=F32)
        outs.append(mixed * ps_ref[:, cs])
    o_ref[0] = jnp.concatenate(outs, axis=1).astype(o_ref.dtype)


def _pool_module(u, pool_w, pool_scale, *, ts):
    bsz, n_pos, n_ch = u.shape
    n_g = len(POOL_WINDOWS)
    return pl.pallas_call(
        functools.partial(_pool_kernel, ts=ts),
        grid=(bsz, n_pos // ts),
        in_specs=[
            pl.BlockSpec((1, n_pos, n_ch), lambda b, i: (b, 0, 0)),
            pl.BlockSpec((n_g, POOL_CH, POOL_CH), lambda b, i: (0, 0, 0)),
            pl.BlockSpec((1, n_ch), lambda b, i: (0, 0)),
        ],
        out_specs=pl.BlockSpec((1, ts, n_ch), lambda b, i: (b, i, 0)),
        out_shape=jax.ShapeDtypeStruct((bsz, n_pos, n_ch), BF16),
        compiler_params=_params(("parallel", "arbitrary")),
        name="pool_module",
    )(u, pool_w.astype(BF16), pool_scale.astype(F32).reshape(1, n_ch))


def _rope_tables(n_pos):
    inv = ROPE_THETA ** (-jnp.arange(0, HEAD_DIM, 2, dtype=F32) / HEAD_DIM)
    ang = jnp.arange(n_pos, dtype=F32)[:, None] * inv[None, :]
    ang = jnp.concatenate([ang, ang], axis=-1)
    return jnp.cos(ang), jnp.sin(ang)


def _lambda_init(layer_idx):
    return 0.8 - 0.6 * math.exp(-0.3 * layer_idx)


def kernel(x, router_w, router_bias, l0_in_proj, l0_conv_w, l0_conv_b, l0_conv_ln_g, l0_conv_ln_b, l0_lambda_q1, l0_lambda_k1, l0_lambda_q2, l0_lambda_k2, l0_subln_g, l0_out_proj, l0_ln_mix_g, l0_ln_mix_b, l0_w_gate, l0_w_up, l0_w_down, l0_ln_ffn_g, l0_ln_ffn_b, l1_in_proj, l1_pool_w, l1_pool_scale, l1_out_proj, l1_ln_mix_g, l1_ln_mix_b, l1_w_gate, l1_w_up, l1_w_down, l1_ln_ffn_g, l1_ln_ffn_b):
    bsz, n_pos, d = x.shape
    n_tok = bsz * n_pos
    x = x.astype(F32)

    conv_ch = l0_conv_w.shape[1]
    dv = l0_subln_g.shape[0]
    n_diff = (l0_out_proj.shape[0] - conv_ch) // dv
    qk_w = n_diff * 2 * HEAD_DIM
    v_w = n_diff * dv

    tm = min(512, n_pos)
    tq_diff = min(512, n_pos)
    tq_sb = min(512, n_pos)
    tk_sb = min(256, n_pos)
    ts = min(256, n_pos)
    tmoe = min(1024, n_tok)

    cos, sin = _rope_tables(n_pos)
    sign = jnp.where(jnp.arange(HEAD_DIM) < HEAD_DIM // 2, -1.0, 1.0).astype(F32)
    cos128 = jnp.concatenate([cos, cos], axis=-1)
    sins128 = jnp.concatenate([sin * sign, sin * sign], axis=-1)
    cost = cos.T
    sint = sin.T

    glu, qt, k, vt = _inproj0(x, l0_in_proj.astype(BF16), cos128, sins128, cost, sint,
                              conv_ch=conv_ch, qk_w=qk_w, v_w=v_w, tm=tm)
    a = _conv_module(glu, l0_conv_w.astype(F32), l0_conv_b.astype(F32),
                     l0_conv_ln_g.astype(F32), l0_conv_ln_b.astype(F32), ts=ts)
    lam_init = _lambda_init(0)
    lam = (jnp.exp(jnp.sum(l0_lambda_q1.astype(F32) * l0_lambda_k1.astype(F32)))
           - jnp.exp(jnp.sum(l0_lambda_q2.astype(F32) * l0_lambda_k2.astype(F32))) + lam_init)
    o = _diff_attention(qt, k, vt, lam, l0_subln_g, n_heads=n_diff, lam_init=lam_init,
                        tq=tq_diff, tk=tq_diff)
    xf = x.reshape(n_tok, d)
    y, yb, comb = _outproj_ln_route(a.reshape(n_tok, conv_ch), o.reshape(n_tok, v_w),
                                    l0_out_proj, xf, l0_ln_mix_g.astype(F32),
                                    l0_ln_mix_b.astype(F32), router_w, router_bias, tm=tm)
    xf = _moe_dense(yb, comb, l0_w_gate, l0_w_up, l0_w_down, y,
                    l0_ln_ffn_g.astype(F32), l0_ln_ffn_b.astype(F32), tm=tmoe)

    sb_w = l1_in_proj.shape[1] - len(POOL_WINDOWS) * POOL_CH
    sb_w = sb_w // 3
    qt, k, vt, u = _inproj1(xf.reshape(bsz, n_pos, d), l1_in_proj.astype(BF16), sb_w=sb_w, tm=tm)
    o = _sb_attention(qt, k, vt, tq=tq_sb, tk=tk_sb)
    dpool = _pool_module(u, l1_pool_w, l1_pool_scale, ts=ts)
    y, yb, comb = _outproj_ln_route(o.reshape(n_tok, sb_w), dpool.reshape(n_tok, -1),
                                    l1_out_proj, xf, l1_ln_mix_g.astype(F32),
                                    l1_ln_mix_b.astype(F32), router_w, router_bias, tm=tm)
    xf = _moe_dense(yb, comb, l1_w_gate, l1_w_up, l1_w_down, y,
                    l1_ln_ffn_g.astype(F32), l1_ln_ffn_b.astype(F32), tm=tmoe)
    return xf.reshape(bsz, n_pos, d)
```

```python
import functools
import math

import jax
import jax.numpy as jnp
from jax import lax
from jax.experimental import pallas as pl
from jax.experimental.pallas import tpu as pltpu

F32 = jnp.float32
BF16 = jnp.bfloat16

DEPTH = 2
HEAD_DIM = 64
ROPE_THETA = 10000.0
LN_EPS = 1e-5
RMS_EPS = 1e-5
DEEPNORM_ALPHA = (2 * DEPTH) ** 0.25
CONV_WIDTH = 31
POOL_WINDOWS = (2, 4, 8, 16)
POOL_CH = 128
N_EXPERTS = 16
N_GROUPS = 4
EXPERTS_PER_GROUP = N_EXPERTS // N_GROUPS
ATTN_SCALE = HEAD_DIM ** -0.5
LOG2E = math.log2(math.e)

LANES = 128
SUBLANES = 8
MIB = 1024 * 1024
VMEM_LIMIT = 52 * MIB

NEG_BIG = -1e30
SB_SCAN_ROWS = 128
SB_UNDERFLOW = 110.0


def _params(semantics, vmem=VMEM_LIMIT):
    return pltpu.CompilerParams(dimension_semantics=semantics, vmem_limit_bytes=vmem)


def _layer_norm_rows(h, g, b):
    mu = jnp.mean(h, axis=-1, keepdims=True)
    d = h - mu
    var = jnp.mean(d * d, axis=-1, keepdims=True)
    return d * lax.rsqrt(var + LN_EPS) * g + b


def _load_token_tiles(ref, row0, n, d):
    k = d // LANES
    return jnp.concatenate([ref[pl.ds(row0 + c, n, stride=k), :] for c in range(k)], axis=1)


def _store_token_tiles(ref, row0, val):
    n, d = val.shape
    k = d // LANES
    for c in range(k):
        ref[pl.ds(row0 + c, n, stride=k), :] = val[:, c * LANES:(c + 1) * LANES]


def _inproj0_kernel(x_ref, w_ref, cos_ref, sins_ref, cost_ref, sint_ref,
                    glu_ref, qt_ref, k_ref, vt_ref, *, conv_ch, qk_w):
    x = x_ref[0].astype(BF16)
    c0, c1, c2, c3 = conv_ch, 2 * conv_ch, 2 * conv_ch + qk_w, 2 * conv_ch + 2 * qk_w

    a_val = jnp.dot(x, w_ref[:, 0:c0], preferred_element_type=F32)
    a_gate = jnp.dot(x, w_ref[:, c0:c1], preferred_element_type=F32)
    glu_ref[0] = a_val * jax.nn.sigmoid(a_gate)

    dq = jnp.dot(x, w_ref[:, c1:c2], preferred_element_type=F32)
    dqt = dq.T
    cost = cost_ref[...]
    sint = sint_ref[...]
    half = HEAD_DIM // 2
    pieces = []
    for blk in range(qk_w // HEAD_DIM):
        u = dqt[blk * HEAD_DIM:(blk + 1) * HEAD_DIM]
        rot = jnp.concatenate([-u[half:], u[:half]], axis=0)
        pieces.append((u * cost + rot * sint) * (ATTN_SCALE * LOG2E))
    qt_ref[0] = jnp.concatenate(pieces, axis=0).astype(BF16)

    dk = jnp.dot(x, w_ref[:, c2:c3], preferred_element_type=F32)
    cos = cos_ref[...]
    sins = sins_ref[...]
    lane = lax.broadcasted_iota(jnp.int32, cos.shape, 1)
    first = (lane & (HEAD_DIM - 1)) < half
    kparts = []
    for h in range(qk_w // LANES):
        u = dk[:, h * LANES:(h + 1) * LANES]
        rot = jnp.where(first, pltpu.roll(u, LANES - half, 1), pltpu.roll(u, half, 1))
        kparts.append(u * cos + rot * sins)
    k_ref[0] = jnp.concatenate(kparts, axis=1).astype(BF16)

    dv = jnp.dot(x, w_ref[:, c3:], preferred_element_type=F32)
    vt_ref[0] = dv.T.astype(BF16)


def _inproj0(x, w_bf, cos128, sins128, cost, sint, *, conv_ch, qk_w, v_w, tm):
    bsz, n_pos, d = x.shape
    n_in = w_bf.shape[1]
    kern = functools.partial(_inproj0_kernel, conv_ch=conv_ch, qk_w=qk_w)
    return pl.pallas_call(
        kern,
        grid=(bsz, n_pos // tm),
        in_specs=[
            pl.BlockSpec((1, tm, d), lambda b, i: (b, i, 0)),
            pl.BlockSpec((d, n_in), lambda b, i: (0, 0)),
            pl.BlockSpec((tm, LANES), lambda b, i: (i, 0)),
            pl.BlockSpec((tm, LANES), lambda b, i: (i, 0)),
            pl.BlockSpec((HEAD_DIM, tm), lambda b, i: (0, i)),
            pl.BlockSpec((HEAD_DIM, tm), lambda b, i: (0, i)),
        ],
        out_specs=[
            pl.BlockSpec((1, tm, conv_ch), lambda b, i: (b, i, 0)),
            pl.BlockSpec((1, qk_w, tm), lambda b, i: (b, 0, i)),
            pl.BlockSpec((1, tm, qk_w), lambda b, i: (b, i, 0)),
            pl.BlockSpec((1, v_w, tm), lambda b, i: (b, 0, i)),
        ],
        out_shape=[
            jax.ShapeDtypeStruct((bsz, n_pos, conv_ch), F32),
            jax.ShapeDtypeStruct((bsz, qk_w, n_pos), BF16),
            jax.ShapeDtypeStruct((bsz, n_pos, qk_w), BF16),
            jax.ShapeDtypeStruct((bsz, v_w, n_pos), BF16),
        ],
        compiler_params=_params(("parallel", "arbitrary")),
        name="inproj0",
    )(x, w_bf, cos128, sins128, cost, sint)


CONV_HALO = 32


def _conv_kernel(a_ref, cw_ref, cb_ref, g_ref, b_ref, o_ref, *, ts):
    i = pl.program_id(1)
    t0 = pl.multiple_of(i * ts, ts)
    h0 = pl.multiple_of(jnp.maximum(t0 - CONV_HALO, 0), CONV_HALO)
    n_ch = a_ref.shape[2]
    outs = []
    for c in range(n_ch // LANES):
        cs = slice(c * LANES, (c + 1) * LANES)
        cur = a_ref[0, pl.ds(t0, ts), cs]
        halo = a_ref[0, pl.ds(h0, CONV_HALO), cs]
        halo = jnp.where(i > 0, halo, 0.0)
        xw = jnp.concatenate([halo, cur], axis=0)
        acc = jnp.zeros((ts, LANES), F32) + cb_ref[:, cs]
        for r in range(SUBLANES):
            xr = xw if r == 0 else xw[r:r + ts + CONV_HALO - SUBLANES]
            for a in range(CONV_HALO // SUBLANES + 1):
                w = SUBLANES * a + r - (CONV_HALO - CONV_WIDTH + 1)
                if 0 <= w < CONV_WIDTH and SUBLANES * a + ts <= xr.shape[0]:
                    acc = acc + xr[SUBLANES * a:SUBLANES * a + ts] * cw_ref[w:w + 1, cs]
        outs.append(acc)
    y = jnp.concatenate(outs, axis=1)
    y = _layer_norm_rows(y, g_ref[...], b_ref[...])
    o_ref[0] = (y * jax.nn.sigmoid(y)).astype(o_ref.dtype)


def _conv_module(a, conv_w, conv_b, ln_g, ln_b, *, ts):
    bsz, n_pos, n_ch = a.shape
    return pl.pallas_call(
        functools.partial(_conv_kernel, ts=ts),
        grid=(bsz, n_pos // ts),
        in_specs=[
            pl.BlockSpec((1, n_pos, n_ch), lambda b, i: (b, 0, 0)),
            pl.BlockSpec((CONV_WIDTH, n_ch), lambda b, i: (0, 0)),
            pl.BlockSpec((1, n_ch), lambda b, i: (0, 0)),
            pl.BlockSpec((1, n_ch), lambda b, i: (0, 0)),
            pl.BlockSpec((1, n_ch), lambda b, i: (0, 0)),
        ],
        out_specs=pl.BlockSpec((1, ts, n_ch), lambda b, i: (b, i, 0)),
        out_shape=jax.ShapeDtypeStruct((bsz, n_pos, n_ch), BF16),
        compiler_params=_params(("parallel", "arbitrary")),
        name="conv_module",
    )(a, conv_w, conv_b.reshape(1, n_ch), ln_g.reshape(1, n_ch), ln_b.reshape(1, n_ch))


def _diff_attn_kernel(lam_ref, qt_ref, k_ref, vt_ref, g_ref, o_ref, *, tq, tk, out_scale):
    i = pl.program_id(2)
    lam = lam_ref[0, 0]
    qt = qt_ref[0]
    zeros = jnp.zeros((HEAD_DIM, tq), BF16)
    qa = jnp.concatenate([qt[:HEAD_DIM], zeros], axis=0)
    qb = jnp.concatenate([zeros, qt[HEAD_DIM:]], axis=0)
    dv = vt_ref.shape[1]

    def online(s, vj, m, l, acc):
        m_new = jnp.maximum(m, jnp.max(s, axis=0, keepdims=True))
        alpha = jnp.exp2(m - m_new)
        p = jnp.exp2(s - m_new)
        l_new = alpha * l + jnp.sum(p, axis=0, keepdims=True)
        acc_new = alpha * acc + jnp.dot(vj, p.astype(BF16), preferred_element_type=F32)
        return m_new, l_new, acc_new

    def block(j, carry, masked):
        m1, l1, acc1, m2, l2, acc2 = carry
        k0 = pl.multiple_of(j * tk, tk)
        kj = k_ref[0, pl.ds(k0, tk), :]
        vj = vt_ref[0, :, pl.ds(k0, tk)]
        s1 = jnp.dot(kj, qa, preferred_element_type=F32)
        s2 = jnp.dot(kj, qb, preferred_element_type=F32)
        if masked:
            key = k0 + lax.broadcasted_iota(jnp.int32, (tk, tq), 0)
            qpos = i * tq + lax.broadcasted_iota(jnp.int32, (tk, tq), 1)
            keep = key <= qpos
            s1 = jnp.where(keep, s1, NEG_BIG)
            s2 = jnp.where(keep, s2, NEG_BIG)
        m1, l1, acc1 = online(s1, vj, m1, l1, acc1)
        m2, l2, acc2 = online(s2, vj, m2, l2, acc2)
        return m1, l1, acc1, m2, l2, acc2

    row = jnp.full((1, tq), NEG_BIG, F32)
    zrow = jnp.zeros((1, tq), F32)
    zacc = jnp.zeros((dv, tq), F32)
    carry = (row, zrow, zacc, row, zrow, zacc)
    carry = lax.fori_loop(0, i, lambda j, c: block(j, c, False), carry)
    m1, l1, acc1, m2, l2, acc2 = block(i, carry, True)

    o = acc1 / l1 - lam * (acc2 / l2)
    o = o * lax.rsqrt(jnp.mean(o * o, axis=0, keepdims=True) + RMS_EPS)
    o = o * g_ref[...] * out_scale
    o_ref[0] = o.T.astype(o_ref.dtype)


def _diff_attention(qt, k, vt, lam, subln_g, *, n_heads, lam_init, tq, tk):
    bsz, _, n_pos = qt.shape
    dv = vt.shape[1] // n_heads
    kern = functools.partial(_diff_attn_kernel, tq=tq, tk=tk, out_scale=1.0 - lam_init)
    return pl.pallas_call(
        kern,
        grid=(bsz, n_heads, n_pos // tq),
        in_specs=[
            pl.BlockSpec(memory_space=pltpu.SMEM),
            pl.BlockSpec((1, 2 * HEAD_DIM, tq), lambda b, h, i: (b, h, i)),
            pl.BlockSpec((1, n_pos, 2 * HEAD_DIM), lambda b, h, i: (b, 0, h)),
            pl.BlockSpec((1, dv, n_pos), lambda b, h, i: (b, h, 0)),
            pl.BlockSpec((dv, 1), lambda b, h, i: (0, 0)),
        ],
        out_specs=pl.BlockSpec((1, tq, dv), lambda b, h, i: (b, i, h)),
        out_shape=jax.ShapeDtypeStruct((bsz, n_pos, n_heads * dv), BF16),
        compiler_params=_params(("parallel", "parallel", "arbitrary")),
        name="diff_attention",
    )(lam.reshape(1, 1), qt, k, vt, subln_g.astype(F32).reshape(dv, 1))


def _masked_top2(vals, member, row):
    big = N_EXPERTS
    v = jnp.where(member, vals, -jnp.inf)
    v1 = jnp.max(v, axis=0, keepdims=True)
    i1 = jnp.min(jnp.where(v == v1, row, big), axis=0, keepdims=True)
    v = jnp.where(row == i1, -jnp.inf, v)
    v2 = jnp.max(v, axis=0, keepdims=True)
    i2 = jnp.min(jnp.where(v == v2, row, big), axis=0, keepdims=True)
    return v1, i1, v2, i2


def _top2_route(aff, sel):
    row = lax.broadcasted_iota(jnp.int32, sel.shape, 0)
    group = lax.shift_right_logical(row, EXPERTS_PER_GROUP.bit_length() - 1)
    gbest = None
    for g in range(N_GROUPS):
        v1, _, v2, _ = _masked_top2(sel, group == g, row)
        score = v1 + v2
        if gbest is None:
            gbest, gval = jnp.zeros_like(score, dtype=jnp.int32), score
        else:
            better = score > gval
            gbest = jnp.where(better, g, gbest)
            gval = jnp.where(better, score, gval)
    _, e1, _, e2 = _masked_top2(sel, group == gbest, row)
    a1 = jnp.sum(jnp.where(row == e1, aff, 0.0), axis=0, keepdims=True)
    a2 = jnp.sum(jnp.where(row == e2, aff, 0.0), axis=0, keepdims=True)
    denom = a1 + a2
    return e1, e2, a1 / denom, a2 / denom


ROUTE_ROWS = SUBLANES


def _outproj_kernel(pa_ref, pb_ref, wa_ref, wb_ref, x_ref, g_ref, b_ref, rw_ref, rb_ref,
                    y_ref, route_ref, gate_ref, count_ref, seen_ref):
    i = pl.program_id(0)

    @pl.when(i == 0)
    def _():
        seen_ref[...] = jnp.zeros_like(seen_ref)

    mix = jnp.dot(pa_ref[...], wa_ref[...], preferred_element_type=F32)
    mix = mix + jnp.dot(pb_ref[...], wb_ref[...], preferred_element_type=F32)
    y = _layer_norm_rows(DEEPNORM_ALPHA * x_ref[...] + mix, g_ref[...], b_ref[...])
    _store_token_tiles(y_ref, 0, y)
    tm = y.shape[0]
    logits = lax.dot_general(rw_ref[...], y, (((1,), (1,)), ((), ())),
                             precision=lax.Precision.HIGHEST, preferred_element_type=F32)
    aff = jax.nn.sigmoid(logits)
    e1, e2, g1, g2 = _top2_route(aff, aff + rb_ref[...])

    erow = lax.broadcasted_iota(jnp.int32, (N_EXPERTS, tm), 0)
    hit = jnp.where((erow == e1) | (erow == e2), 1.0, 0.0)
    upto = (lax.broadcasted_iota(jnp.int32, (tm, tm), 0)
            <= lax.broadcasted_iota(jnp.int32, (tm, tm), 1)).astype(BF16)
    incl = jnp.dot(hit.astype(BF16), upto, preferred_element_type=F32)
    before = seen_ref[:, 0:1] + incl - hit
    r1 = jnp.sum(jnp.where(erow == e1, before, 0.0), axis=0, keepdims=True).astype(jnp.int32)
    r2 = jnp.sum(jnp.where(erow == e2, before, 0.0), axis=0, keepdims=True).astype(jnp.int32)
    seen_ref[...] = seen_ref[...] + jnp.sum(hit, axis=1, keepdims=True)
    count_ref[...] = seen_ref[...]

    rrow = lax.broadcasted_iota(jnp.int32, (ROUTE_ROWS, tm), 0)
    route_ref[0] = jnp.where(rrow == 0, e1, jnp.where(rrow == 1, e2,
                             jnp.where(rrow == 2, r1, jnp.where(rrow == 3, r2, 0))))
    grow = lax.broadcasted_iota(jnp.int32, (LANES, tm), 0)
    gate_ref[...] = jnp.where(grow == 0, g1, jnp.where(grow == 1, g2, 0.0)).T


def _outproj_ln_route(pa, pb, w_out, x, ln_g, ln_b, router_w, router_bias, *, tm):
    n_tok, d = x.shape
    ka, kb = pa.shape[1], pb.shape[1]
    wa = w_out[:ka].astype(BF16)
    wb = w_out[ka:].astype(BF16)
    rows = lambda i: (i, 0)
    fixed = lambda i: (0, 0)
    return pl.pallas_call(
        _outproj_kernel,
        grid=(n_tok // tm,),
        in_specs=[
            pl.BlockSpec((tm, ka), rows),
            pl.BlockSpec((tm, kb), rows),
            pl.BlockSpec((ka, d), fixed),
            pl.BlockSpec((kb, d), fixed),
            pl.BlockSpec((tm, d), rows),
            pl.BlockSpec((1, d), fixed),
            pl.BlockSpec((1, d), fixed),
            pl.BlockSpec((N_EXPERTS, d), fixed),
            pl.BlockSpec((N_EXPERTS, 1), fixed),
        ],
        out_specs=[
            pl.BlockSpec((tm * d // LANES, LANES), rows),
            pl.BlockSpec((1, ROUTE_ROWS, tm), lambda i: (i, 0, 0)),
            pl.BlockSpec((tm, LANES), rows),
            pl.BlockSpec((N_EXPERTS, LANES), fixed),
        ],
        out_shape=[
            jax.ShapeDtypeStruct((n_tok * d // LANES, LANES), F32),
            jax.ShapeDtypeStruct((n_tok // tm, ROUTE_ROWS, tm), jnp.int32),
            jax.ShapeDtypeStruct((n_tok, LANES), F32),
            jax.ShapeDtypeStruct((N_EXPERTS, LANES), F32),
        ],
        scratch_shapes=[pltpu.VMEM((N_EXPERTS, LANES), F32)],
        compiler_params=_params(("arbitrary",)),
        name="outproj_ln_route",
    )(pa, pb, wa, wb, x, ln_g.reshape(1, d), ln_b.reshape(1, d),
      router_w.astype(F32).T, router_bias.astype(F32).reshape(N_EXPERTS, 1))


MOE_ROW_TILE = 512
DMA_UNROLL = 8


def _row_copy_wait(src_hbm, dst, sem, n_rows):
    pltpu.make_async_copy(src_hbm.at[pl.ds(0, n_rows)], dst, sem).wait()


def _dispatch_kernel(offend_ref, pos_ref, y_hbm, xs_hbm, zero_ref, sem, zsem, *, tmd, tr):
    i = pl.program_id(0)
    n = pl.num_programs(0)
    tile_rows = MOE_ROW_TILE * tr

    @pl.when(i == 0)
    def _():
        zero_ref[...] = jnp.zeros_like(zero_ref)
        def tail_copy(e):
            start = pl.multiple_of((offend_ref[e] - MOE_ROW_TILE) * tr, tile_rows)
            return pltpu.make_async_copy(zero_ref, xs_hbm.at[pl.ds(start, tile_rows)], zsem)
        def nonempty(e):
            return offend_ref[e] > (offend_ref[e - 1] if e > 0 else 0)
        for e in range(N_EXPERTS):
            @pl.when(nonempty(e))
            def _():
                tail_copy(e).start()
        for e in range(N_EXPERTS):
            @pl.when(nonempty(e))
            def _():
                tail_copy(e).wait()
        def unused_copy(j):
            return pltpu.make_async_copy(
                zero_ref, xs_hbm.at[pl.ds(pl.multiple_of(j * tile_rows, tile_rows), tile_rows)],
                zsem)
        first_unused = lax.shift_right_logical(offend_ref[N_EXPERTS - 1],
                                               MOE_ROW_TILE.bit_length() - 1)
        n_tiles = xs_hbm.shape[0] // tile_rows
        lax.fori_loop(first_unused, n_tiles, lambda j, c: (unused_copy(j).start(), c)[1], 0)
        lax.fori_loop(first_unused, n_tiles, lambda j, c: (unused_copy(j).wait(), c)[1], 0)

    slot = i % 2

    def issue(g, c):
        r0 = g * DMA_UNROLL
        ps = [[pos_ref[0, 0, k * tmd + r0 + j] for k in range(2)] for j in range(DMA_UNROLL)]
        for j in range(DMA_UNROLL):
            src = y_hbm.at[pl.ds(pl.multiple_of((i * tmd + r0 + j) * tr, tr), tr)]
            for k in range(2):
                dst = xs_hbm.at[pl.ds(pl.multiple_of(ps[j][k] * tr, tr), tr)]
                pltpu.make_async_copy(src, dst, sem.at[slot]).start()
        return c

    lax.fori_loop(0, tmd // DMA_UNROLL, issue, 0)

    step_rows = 2 * tmd * tr
    @pl.when(i > 0)
    def _():
        _row_copy_wait(y_hbm, xs_hbm.at[pl.ds(0, step_rows)], sem.at[1 - slot], step_rows)

    @pl.when(i == n - 1)
    def _():
        _row_copy_wait(y_hbm, xs_hbm.at[pl.ds(0, step_rows)], sem.at[slot], step_rows)


def _dispatch(y_tok, pos, off_end, *, n_sorted, tmd, tr):
    n_tok = y_tok.shape[0] // tr
    return pl.pallas_call(
        functools.partial(_dispatch_kernel, tmd=tmd, tr=tr),
        grid_spec=pltpu.PrefetchScalarGridSpec(
            num_scalar_prefetch=1,
            grid=(n_tok // tmd,),
            in_specs=[
                pl.BlockSpec((1, 1, 2 * tmd), lambda i, off: (i, 0, 0), memory_space=pltpu.SMEM),
                pl.BlockSpec(memory_space=pl.ANY),
            ],
            out_specs=pl.BlockSpec(memory_space=pl.ANY),
            scratch_shapes=[pltpu.VMEM((MOE_ROW_TILE * tr, LANES), F32),
                            pltpu.SemaphoreType.DMA((2,)), pltpu.SemaphoreType.DMA],
        ),
        out_shape=jax.ShapeDtypeStruct((n_sorted * tr, LANES), F32),
        compiler_params=_params(("arbitrary",)),
        name="moe_dispatch",
    )(off_end, pos, y_tok)


def _experts_kernel(texp_ref, nused_ref, xs_ref, wg_ref, wu_ref, wd_ref, ys_ref):
    @pl.when(pl.program_id(0) >= nused_ref[0])
    def _():
        ys_ref[...] = jnp.zeros_like(ys_ref)

    @pl.when(pl.program_id(0) < nused_ref[0])
    def _():
        d = wg_ref.shape[1]
        x = _load_token_tiles(xs_ref, 0, MOE_ROW_TILE, d).astype(BF16)
        hg = jnp.dot(x, wg_ref[0].astype(BF16), preferred_element_type=F32)
        hu = jnp.dot(x, wu_ref[0].astype(BF16), preferred_element_type=F32)
        h = (hg * jax.nn.sigmoid(hg)) * hu
        _store_token_tiles(ys_ref, 0, jnp.dot(h.astype(BF16), wd_ref[0].astype(BF16),
                                              preferred_element_type=F32))


def _experts(xs, tile_expert, n_used, w_gate, w_up, w_down):
    _, d, dff = w_gate.shape
    tile_rows = MOE_ROW_TILE * d // LANES
    tile = lambda i, te, nu: (jnp.minimum(i, nu[0] - 1), 0)
    every = lambda i, te, nu: (i, 0)
    wsel = lambda i, te, nu: (te[jnp.minimum(i, nu[0] - 1)], 0, 0)
    return pl.pallas_call(
        _experts_kernel,
        grid_spec=pltpu.PrefetchScalarGridSpec(
            num_scalar_prefetch=2,
            grid=(xs.shape[0] // tile_rows,),
            in_specs=[
                pl.BlockSpec((tile_rows, LANES), tile),
                pl.BlockSpec((1, d, dff), wsel),
                pl.BlockSpec((1, d, dff), wsel),
                pl.BlockSpec((1, dff, d), wsel),
            ],
            out_specs=pl.BlockSpec((tile_rows, LANES), every),
        ),
        out_shape=jax.ShapeDtypeStruct(xs.shape, F32),
        compiler_params=_params(("arbitrary",)),
        name="moe_experts",
    )(tile_expert, n_used, xs, w_gate, w_up, w_down)


def _combine_kernel(pos_ref, posn_ref, y_ref, gate_ref, g_ref, b_ref, ys_hbm, o_ref,
                    buf, sem, *, tmc, tr):
    i = pl.program_id(0)
    n = pl.num_programs(0)
    d = o_ref.shape[1]
    half = tmc * tr
    slot_rows = 2 * half

    def gather(p_ref, slot):
        base = slot * slot_rows
        def issue(g, c):
            r0 = g * DMA_UNROLL
            ps = [[p_ref[0, 0, k * tmc + r0 + j] for k in range(2)] for j in range(DMA_UNROLL)]
            for j in range(DMA_UNROLL):
                for k in range(2):
                    src = ys_hbm.at[pl.ds(pl.multiple_of(ps[j][k] * tr, tr), tr)]
                    dst = buf.at[pl.ds(pl.multiple_of(base + k * half + (r0 + j) * tr, tr), tr)]
                    pltpu.make_async_copy(src, dst, sem.at[slot]).start(priority=k)
            return c
        lax.fori_loop(0, tmc // DMA_UNROLL, issue, 0)

    @pl.when(i == 0)
    def _():
        gather(pos_ref, 0)

    @pl.when(i + 1 < n)
    def _():
        gather(posn_ref, (i + 1) % 2)

    slot = i % 2
    base = pl.multiple_of(slot * slot_rows, slot_rows)
    _row_copy_wait(ys_hbm, buf.at[pl.ds(base, slot_rows)], sem.at[slot], slot_rows)
    gate = gate_ref[...]
    ffn = (gate[:, 0:1] * _load_token_tiles(buf, base, tmc, d)
           + gate[:, 1:2] * _load_token_tiles(buf, base + half, tmc, d))
    y = _load_token_tiles(y_ref, 0, tmc, d)
    o_ref[...] = _layer_norm_rows(DEEPNORM_ALPHA * y + ffn, g_ref[...], b_ref[...])


def _combine(ys, pos, y_tok, gates, ln_g, ln_b, *, tmc, tr):
    d = tr * LANES
    n_tok = y_tok.shape[0] // tr
    n = n_tok // tmc
    rows = lambda i: (i, 0)
    fixed = lambda i: (0, 0)
    return pl.pallas_call(
        functools.partial(_combine_kernel, tmc=tmc, tr=tr),
        grid=(n,),
        in_specs=[
            pl.BlockSpec((1, 1, 2 * tmc), lambda i: (i, 0, 0), memory_space=pltpu.SMEM),
            pl.BlockSpec((1, 1, 2 * tmc), lambda i: (jnp.minimum(i + 1, n - 1), 0, 0),
                         memory_space=pltpu.SMEM),
            pl.BlockSpec((tmc * tr, LANES), rows),
            pl.BlockSpec((tmc, LANES), rows),
            pl.BlockSpec((1, d), fixed),
            pl.BlockSpec((1, d), fixed),
            pl.BlockSpec(memory_space=pl.ANY),
        ],
        out_specs=pl.BlockSpec((tmc, d), rows),
        out_shape=jax.ShapeDtypeStruct((n_tok, d), F32),
        scratch_shapes=[pltpu.VMEM((2 * 2 * tmc * tr, LANES), F32),
                        pltpu.SemaphoreType.DMA((2,))],
        compiler_params=_params(("arbitrary",)),
        name="moe_combine",
    )(pos, pos, y_tok, gates, ln_g.reshape(1, d), ln_b.reshape(1, d), ys)


def _tile_positions(route, off, t):
    e = route[:, 0:2, :]
    pos = jnp.take(off, e) + route[:, 2:4, :]
    pos = pos.transpose(1, 0, 2).reshape(2, -1, t)
    return pos.transpose(1, 0, 2).reshape(-1, 1, 2 * t)


def _moe_sparse(y_tok, route, gates, counts, w_gate, w_up, w_down, ln_g, ln_b, *, tmd, tmc):
    d = w_gate.shape[1]
    tr = d // LANES
    n_tok = y_tok.shape[0] // tr
    cnt = counts[:, 0].astype(jnp.int32)
    padded = (cnt + MOE_ROW_TILE - 1) // MOE_ROW_TILE * MOE_ROW_TILE
    off_end = jnp.cumsum(padded)
    off = off_end - padded
    n_sorted = 2 * n_tok + N_EXPERTS * MOE_ROW_TILE
    n_tiles = n_sorted // MOE_ROW_TILE
    n_used = (off_end[-1:] // MOE_ROW_TILE).astype(jnp.int32)
    tile_start = jnp.arange(n_tiles, dtype=jnp.int32) * MOE_ROW_TILE
    tile_expert = jnp.minimum(jnp.sum(tile_start[:, None] >= off_end[None, :], axis=1),
                              N_EXPERTS - 1).astype(jnp.int32)
    xs = _dispatch(y_tok, _tile_positions(route, off, tmd), off_end.astype(jnp.int32),
                   n_sorted=n_sorted, tmd=tmd, tr=tr)
    ys = _experts(xs, tile_expert, n_used, w_gate, w_up, w_down)
    return _combine(ys, _tile_positions(route, off, tmc), y_tok, gates, ln_g, ln_b,
                    tmc=tmc, tr=tr)


def _inproj1_kernel(x_ref, w_ref, qt_ref, k_ref, vt_ref, u_ref, *, sb_w):
    x = x_ref[0].astype(BF16)
    q = jnp.dot(x, w_ref[:, 0:sb_w], preferred_element_type=F32) * ATTN_SCALE
    qt_ref[0] = q.T.astype(BF16)
    k_ref[0] = jnp.dot(x, w_ref[:, sb_w:2 * sb_w], preferred_element_type=F32).astype(BF16)
    v = jnp.dot(x, w_ref[:, 2 * sb_w:3 * sb_w], preferred_element_type=F32)
    vt_ref[0] = v.T.astype(BF16)
    u_ref[0] = jnp.dot(x, w_ref[:, 3 * sb_w:], preferred_element_type=F32)


def _inproj1(x, w_bf, *, sb_w, tm):
    bsz, n_pos, d = x.shape
    n_in = w_bf.shape[1]
    pool_w = n_in - 3 * sb_w
    return pl.pallas_call(
        functools.partial(_inproj1_kernel, sb_w=sb_w),
        grid=(bsz, n_pos // tm),
        in_specs=[
            pl.BlockSpec((1, tm, d), lambda b, i: (b, i, 0)),
            pl.BlockSpec((d, n_in), lambda b, i: (0, 0)),
        ],
        out_specs=[
            pl.BlockSpec((1, sb_w, tm), lambda b, i: (b, 0, i)),
            pl.BlockSpec((1, tm, sb_w), lambda b, i: (b, i, 0)),
            pl.BlockSpec((1, sb_w, tm), lambda b, i: (b, 0, i)),
            pl.BlockSpec((1, tm, pool_w), lambda b, i: (b, i, 0)),
        ],
        out_shape=[
            jax.ShapeDtypeStruct((bsz, sb_w, n_pos), BF16),
            jax.ShapeDtypeStruct((bsz, n_pos, sb_w), BF16),
            jax.ShapeDtypeStruct((bsz, sb_w, n_pos), BF16),
            jax.ShapeDtypeStruct((bsz, n_pos, pool_w), F32),
        ],
        compiler_params=_params(("parallel", "arbitrary")),
        name="inproj1",
    )(x, w_bf)


def _sb_attn_kernel(qt_ref, k_ref, vt_ref, o_ref, *, tq, tk):
    i = pl.program_id(2)
    qt = qt_ref[0]
    zeros = jnp.zeros((HEAD_DIM, tq), BF16)
    qpads = (jnp.concatenate([qt[:HEAD_DIM], zeros], axis=0),
             jnp.concatenate([zeros, qt[HEAD_DIM:]], axis=0))
    cb = SB_SCAN_ROWS
    upper2 = ((lax.broadcasted_iota(jnp.int32, (cb, 2 * cb), 1) & (cb - 1))
              > lax.broadcasted_iota(jnp.int32, (cb, 2 * cb), 0)).astype(BF16)

    def block(j, carry, masked):
        k0 = pl.multiple_of(j * tk, tk)
        kj = k_ref[0, pl.ds(k0, tk), :]
        if masked:
            key = k0 + lax.broadcasted_iota(jnp.int32, (tk, tq), 0)
            qpos = i * tq + lax.broadcasted_iota(jnp.int32, (tk, tq), 1)
            keep = key < qpos
        heads = range(2)
        zs = [jnp.dot(kj, qpads[hh], preferred_element_type=F32) for hh in heads]
        sps = [jnp.maximum(z, 0.0) + jnp.log(1.0 + jnp.exp2(jnp.abs(z) * (-LOG2E))) for z in zs]
        if masked:
            sps = [jnp.where(keep, sp, 0.0) for sp in sps]
        his = [sp.astype(BF16) for sp in sps]
        los = [(sp - hi.astype(F32)).astype(BF16) for sp, hi in zip(sps, his)]
        laters = [carry[hh][1] for hh in heads]
        pieces = [[None] * (tk // cb) for _ in heads]
        for c in reversed(range(tk // cb)):
            rs = slice(c * cb, (c + 1) * cb)
            sufs = [jnp.dot(upper2, jnp.concatenate([his[hh][rs], los[hh][rs]], axis=0),
                            preferred_element_type=F32) for hh in heads]
            for hh in heads:
                spc = sps[hh][rs]
                pieces[hh][c] = jnp.exp2((zs[hh][rs] - (spc + sufs[hh] + laters[hh])) * LOG2E)
                laters[hh] = laters[hh] + sufs[hh][0:1] + spc[0:1]
        out = []
        for hh in heads:
            a = jnp.concatenate(pieces[hh], axis=0)
            if masked:
                a = jnp.where(keep, a, 0.0)
            vj = vt_ref[0, hh * HEAD_DIM:(hh + 1) * HEAD_DIM, pl.ds(k0, tk)]
            acc = carry[hh][0] + jnp.dot(vj, a.astype(BF16), preferred_element_type=F32)
            out.append((acc, laters[hh]))
        return tuple(out)

    init = (jnp.zeros((HEAD_DIM, tq), F32), jnp.zeros((1, tq), F32))
    carry = (init, init)
    n_diag = tq // tk
    last = (i + 1) * n_diag - 1
    for d in range(n_diag):
        carry = block(last - d, carry, True)
    n_old = i * n_diag

    def reach(c):
        return jnp.min(jnp.minimum(c[0][1], c[1][1]))

    def live(state):
        t, r, _ = state
        return jnp.logical_and(t < n_old, r < SB_UNDERFLOW)

    def older(state):
        t, _, c = state
        c = block(n_old - 1 - t, c, False)
        return t + 1, reach(c), c

    _, _, carry = lax.while_loop(live, older, (jnp.int32(0), reach(carry), carry))
    o_ref[0] = jnp.concatenate([carry[0][0], carry[1][0]], axis=0).T.astype(o_ref.dtype)


def _sb_attention(qt, k, vt, *, tq, tk):
    bsz, sb_w, n_pos = qt.shape
    return pl.pallas_call(
        functools.partial(_sb_attn_kernel, tq=tq, tk=tk),
        grid=(bsz, sb_w // LANES, n_pos // tq),
        in_specs=[
            pl.BlockSpec((1, LANES, tq), lambda b, h, i: (b, h, i)),
            pl.BlockSpec((1, n_pos, LANES), lambda b, h, i: (b, 0, h)),
            pl.BlockSpec((1, LANES, n_pos), lambda b, h, i: (b, h, 0)),
        ],
        out_specs=pl.BlockSpec((1, tq, LANES), lambda b, h, i: (b, i, h)),
        out_shape=jax.ShapeDtypeStruct((bsz, n_pos, sb_w), BF16),
        compiler_params=_params(("parallel", "parallel", "arbitrary")),
        name="sb_attention",
    )(qt, k, vt)


POOL_HALO = 32


def _pool_kernel(u_ref, pw_ref, ps_ref, o_ref, *, ts):
    i = pl.program_id(1)
    t0 = pl.multiple_of(i * ts, ts)
    h0 = pl.multiple_of(jnp.maximum(t0 - POOL_HALO, 0), POOL_HALO)
    tpos = t0 + lax.broadcasted_iota(jnp.int32, (ts, 1), 0)
    outs = []
    for g, win in enumerate(POOL_WINDOWS):
        cs = slice(g * POOL_CH, (g + 1) * POOL_CH)
        cur = u_ref[0, pl.ds(t0, ts), cs]
        halo = u_ref[0, pl.ds(h0, POOL_HALO), cs]
        halo = jnp.where(i > 0, halo, 0.0)
        run = jnp.concatenate([halo, cur], axis=0)
        span = 1
        while span < win:
            n = run.shape[0] - span
            run = run[span:span + n] + run[0:n]
            span *= 2
        off = POOL_HALO - (win - 1)
        wsum = run[off:off + ts]
        count = jnp.minimum(tpos + 1, win).astype(F32)
        pooled = wsum / count - cur
        mixed = jnp.dot(pooled.astype(BF16), pw_ref[g], preferred_element_type=F32)
        outs.append(mixed * ps_ref[:, cs])
    o_ref[0] = jnp.concatenate(outs, axis=1).astype(o_ref.dtype)


def _pool_module(u, pool_w, pool_scale, *, ts):
    bsz, n_pos, n_ch = u.shape
    n_g = len(POOL_WINDOWS)
    return pl.pallas_call(
        functools.partial(_pool_kernel, ts=ts),
        grid=(bsz, n_pos // ts),
        in_specs=[
            pl.BlockSpec((1, n_pos, n_ch), lambda b, i: (b, 0, 0)),
            pl.BlockSpec((n_g, POOL_CH, POOL_CH), lambda b, i: (0, 0, 0)),
            pl.BlockSpec((1, n_ch), lambda b, i: (0, 0)),
        ],
        out_specs=pl.BlockSpec((1, ts, n_ch), lambda b, i: (b, i, 0)),
        out_shape=jax.ShapeDtypeStruct((bsz, n_pos, n_ch), BF16),
        compiler_params=_params(("parallel", "arbitrary")),
        name="pool_module",
    )(u, pool_w.astype(BF16), pool_scale.astype(F32).reshape(1, n_ch))


def _rope_tables(n_pos):
    inv = ROPE_THETA ** (-jnp.arange(0, HEAD_DIM, 2, dtype=F32) / HEAD_DIM)
    ang = jnp.arange(n_pos, dtype=F32)[:, None] * inv[None, :]
    ang = jnp.concatenate([ang, ang], axis=-1)
    return jnp.cos(ang), jnp.sin(ang)


def _lambda_init(layer_idx):
    return 0.8 - 0.6 * math.exp(-0.3 * layer_idx)


def kernel(x, router_w, router_bias, l0_in_proj, l0_conv_w, l0_conv_b, l0_conv_ln_g, l0_conv_ln_b, l0_lambda_q1, l0_lambda_k1, l0_lambda_q2, l0_lambda_k2, l0_subln_g, l0_out_proj, l0_ln_mix_g, l0_ln_mix_b, l0_w_gate, l0_w_up, l0_w_down, l0_ln_ffn_g, l0_ln_ffn_b, l1_in_proj, l1_pool_w, l1_pool_scale, l1_out_proj, l1_ln_mix_g, l1_ln_mix_b, l1_w_gate, l1_w_up, l1_w_down, l1_ln_ffn_g, l1_ln_ffn_b):
    bsz, n_pos, d = x.shape
    n_tok = bsz * n_pos
    x = x.astype(F32)

    conv_ch = l0_conv_w.shape[1]
    dv = l0_subln_g.shape[0]
    n_diff = (l0_out_proj.shape[0] - conv_ch) // dv
    qk_w = n_diff * 2 * HEAD_DIM
    v_w = n_diff * dv

    tm = min(512, n_pos)
    tq_diff = min(512, n_pos)
    tq_sb = min(512, n_pos)
    tk_sb = min(256, n_pos)
    ts = min(256, n_pos)
    tmc = min(256, n_pos)

    cos, sin = _rope_tables(n_pos)
    sign = jnp.where(jnp.arange(HEAD_DIM) < HEAD_DIM // 2, -1.0, 1.0).astype(F32)
    cos128 = jnp.concatenate([cos, cos], axis=-1)
    sins128 = jnp.concatenate([sin * sign, sin * sign], axis=-1)
    cost = cos.T
    sint = sin.T

    glu, qt, k, vt = _inproj0(x, l0_in_proj.astype(BF16), cos128, sins128, cost, sint,
                              conv_ch=conv_ch, qk_w=qk_w, v_w=v_w, tm=tm)
    a = _conv_module(glu, l0_conv_w.astype(F32), l0_conv_b.astype(F32),
                     l0_conv_ln_g.astype(F32), l0_conv_ln_b.astype(F32), ts=ts)
    lam_init = _lambda_init(0)
    lam = (jnp.exp(jnp.sum(l0_lambda_q1.astype(F32) * l0_lambda_k1.astype(F32)))
           - jnp.exp(jnp.sum(l0_lambda_q2.astype(F32) * l0_lambda_k2.astype(F32))) + lam_init)
    o = _diff_attention(qt, k, vt, lam, l0_subln_g, n_heads=n_diff, lam_init=lam_init,
                        tq=tq_diff, tk=tq_diff)
    xf = x.reshape(n_tok, d)
    y, route, gates, counts = _outproj_ln_route(
        a.reshape(n_tok, conv_ch), o.reshape(n_tok, v_w), l0_out_proj, xf,
        l0_ln_mix_g.astype(F32), l0_ln_mix_b.astype(F32), router_w, router_bias, tm=tm)
    xf = _moe_sparse(y, route, gates, counts, l0_w_gate, l0_w_up, l0_w_down,
                     l0_ln_ffn_g.astype(F32), l0_ln_ffn_b.astype(F32), tmd=tm, tmc=tmc)

    sb_w = l1_in_proj.shape[1] - len(POOL_WINDOWS) * POOL_CH
    sb_w = sb_w // 3
    qt, k, vt, u = _inproj1(xf.reshape(bsz, n_pos, d), l1_in_proj.astype(BF16), sb_w=sb_w, tm=tm)
    o = _sb_attention(qt, k, vt, tq=tq_sb, tk=tk_sb)
    dpool = _pool_module(u, l1_pool_w, l1_pool_scale, ts=ts)
    y, route, gates, counts = _outproj_ln_route(
        o.reshape(n_tok, sb_w), dpool.reshape(n_tok, -1), l1_out_proj, xf,
        l1_ln_mix_g.astype(F32), l1_ln_mix_b.astype(F32), router_w, router_bias, tm=tm)
    xf = _moe_sparse(y, route, gates, counts, l1_w_gate, l1_w_up, l1_w_down,
                     l1_ln_ffn_g.astype(F32), l1_ln_ffn_b.astype(F32), tmd=tm, tmc=tmc)
    return xf.reshape(bsz, n_pos, d)
```

```python
import functools
import math

import jax
import jax.numpy as jnp
from jax import lax
from jax.experimental import pallas as pl
from jax.experimental.pallas import tpu as pltpu

F32 = jnp.float32
BF16 = jnp.bfloat16

DEPTH = 2
HEAD_DIM = 64
ROPE_THETA = 10000.0
LN_EPS = 1e-5
RMS_EPS = 1e-5
DEEPNORM_ALPHA = (2 * DEPTH) ** 0.25
CONV_WIDTH = 31
POOL_WINDOWS = (2, 4, 8, 16)
POOL_CH = 128
N_EXPERTS = 16
N_GROUPS = 4
EXPERTS_PER_GROUP = N_EXPERTS // N_GROUPS
ATTN_SCALE = HEAD_DIM ** -0.5
LOG2E = math.log2(math.e)

LANES = 128
SUBLANES = 8
MIB = 1024 * 1024
VMEM_LIMIT = 52 * MIB

NEG_BIG = -1e30
SB_SCAN_ROWS = 128
SB_UNDERFLOW = 110.0


def _params(semantics, vmem=VMEM_LIMIT):
    return pltpu.CompilerParams(dimension_semantics=semantics, vmem_limit_bytes=vmem)


def _layer_norm_rows(h, g, b):
    mu = jnp.mean(h, axis=-1, keepdims=True)
    d = h - mu
    var = jnp.mean(d * d, axis=-1, keepdims=True)
    return d * lax.rsqrt(var + LN_EPS) * g + b


def _load_token_tiles(ref, row0, n, d):
    k = d // LANES
    return jnp.concatenate([ref[pl.ds(row0 + c, n, stride=k), :] for c in range(k)], axis=1)


def _store_token_tiles(ref, row0, val):
    n, d = val.shape
    k = d // LANES
    for c in range(k):
        ref[pl.ds(row0 + c, n, stride=k), :] = val[:, c * LANES:(c + 1) * LANES]


def _inproj0_kernel(x_ref, w_ref, cos_ref, sins_ref, cost_ref, sint_ref,
                    glu_ref, qt_ref, k_ref, vt_ref, *, conv_ch, qk_w):
    x = x_ref[0].astype(BF16)
    c0, c1, c2, c3 = conv_ch, 2 * conv_ch, 2 * conv_ch + qk_w, 2 * conv_ch + 2 * qk_w

    a_val = jnp.dot(x, w_ref[:, 0:c0], preferred_element_type=F32)
    a_gate = jnp.dot(x, w_ref[:, c0:c1], preferred_element_type=F32)
    glu_ref[0] = a_val * jax.nn.sigmoid(a_gate)

    dq = jnp.dot(x, w_ref[:, c1:c2], preferred_element_type=F32)
    dqt = dq.T
    cost = cost_ref[...]
    sint = sint_ref[...]
    half = HEAD_DIM // 2
    pieces = []
    for blk in range(qk_w // HEAD_DIM):
        u = dqt[blk * HEAD_DIM:(blk + 1) * HEAD_DIM]
        rot = jnp.concatenate([-u[half:], u[:half]], axis=0)
        pieces.append((u * cost + rot * sint) * (ATTN_SCALE * LOG2E))
    qt_ref[0] = jnp.concatenate(pieces, axis=0).astype(BF16)

    dk = jnp.dot(x, w_ref[:, c2:c3], preferred_element_type=F32)
    cos = cos_ref[...]
    sins = sins_ref[...]
    lane = lax.broadcasted_iota(jnp.int32, cos.shape, 1)
    first = (lane & (HEAD_DIM - 1)) < half
    kparts = []
    for h in range(qk_w // LANES):
        u = dk[:, h * LANES:(h + 1) * LANES]
        rot = jnp.where(first, pltpu.roll(u, LANES - half, 1), pltpu.roll(u, half, 1))
        kparts.append(u * cos + rot * sins)
    k_ref[0] = jnp.concatenate(kparts, axis=1).astype(BF16)

    dv = jnp.dot(x, w_ref[:, c3:], preferred_element_type=F32)
    vt_ref[0] = dv.T.astype(BF16)


def _inproj0(x, w_bf, cos128, sins128, cost, sint, *, conv_ch, qk_w, v_w, tm):
    bsz, n_pos, d = x.shape
    n_in = w_bf.shape[1]
    kern = functools.partial(_inproj0_kernel, conv_ch=conv_ch, qk_w=qk_w)
    return pl.pallas_call(
        kern,
        grid=(bsz, n_pos // tm),
        in_specs=[
            pl.BlockSpec((1, tm, d), lambda b, i: (b, i, 0)),
            pl.BlockSpec((d, n_in), lambda b, i: (0, 0)),
            pl.BlockSpec((tm, LANES), lambda b, i: (i, 0)),
            pl.BlockSpec((tm, LANES), lambda b, i: (i, 0)),
            pl.BlockSpec((HEAD_DIM, tm), lambda b, i: (0, i)),
            pl.BlockSpec((HEAD_DIM, tm), lambda b, i: (0, i)),
        ],
        out_specs=[
            pl.BlockSpec((1, tm, conv_ch), lambda b, i: (b, i, 0)),
            pl.BlockSpec((1, qk_w, tm), lambda b, i: (b, 0, i)),
            pl.BlockSpec((1, tm, qk_w), lambda b, i: (b, i, 0)),
            pl.BlockSpec((1, v_w, tm), lambda b, i: (b, 0, i)),
        ],
        out_shape=[
            jax.ShapeDtypeStruct((bsz, n_pos, conv_ch), F32),
            jax.ShapeDtypeStruct((bsz, qk_w, n_pos), BF16),
            jax.ShapeDtypeStruct((bsz, n_pos, qk_w), BF16),
            jax.ShapeDtypeStruct((bsz, v_w, n_pos), BF16),
        ],
        compiler_params=_params(("parallel", "arbitrary")),
        name="inproj0",
    )(x, w_bf, cos128, sins128, cost, sint)


CONV_HALO = 32


def _conv_kernel(a_ref, cw_ref, cb_ref, g_ref, b_ref, o_ref, win_ref, *, ts):
    i = pl.program_id(1)
    t0 = pl.multiple_of(i * ts, ts)
    h0 = pl.multiple_of(jnp.maximum(t0 - CONV_HALO, 0), CONV_HALO)
    n_ch = a_ref.shape[2]
    n_shift = ts + CONV_HALO - SUBLANES
    outs = []
    for c in range(n_ch // LANES):
        cs = slice(c * LANES, (c + 1) * LANES)
        cur = a_ref[0, pl.ds(t0, ts), cs]
        halo = a_ref[0, pl.ds(h0, CONV_HALO), cs]
        halo = jnp.where(i > 0, halo, 0.0)
        xw = jnp.concatenate([halo, cur], axis=0)
        win_ref[c, 0] = xw
        for r in range(1, SUBLANES):
            win_ref[c, r, 0:n_shift] = xw[r:r + n_shift]
        acc = jnp.zeros((ts, LANES), F32) + cb_ref[:, cs]
        for r in range(SUBLANES):
            rows = xw.shape[0] if r == 0 else n_shift
            for a in range(CONV_HALO // SUBLANES + 1):
                w = SUBLANES * a + r - (CONV_HALO - CONV_WIDTH + 1)
                if 0 <= w < CONV_WIDTH and SUBLANES * a + ts <= rows:
                    tap = win_ref[c, r, SUBLANES * a:SUBLANES * a + ts]
                    acc = acc + tap * cw_ref[w:w + 1, cs]
        outs.append(acc)
    y = jnp.concatenate(outs, axis=1)
    y = _layer_norm_rows(y, g_ref[...], b_ref[...])
    o_ref[0] = (y * jax.nn.sigmoid(y)).astype(o_ref.dtype)


def _conv_module(a, conv_w, conv_b, ln_g, ln_b, *, ts):
    bsz, n_pos, n_ch = a.shape
    return pl.pallas_call(
        functools.partial(_conv_kernel, ts=ts),
        grid=(bsz, n_pos // ts),
        in_specs=[
            pl.BlockSpec((1, n_pos, n_ch), lambda b, i: (b, 0, 0)),
            pl.BlockSpec((CONV_WIDTH, n_ch), lambda b, i: (0, 0)),
            pl.BlockSpec((1, n_ch), lambda b, i: (0, 0)),
            pl.BlockSpec((1, n_ch), lambda b, i: (0, 0)),
            pl.BlockSpec((1, n_ch), lambda b, i: (0, 0)),
        ],
        out_specs=pl.BlockSpec((1, ts, n_ch), lambda b, i: (b, i, 0)),
        out_shape=jax.ShapeDtypeStruct((bsz, n_pos, n_ch), BF16),
        scratch_shapes=[pltpu.VMEM((n_ch // LANES, SUBLANES, ts + CONV_HALO, LANES), F32)],
        compiler_params=_params(("parallel", "arbitrary")),
        name="conv_module",
    )(a, conv_w, conv_b.reshape(1, n_ch), ln_g.reshape(1, n_ch), ln_b.reshape(1, n_ch))


def _diff_attn_kernel(lam_ref, qt_ref, k_ref, vt_ref, g_ref, o_ref, *, tq, tk, out_scale):
    i = pl.program_id(2)
    lam = lam_ref[0, 0]
    qt = qt_ref[0]
    zeros = jnp.zeros((HEAD_DIM, tq), BF16)
    qa = jnp.concatenate([qt[:HEAD_DIM], zeros], axis=0)
    qb = jnp.concatenate([zeros, qt[HEAD_DIM:]], axis=0)
    dv = vt_ref.shape[1]

    def online(s, vj, m, l, acc):
        m_new = jnp.maximum(m, jnp.max(s, axis=0, keepdims=True))
        alpha = jnp.exp2(m - m_new)
        p = jnp.exp2(s - m_new)
        l_new = alpha * l + jnp.sum(p, axis=0, keepdims=True)
        acc_new = alpha * acc + jnp.dot(vj, p.astype(BF16), preferred_element_type=F32)
        return m_new, l_new, acc_new

    def block(j, carry, masked):
        m1, l1, acc1, m2, l2, acc2 = carry
        k0 = pl.multiple_of(j * tk, tk)
        kj = k_ref[0, pl.ds(k0, tk), :]
        vj = vt_ref[0, :, pl.ds(k0, tk)]
        s1 = jnp.dot(kj, qa, preferred_element_type=F32)
        s2 = jnp.dot(kj, qb, preferred_element_type=F32)
        if masked:
            key = k0 + lax.broadcasted_iota(jnp.int32, (tk, tq), 0)
            qpos = i * tq + lax.broadcasted_iota(jnp.int32, (tk, tq), 1)
            keep = key <= qpos
            s1 = jnp.where(keep, s1, NEG_BIG)
            s2 = jnp.where(keep, s2, NEG_BIG)
        m1, l1, acc1 = online(s1, vj, m1, l1, acc1)
        m2, l2, acc2 = online(s2, vj, m2, l2, acc2)
        return m1, l1, acc1, m2, l2, acc2

    row = jnp.full((1, tq), NEG_BIG, F32)
    zrow = jnp.zeros((1, tq), F32)
    zacc = jnp.zeros((dv, tq), F32)
    carry = (row, zrow, zacc, row, zrow, zacc)
    carry = lax.fori_loop(0, i, lambda j, c: block(j, c, False), carry)
    m1, l1, acc1, m2, l2, acc2 = block(i, carry, True)

    o = acc1 / l1 - lam * (acc2 / l2)
    o = o * lax.rsqrt(jnp.mean(o * o, axis=0, keepdims=True) + RMS_EPS)
    o = o * g_ref[...] * out_scale
    o_ref[0] = o.T.astype(o_ref.dtype)


def _diff_attention(qt, k, vt, lam, subln_g, *, n_heads, lam_init, tq, tk):
    bsz, _, n_pos = qt.shape
    dv = vt.shape[1] // n_heads
    kern = functools.partial(_diff_attn_kernel, tq=tq, tk=tk, out_scale=1.0 - lam_init)
    return pl.pallas_call(
        kern,
        grid=(bsz, n_heads, n_pos // tq),
        in_specs=[
            pl.BlockSpec(memory_space=pltpu.SMEM),
            pl.BlockSpec((1, 2 * HEAD_DIM, tq), lambda b, h, i: (b, h, i)),
            pl.BlockSpec((1, n_pos, 2 * HEAD_DIM), lambda b, h, i: (b, 0, h)),
            pl.BlockSpec((1, dv, n_pos), lambda b, h, i: (b, h, 0)),
            pl.BlockSpec((dv, 1), lambda b, h, i: (0, 0)),
        ],
        out_specs=pl.BlockSpec((1, tq, dv), lambda b, h, i: (b, i, h)),
        out_shape=jax.ShapeDtypeStruct((bsz, n_pos, n_heads * dv), BF16),
        compiler_params=_params(("parallel", "parallel", "arbitrary")),
        name="diff_attention",
    )(lam.reshape(1, 1), qt, k, vt, subln_g.astype(F32).reshape(dv, 1))


def _masked_top2(vals, member, row):
    big = N_EXPERTS
    v = jnp.where(member, vals, -jnp.inf)
    v1 = jnp.max(v, axis=0, keepdims=True)
    i1 = jnp.min(jnp.where(v == v1, row, big), axis=0, keepdims=True)
    v = jnp.where(row == i1, -jnp.inf, v)
    v2 = jnp.max(v, axis=0, keepdims=True)
    i2 = jnp.min(jnp.where(v == v2, row, big), axis=0, keepdims=True)
    return v1, i1, v2, i2


def _top2_route(aff, sel):
    row = lax.broadcasted_iota(jnp.int32, sel.shape, 0)
    group = lax.shift_right_logical(row, EXPERTS_PER_GROUP.bit_length() - 1)
    gbest = None
    for g in range(N_GROUPS):
        v1, _, v2, _ = _masked_top2(sel, group == g, row)
        score = v1 + v2
        if gbest is None:
            gbest, gval = jnp.zeros_like(score, dtype=jnp.int32), score
        else:
            better = score > gval
            gbest = jnp.where(better, g, gbest)
            gval = jnp.where(better, score, gval)
    _, e1, _, e2 = _masked_top2(sel, group == gbest, row)
    a1 = jnp.sum(jnp.where(row == e1, aff, 0.0), axis=0, keepdims=True)
    a2 = jnp.sum(jnp.where(row == e2, aff, 0.0), axis=0, keepdims=True)
    denom = a1 + a2
    return e1, e2, a1 / denom, a2 / denom


ROUTE_ROWS = SUBLANES


def _outproj_kernel(pa_ref, pb_ref, wa_ref, wb_ref, x_ref, g_ref, b_ref, rw_ref, rb_ref,
                    y_ref, route_ref, gate_ref, count_ref, seen_ref):
    i = pl.program_id(0)

    @pl.when(i == 0)
    def _():
        seen_ref[...] = jnp.zeros_like(seen_ref)

    mix = jnp.dot(pa_ref[...], wa_ref[...], preferred_element_type=F32)
    mix = mix + jnp.dot(pb_ref[...], wb_ref[...], preferred_element_type=F32)
    y = _layer_norm_rows(DEEPNORM_ALPHA * x_ref[...] + mix, g_ref[...], b_ref[...])
    _store_token_tiles(y_ref, 0, y)
    tm = y.shape[0]
    logits = lax.dot_general(rw_ref[...], y, (((1,), (1,)), ((), ())),
                             precision=lax.Precision.HIGHEST, preferred_element_type=F32)
    aff = jax.nn.sigmoid(logits)
    e1, e2, g1, g2 = _top2_route(aff, aff + rb_ref[...])

    erow = lax.broadcasted_iota(jnp.int32, (N_EXPERTS, tm), 0)
    hit = jnp.where((erow == e1) | (erow == e2), 1.0, 0.0)
    upto = (lax.broadcasted_iota(jnp.int32, (tm, tm), 0)
            <= lax.broadcasted_iota(jnp.int32, (tm, tm), 1)).astype(BF16)
    incl = jnp.dot(hit.astype(BF16), upto, preferred_element_type=F32)
    before = seen_ref[:, 0:1] + incl - hit
    r1 = jnp.sum(jnp.where(erow == e1, before, 0.0), axis=0, keepdims=True).astype(jnp.int32)
    r2 = jnp.sum(jnp.where(erow == e2, before, 0.0), axis=0, keepdims=True).astype(jnp.int32)
    seen_ref[...] = seen_ref[...] + jnp.sum(hit, axis=1, keepdims=True)
    count_ref[...] = seen_ref[...]

    rrow = lax.broadcasted_iota(jnp.int32, (ROUTE_ROWS, tm), 0)
    route_ref[0] = jnp.where(rrow == 0, e1, jnp.where(rrow == 1, e2,
                             jnp.where(rrow == 2, r1, jnp.where(rrow == 3, r2, 0))))
    grow = lax.broadcasted_iota(jnp.int32, (LANES, tm), 0)
    gate_ref[...] = jnp.where(grow == 0, g1, jnp.where(grow == 1, g2, 0.0)).T


def _outproj_ln_route(pa, pb, w_out, x, ln_g, ln_b, router_w, router_bias, *, tm):
    n_tok, d = x.shape
    ka, kb = pa.shape[1], pb.shape[1]
    wa = w_out[:ka].astype(BF16)
    wb = w_out[ka:].astype(BF16)
    rows = lambda i: (i, 0)
    fixed = lambda i: (0, 0)
    return pl.pallas_call(
        _outproj_kernel,
        grid=(n_tok // tm,),
        in_specs=[
            pl.BlockSpec((tm, ka), rows),
            pl.BlockSpec((tm, kb), rows),
            pl.BlockSpec((ka, d), fixed),
            pl.BlockSpec((kb, d), fixed),
            pl.BlockSpec((tm, d), rows),
            pl.BlockSpec((1, d), fixed),
            pl.BlockSpec((1, d), fixed),
            pl.BlockSpec((N_EXPERTS, d), fixed),
            pl.BlockSpec((N_EXPERTS, 1), fixed),
        ],
        out_specs=[
            pl.BlockSpec((tm * d // LANES, LANES), rows),
            pl.BlockSpec((1, ROUTE_ROWS, tm), lambda i: (i, 0, 0)),
            pl.BlockSpec((tm, LANES), rows),
            pl.BlockSpec((N_EXPERTS, LANES), fixed),
        ],
        out_shape=[
            jax.ShapeDtypeStruct((n_tok * d // LANES, LANES), F32),
            jax.ShapeDtypeStruct((n_tok // tm, ROUTE_ROWS, tm), jnp.int32),
            jax.ShapeDtypeStruct((n_tok, LANES), F32),
            jax.ShapeDtypeStruct((N_EXPERTS, LANES), F32),
        ],
        scratch_shapes=[pltpu.VMEM((N_EXPERTS, LANES), F32)],
        compiler_params=_params(("arbitrary",)),
        name="outproj_ln_route",
    )(pa, pb, wa, wb, x, ln_g.reshape(1, d), ln_b.reshape(1, d),
      router_w.astype(F32).T, router_bias.astype(F32).reshape(N_EXPERTS, 1))


MOE_ROW_TILE = 512
DMA_UNROLL = 8


def _row_copy_wait(src_hbm, dst, sem, n_rows):
    pltpu.make_async_copy(src_hbm.at[pl.ds(0, n_rows)], dst, sem).wait()


def _dispatch_kernel(offend_ref, pos_ref, y_ref, xs_hbm, zero_ref, sem, zsem, *, tmd, tr):
    i = pl.program_id(0)
    tile_rows = MOE_ROW_TILE * tr

    @pl.when(i == 0)
    def _():
        zero_ref[...] = jnp.zeros_like(zero_ref)
        def tail_copy(e):
            start = pl.multiple_of((offend_ref[e] - MOE_ROW_TILE) * tr, tile_rows)
            return pltpu.make_async_copy(zero_ref, xs_hbm.at[pl.ds(start, tile_rows)], zsem)
        def nonempty(e):
            return offend_ref[e] > (offend_ref[e - 1] if e > 0 else 0)
        for e in range(N_EXPERTS):
            @pl.when(nonempty(e))
            def _():
                tail_copy(e).start()
        for e in range(N_EXPERTS):
            @pl.when(nonempty(e))
            def _():
                tail_copy(e).wait()
        def unused_copy(j):
            return pltpu.make_async_copy(
                zero_ref, xs_hbm.at[pl.ds(pl.multiple_of(j * tile_rows, tile_rows), tile_rows)],
                zsem)
        first_unused = lax.shift_right_logical(offend_ref[N_EXPERTS - 1],
                                               MOE_ROW_TILE.bit_length() - 1)
        n_tiles = xs_hbm.shape[0] // tile_rows
        lax.fori_loop(first_unused, n_tiles, lambda j, c: (unused_copy(j).start(), c)[1], 0)
        lax.fori_loop(first_unused, n_tiles, lambda j, c: (unused_copy(j).wait(), c)[1], 0)

    def issue(g, c):
        r0 = g * DMA_UNROLL
        ps = [[pos_ref[0, 0, k * tmd + r0 + j] for k in range(2)] for j in range(DMA_UNROLL)]
        for j in range(DMA_UNROLL):
            src = y_ref.at[pl.ds(pl.multiple_of((r0 + j) * tr, tr), tr)]
            for k in range(2):
                dst = xs_hbm.at[pl.ds(pl.multiple_of(ps[j][k] * tr, tr), tr)]
                pltpu.make_async_copy(src, dst, sem).start(priority=k)
        return c

    lax.fori_loop(0, tmd // DMA_UNROLL, issue, 0)
    for k in range(2):
        pltpu.make_async_copy(y_ref, xs_hbm.at[pl.ds(0, tmd * tr)], sem).wait()


def _dispatch(y_tok, pos, off_end, *, n_sorted, tmd, tr):
    n_tok = y_tok.shape[0] // tr
    return pl.pallas_call(
        functools.partial(_dispatch_kernel, tmd=tmd, tr=tr),
        grid_spec=pltpu.PrefetchScalarGridSpec(
            num_scalar_prefetch=1,
            grid=(n_tok // tmd,),
            in_specs=[
                pl.BlockSpec((1, 1, 2 * tmd), lambda i, off: (i, 0, 0), memory_space=pltpu.SMEM),
                pl.BlockSpec((tmd * tr, LANES), lambda i, off: (i, 0)),
            ],
            out_specs=pl.BlockSpec(memory_space=pl.ANY),
            scratch_shapes=[pltpu.VMEM((MOE_ROW_TILE * tr, LANES), F32),
                            pltpu.SemaphoreType.DMA, pltpu.SemaphoreType.DMA],
        ),
        out_shape=jax.ShapeDtypeStruct((n_sorted * tr, LANES), F32),
        compiler_params=_params(("arbitrary",)),
        name="moe_dispatch",
    )(off_end, pos, y_tok)


def _experts_kernel(texp_ref, nused_ref, xs_ref, wg_ref, wu_ref, wd_ref, ys_ref,
                    wg_bf, wu_bf, wd_bf):
    i = pl.program_id(0)
    used = i < nused_ref[0]

    @pl.when(jnp.logical_not(used))
    def _():
        ys_ref[...] = jnp.zeros_like(ys_ref)

    new_expert = jnp.logical_or(i == 0, texp_ref[i] != texp_ref[jnp.maximum(i - 1, 0)])

    @pl.when(jnp.logical_and(used, new_expert))
    def _():
        wg_bf[...] = wg_ref[0].astype(BF16)
        wu_bf[...] = wu_ref[0].astype(BF16)
        wd_bf[...] = wd_ref[0].astype(BF16)

    @pl.when(used)
    def _():
        d = wg_ref.shape[1]
        x = _load_token_tiles(xs_ref, 0, MOE_ROW_TILE, d).astype(BF16)
        hg = jnp.dot(x, wg_bf[...], preferred_element_type=F32)
        hu = jnp.dot(x, wu_bf[...], preferred_element_type=F32)
        h = (hg * jax.nn.sigmoid(hg)) * hu
        _store_token_tiles(ys_ref, 0, jnp.dot(h.astype(BF16), wd_bf[...],
                                              preferred_element_type=F32))


def _experts(xs, tile_expert, n_used, w_gate, w_up, w_down):
    _, d, dff = w_gate.shape
    tile_rows = MOE_ROW_TILE * d // LANES
    tile = lambda i, te, nu: (jnp.minimum(i, nu[0] - 1), 0)
    every = lambda i, te, nu: (i, 0)
    wsel = lambda i, te, nu: (te[jnp.minimum(i, nu[0] - 1)], 0, 0)
    return pl.pallas_call(
        _experts_kernel,
        grid_spec=pltpu.PrefetchScalarGridSpec(
            num_scalar_prefetch=2,
            grid=(xs.shape[0] // tile_rows,),
            in_specs=[
                pl.BlockSpec((tile_rows, LANES), tile),
                pl.BlockSpec((1, d, dff), wsel),
                pl.BlockSpec((1, d, dff), wsel),
                pl.BlockSpec((1, dff, d), wsel),
            ],
            out_specs=pl.BlockSpec((tile_rows, LANES), every),
            scratch_shapes=[pltpu.VMEM((d, dff), BF16), pltpu.VMEM((d, dff), BF16),
                            pltpu.VMEM((dff, d), BF16)],
        ),
        out_shape=jax.ShapeDtypeStruct(xs.shape, F32),
        compiler_params=_params(("arbitrary",)),
        name="moe_experts",
    )(tile_expert, n_used, xs, w_gate, w_up, w_down)


def _combine_kernel(pos_ref, posn_ref, y_ref, gate_ref, g_ref, b_ref, ys_hbm, o_ref,
                    buf, sem, *, tmc, tr):
    i = pl.program_id(0)
    n = pl.num_programs(0)
    d = o_ref.shape[1]
    half = tmc * tr
    slot_rows = 2 * half

    def gather(p_ref, slot):
        base = slot * slot_rows
        def issue(g, c):
            r0 = g * DMA_UNROLL
            ps = [[p_ref[0, 0, k * tmc + r0 + j] for k in range(2)] for j in range(DMA_UNROLL)]
            for j in range(DMA_UNROLL):
                for k in range(2):
                    src = ys_hbm.at[pl.ds(pl.multiple_of(ps[j][k] * tr, tr), tr)]
                    dst = buf.at[pl.ds(pl.multiple_of(base + k * half + (r0 + j) * tr, tr), tr)]
                    pltpu.make_async_copy(src, dst, sem.at[slot]).start(priority=k)
            return c
        lax.fori_loop(0, tmc // DMA_UNROLL, issue, 0)

    @pl.when(i == 0)
    def _():
        gather(pos_ref, 0)

    @pl.when(i + 1 < n)
    def _():
        gather(posn_ref, (i + 1) % 2)

    slot = i % 2
    base = pl.multiple_of(slot * slot_rows, slot_rows)
    _row_copy_wait(ys_hbm, buf.at[pl.ds(base, slot_rows)], sem.at[slot], slot_rows)
    gate = gate_ref[...]
    ffn = (gate[:, 0:1] * _load_token_tiles(buf, base, tmc, d)
           + gate[:, 1:2] * _load_token_tiles(buf, base + half, tmc, d))
    y = _load_token_tiles(y_ref, 0, tmc, d)
    o_ref[...] = _layer_norm_rows(DEEPNORM_ALPHA * y + ffn, g_ref[...], b_ref[...])


def _combine(ys, pos, y_tok, gates, ln_g, ln_b, *, tmc, tr):
    d = tr * LANES
    n_tok = y_tok.shape[0] // tr
    n = n_tok // tmc
    rows = lambda i: (i, 0)
    fixed = lambda i: (0, 0)
    return pl.pallas_call(
        functools.partial(_combine_kernel, tmc=tmc, tr=tr),
        grid=(n,),
        in_specs=[
            pl.BlockSpec((1, 1, 2 * tmc), lambda i: (i, 0, 0), memory_space=pltpu.SMEM),
            pl.BlockSpec((1, 1, 2 * tmc), lambda i: (jnp.minimum(i + 1, n - 1), 0, 0),
                         memory_space=pltpu.SMEM),
            pl.BlockSpec((tmc * tr, LANES), rows),
            pl.BlockSpec((tmc, LANES), rows),
            pl.BlockSpec((1, d), fixed),
            pl.BlockSpec((1, d), fixed),
            pl.BlockSpec(memory_space=pl.ANY),
        ],
        out_specs=pl.BlockSpec((tmc, d), rows),
        out_shape=jax.ShapeDtypeStruct((n_tok, d), F32),
        scratch_shapes=[pltpu.VMEM((2 * 2 * tmc * tr, LANES), F32),
                        pltpu.SemaphoreType.DMA((2,))],
        compiler_params=_params(("arbitrary",)),
        name="moe_combine",
    )(pos, pos, y_tok, gates, ln_g.reshape(1, d), ln_b.reshape(1, d), ys)


def _tile_positions(route, off, t):
    e = route[:, 0:2, :]
    hit = e[..., None] == jnp.arange(N_EXPERTS, dtype=jnp.int32)
    pos = jnp.sum(jnp.where(hit, off, 0), axis=-1) + route[:, 2:4, :]
    pos = pos.transpose(1, 0, 2).reshape(2, -1, t)
    return pos.transpose(1, 0, 2).reshape(-1, 1, 2 * t)


def _moe_sparse(y_tok, route, gates, counts, w_gate, w_up, w_down, ln_g, ln_b, *, tmd, tmc):
    d = w_gate.shape[1]
    tr = d // LANES
    n_tok = y_tok.shape[0] // tr
    cnt = counts[:, 0].astype(jnp.int32)
    padded = (cnt + MOE_ROW_TILE - 1) // MOE_ROW_TILE * MOE_ROW_TILE
    off_end = jnp.cumsum(padded)
    off = off_end - padded
    n_sorted = 2 * n_tok + N_EXPERTS * MOE_ROW_TILE
    n_tiles = n_sorted // MOE_ROW_TILE
    n_used = (off_end[-1:] // MOE_ROW_TILE).astype(jnp.int32)
    tile_start = jnp.arange(n_tiles, dtype=jnp.int32) * MOE_ROW_TILE
    tile_expert = jnp.minimum(jnp.sum(tile_start[:, None] >= off_end[None, :], axis=1),
                              N_EXPERTS - 1).astype(jnp.int32)
    xs = _dispatch(y_tok, _tile_positions(route, off, tmd), off_end.astype(jnp.int32),
                   n_sorted=n_sorted, tmd=tmd, tr=tr)
    ys = _experts(xs, tile_expert, n_used, w_gate, w_up, w_down)
    return _combine(ys, _tile_positions(route, off, tmc), y_tok, gates, ln_g, ln_b,
                    tmc=tmc, tr=tr)


def _inproj1_kernel(x_ref, w_ref, qt_ref, k_ref, vt_ref, u_ref, *, sb_w):
    x = x_ref[0].astype(BF16)
    q = jnp.dot(x, w_ref[:, 0:sb_w], preferred_element_type=F32) * ATTN_SCALE
    qt_ref[0] = q.T.astype(BF16)
    k_ref[0] = jnp.dot(x, w_ref[:, sb_w:2 * sb_w], preferred_element_type=F32).astype(BF16)
    v = jnp.dot(x, w_ref[:, 2 * sb_w:3 * sb_w], preferred_element_type=F32)
    vt_ref[0] = v.T.astype(BF16)
    u_ref[0] = jnp.dot(x, w_ref[:, 3 * sb_w:], preferred_element_type=F32)


def _inproj1(x, w_bf, *, sb_w, tm):
    bsz, n_pos, d = x.shape
    n_in = w_bf.shape[1]
    pool_w = n_in - 3 * sb_w
    return pl.pallas_call(
        functools.partial(_inproj1_kernel, sb_w=sb_w),
        grid=(bsz, n_pos // tm),
        in_specs=[
            pl.BlockSpec((1, tm, d), lambda b, i: (b, i, 0)),
            pl.BlockSpec((d, n_in), lambda b, i: (0, 0)),
        ],
        out_specs=[
            pl.BlockSpec((1, sb_w, tm), lambda b, i: (b, 0, i)),
            pl.BlockSpec((1, tm, sb_w), lambda b, i: (b, i, 0)),
            pl.BlockSpec((1, sb_w, tm), lambda b, i: (b, 0, i)),
            pl.BlockSpec((1, tm, pool_w), lambda b, i: (b, i, 0)),
        ],
        out_shape=[
            jax.ShapeDtypeStruct((bsz, sb_w, n_pos), BF16),
            jax.ShapeDtypeStruct((bsz, n_pos, sb_w), BF16),
            jax.ShapeDtypeStruct((bsz, sb_w, n_pos), BF16),
            jax.ShapeDtypeStruct((bsz, n_pos, pool_w), F32),
        ],
        compiler_params=_params(("parallel", "arbitrary")),
        name="inproj1",
    )(x, w_bf)


def _sb_attn_kernel(qt_ref, k_ref, vt_ref, o_ref, *, tq, tk):
    i = pl.program_id(2)
    qt = qt_ref[0]
    zeros = jnp.zeros((HEAD_DIM, tq), BF16)
    qpads = (jnp.concatenate([qt[:HEAD_DIM], zeros], axis=0),
             jnp.concatenate([zeros, qt[HEAD_DIM:]], axis=0))
    cb = SB_SCAN_ROWS
    upper2 = ((lax.broadcasted_iota(jnp.int32, (cb, 2 * cb), 1) & (cb - 1))
              > lax.broadcasted_iota(jnp.int32, (cb, 2 * cb), 0)).astype(BF16)

    def block(j, carry, masked):
        k0 = pl.multiple_of(j * tk, tk)
        kj = k_ref[0, pl.ds(k0, tk), :]
        if masked:
            key = k0 + lax.broadcasted_iota(jnp.int32, (tk, tq), 0)
            qpos = i * tq + lax.broadcasted_iota(jnp.int32, (tk, tq), 1)
            keep = key < qpos
        heads = range(2)
        zs = [jnp.dot(kj, qpads[hh], preferred_element_type=F32) for hh in heads]
        sps = [jnp.maximum(z, 0.0) + jnp.log(1.0 + jnp.exp2(jnp.abs(z) * (-LOG2E))) for z in zs]
        if masked:
            sps = [jnp.where(keep, sp, 0.0) for sp in sps]
        his = [sp.astype(BF16) for sp in sps]
        los = [(sp - hi.astype(F32)).astype(BF16) for sp, hi in zip(sps, his)]
        laters = [carry[hh][1] for hh in heads]
        pieces = [[None] * (tk // cb) for _ in heads]
        for c in reversed(range(tk // cb)):
            rs = slice(c * cb, (c + 1) * cb)
            sufs = [jnp.dot(upper2, jnp.concatenate([his[hh][rs], los[hh][rs]], axis=0),
                            preferred_element_type=F32) for hh in heads]
            for hh in heads:
                spc = sps[hh][rs]
                pieces[hh][c] = jnp.exp2((zs[hh][rs] - (spc + sufs[hh] + laters[hh])) * LOG2E)
                laters[hh] = laters[hh] + sufs[hh][0:1] + spc[0:1]
        out = []
        for hh in heads:
            a = jnp.concatenate(pieces[hh], axis=0)
            if masked:
                a = jnp.where(keep, a, 0.0)
            vj = vt_ref[0, hh * HEAD_DIM:(hh + 1) * HEAD_DIM, pl.ds(k0, tk)]
            acc = carry[hh][0] + jnp.dot(vj, a.astype(BF16), preferred_element_type=F32)
            out.append((acc, laters[hh]))
        return tuple(out)

    init = (jnp.zeros((HEAD_DIM, tq), F32), jnp.zeros((1, tq), F32))
    carry = (init, init)
    n_diag = tq // tk
    last = (i + 1) * n_diag - 1
    for d in range(n_diag):
        carry = block(last - d, carry, True)
    n_old = i * n_diag

    def reach(c):
        return jnp.min(jnp.minimum(c[0][1], c[1][1]))

    def live(state):
        t, r, _ = state
        return jnp.logical_and(t < n_old, r < SB_UNDERFLOW)

    def older(state):
        t, _, c = state
        c = block(n_old - 1 - t, c, False)
        return t + 1, reach(c), c

    _, _, carry = lax.while_loop(live, older, (jnp.int32(0), reach(carry), carry))
    o_ref[0] = jnp.concatenate([carry[0][0], carry[1][0]], axis=0).T.astype(o_ref.dtype)


def _sb_attention(qt, k, vt, *, tq, tk):
    bsz, sb_w, n_pos = qt.shape
    return pl.pallas_call(
        functools.partial(_sb_attn_kernel, tq=tq, tk=tk),
        grid=(bsz, sb_w // LANES, n_pos // tq),
        in_specs=[
            pl.BlockSpec((1, LANES, tq), lambda b, h, i: (b, h, i)),
            pl.BlockSpec((1, n_pos, LANES), lambda b, h, i: (b, 0, h)),
            pl.BlockSpec((1, LANES, n_pos), lambda b, h, i: (b, h, 0)),
        ],
        out_specs=pl.BlockSpec((1, tq, LANES), lambda b, h, i: (b, i, h)),
        out_shape=jax.ShapeDtypeStruct((bsz, n_pos, sb_w), BF16),
        compiler_params=_params(("parallel", "parallel", "arbitrary")),
        name="sb_attention",
    )(qt, k, vt)


POOL_HALO = 32


def _pool_kernel(u_ref, pw_ref, ps_ref, o_ref, *, ts):
    i = pl.program_id(1)
    t0 = pl.multiple_of(i * ts, ts)
    h0 = pl.multiple_of(jnp.maximum(t0 - POOL_HALO, 0), POOL_HALO)
    tpos = t0 + lax.broadcasted_iota(jnp.int32, (ts, 1), 0)
    outs = []
    for g, win in enumerate(POOL_WINDOWS):
        cs = slice(g * POOL_CH, (g + 1) * POOL_CH)
        cur = u_ref[0, pl.ds(t0, ts), cs]
        halo = u_ref[0, pl.ds(h0, POOL_HALO), cs]
        halo = jnp.where(i > 0, halo, 0.0)
        run = jnp.concatenate([halo, cur], axis=0)
        span = 1
        while span < win:
            n = run.shape[0] - span
            run = run[span:span + n] + run[0:n]
            span *= 2
        off = POOL_HALO - (win - 1)
        wsum = run[off:off + ts]
        count = jnp.minimum(tpos + 1, win).astype(F32)
        pooled = wsum / count - cur
        mixed = jnp.dot(pooled.astype(BF16), pw_ref[g], preferred_element_type=F32)
        outs.append(mixed * ps_ref[:, cs])
    o_ref[0] = jnp.concatenate(outs, axis=1).astype(o_ref.dtype)


def _pool_module(u, pool_w, pool_scale, *, ts):
    bsz, n_pos, n_ch = u.shape
    n_g = len(POOL_WINDOWS)
    return pl.pallas_call(
        functools.partial(_pool_kernel, ts=ts),
        grid=(bsz, n_pos // ts),
        in_specs=[
            pl.BlockSpec((1, n_pos, n_ch), lambda b, i: (b, 0, 0)),
            pl.BlockSpec((n_g, POOL_CH, POOL_CH), lambda b, i: (0, 0, 0)),
            pl.BlockSpec((1, n_ch), lambda b, i: (0, 0)),
        ],
        out_specs=pl.BlockSpec((1, ts, n_ch), lambda b, i: (b, i, 0)),
        out_shape=jax.ShapeDtypeStruct((bsz, n_pos, n_ch), BF16),
        compiler_params=_params(("parallel", "arbitrary")),
        name="pool_module",
    )(u, pool_w.astype(BF16), pool_scale.astype(F32).reshape(1, n_ch))


def _rope_tables(n_pos):
    inv = ROPE_THETA ** (-jnp.arange(0, HEAD_DIM, 2, dtype=F32) / HEAD_DIM)
    ang = jnp.arange(n_pos, dtype=F32)[:, None] * inv[None, :]
    ang = jnp.concatenate([ang, ang], axis=-1)
    return jnp.cos(ang), jnp.sin(ang)


def _lambda_init(layer_idx):
    return 0.8 - 0.6 * math.exp(-0.3 * layer_idx)


def kernel(x, router_w, router_bias, l0_in_proj, l0_conv_w, l0_conv_b, l0_conv_ln_g, l0_conv_ln_b, l0_lambda_q1, l0_lambda_k1, l0_lambda_q2, l0_lambda_k2, l0_subln_g, l0_out_proj, l0_ln_mix_g, l0_ln_mix_b, l0_w_gate, l0_w_up, l0_w_down, l0_ln_ffn_g, l0_ln_ffn_b, l1_in_proj, l1_pool_w, l1_pool_scale, l1_out_proj, l1_ln_mix_g, l1_ln_mix_b, l1_w_gate, l1_w_up, l1_w_down, l1_ln_ffn_g, l1_ln_ffn_b):
    bsz, n_pos, d = x.shape
    n_tok = bsz * n_pos
    x = x.astype(F32)

    conv_ch = l0_conv_w.shape[1]
    dv = l0_subln_g.shape[0]
    n_diff = (l0_out_proj.shape[0] - conv_ch) // dv
    qk_w = n_diff * 2 * HEAD_DIM
    v_w = n_diff * dv

    tm = min(512, n_pos)
    tq_diff = min(512, n_pos)
    tq_sb = min(512, n_pos)
    tk_sb = min(256, n_pos)
    ts = min(256, n_pos)
    tmc = min(256, n_pos)
    tm_out = min(1024, n_tok)

    cos, sin = _rope_tables(n_pos)
    sign = jnp.where(jnp.arange(HEAD_DIM) < HEAD_DIM // 2, -1.0, 1.0).astype(F32)
    cos128 = jnp.concatenate([cos, cos], axis=-1)
    sins128 = jnp.concatenate([sin * sign, sin * sign], axis=-1)
    cost = cos.T
    sint = sin.T

    glu, qt, k, vt = _inproj0(x, l0_in_proj.astype(BF16), cos128, sins128, cost, sint,
                              conv_ch=conv_ch, qk_w=qk_w, v_w=v_w, tm=tm)
    a = _conv_module(glu, l0_conv_w.astype(F32), l0_conv_b.astype(F32),
                     l0_conv_ln_g.astype(F32), l0_conv_ln_b.astype(F32), ts=ts)
    lam_init = _lambda_init(0)
    lam = (jnp.exp(jnp.sum(l0_lambda_q1.astype(F32) * l0_lambda_k1.astype(F32)))
           - jnp.exp(jnp.sum(l0_lambda_q2.astype(F32) * l0_lambda_k2.astype(F32))) + lam_init)
    o = _diff_attention(qt, k, vt, lam, l0_subln_g, n_heads=n_diff, lam_init=lam_init,
                        tq=tq_diff, tk=tq_diff)
    xf = x.reshape(n_tok, d)
    y, route, gates, counts = _outproj_ln_route(
        a.reshape(n_tok, conv_ch), o.reshape(n_tok, v_w), l0_out_proj, xf,
        l0_ln_mix_g.astype(F32), l0_ln_mix_b.astype(F32), router_w, router_bias, tm=tm_out)
    xf = _moe_sparse(y, route, gates, counts, l0_w_gate, l0_w_up, l0_w_down,
                     l0_ln_ffn_g.astype(F32), l0_ln_ffn_b.astype(F32), tmd=tm_out, tmc=tmc)

    sb_w = l1_in_proj.shape[1] - len(POOL_WINDOWS) * POOL_CH
    sb_w = sb_w // 3
    qt, k, vt, u = _inproj1(xf.reshape(bsz, n_pos, d), l1_in_proj.astype(BF16), sb_w=sb_w, tm=tm)
    o = _sb_attention(qt, k, vt, tq=tq_sb, tk=tk_sb)
    dpool = _pool_module(u, l1_pool_w, l1_pool_scale, ts=ts)
    y, route, gates, counts = _outproj_ln_route(
        o.reshape(n_tok, sb_w), dpool.reshape(n_tok, -1), l1_out_proj, xf,
        l1_ln_mix_g.astype(F32), l1_ln_mix_b.astype(F32), router_w, router_bias, tm=tm_out)
    xf = _moe_sparse(y, route, gates, counts, l1_w_gate, l1_w_up, l1_w_down,
                     l1_ln_ffn_g.astype(F32), l1_ln_ffn_b.astype(F32), tmd=tm_out, tmc=tmc)
    return xf.reshape(bsz, n_pos, d)
```

```python
import functools
import math

import jax
import jax.numpy as jnp
from jax import lax
from jax.experimental import pallas as pl
from jax.experimental.pallas import tpu as pltpu

F32 = jnp.float32
BF16 = jnp.bfloat16

DEPTH = 2
HEAD_DIM = 64
ROPE_THETA = 10000.0
LN_EPS = 1e-5
RMS_EPS = 1e-5
DEEPNORM_ALPHA = (2 * DEPTH) ** 0.25
CONV_WIDTH = 31
POOL_WINDOWS = (2, 4, 8, 16)
POOL_CH = 128
N_EXPERTS = 16
N_GROUPS = 4
EXPERTS_PER_GROUP = N_EXPERTS // N_GROUPS
ATTN_SCALE = HEAD_DIM ** -0.5
LOG2E = math.log2(math.e)

LANES = 128
SUBLANES = 8
MIB = 1024 * 1024
VMEM_LIMIT = 52 * MIB

NEG_BIG = -1e30
SB_SCAN_ROWS = 128
SB_UNDERFLOW = 110.0


def _params(semantics, vmem=VMEM_LIMIT):
    return pltpu.CompilerParams(dimension_semantics=semantics, vmem_limit_bytes=vmem)


def _layer_norm_rows(h, g, b):
    mu = jnp.mean(h, axis=-1, keepdims=True)
    d = h - mu
    var = jnp.mean(d * d, axis=-1, keepdims=True)
    return d * lax.rsqrt(var + LN_EPS) * g + b


def _load_token_tiles(ref, row0, n, d):
    k = d // LANES
    return jnp.concatenate([ref[pl.ds(row0 + c, n, stride=k), :] for c in range(k)], axis=1)


def _store_token_tiles(ref, row0, val):
    n, d = val.shape
    k = d // LANES
    for c in range(k):
        ref[pl.ds(row0 + c, n, stride=k), :] = val[:, c * LANES:(c + 1) * LANES]


def _inproj0_kernel(x_ref, w_ref, cos_ref, sins_ref, cost_ref, sint_ref,
                    glu_ref, qt_ref, k_ref, vt_ref, *, conv_ch, qk_w):
    x = x_ref[0].astype(BF16)
    c0, c1, c2, c3 = conv_ch, 2 * conv_ch, 2 * conv_ch + qk_w, 2 * conv_ch + 2 * qk_w

    a_val = jnp.dot(x, w_ref[:, 0:c0], preferred_element_type=F32)
    a_gate = jnp.dot(x, w_ref[:, c0:c1], preferred_element_type=F32)
    glu_ref[0] = a_val * jax.nn.sigmoid(a_gate)

    dq = jnp.dot(x, w_ref[:, c1:c2], preferred_element_type=F32)
    dqt = dq.T
    cost = cost_ref[...]
    sint = sint_ref[...]
    half = HEAD_DIM // 2
    pieces = []
    for blk in range(qk_w // HEAD_DIM):
        u = dqt[blk * HEAD_DIM:(blk + 1) * HEAD_DIM]
        rot = jnp.concatenate([-u[half:], u[:half]], axis=0)
        pieces.append((u * cost + rot * sint) * (ATTN_SCALE * LOG2E))
    qt_ref[0] = jnp.concatenate(pieces, axis=0).astype(BF16)

    dk = jnp.dot(x, w_ref[:, c2:c3], preferred_element_type=F32)
    cos = cos_ref[...]
    sins = sins_ref[...]
    lane = lax.broadcasted_iota(jnp.int32, cos.shape, 1)
    first = (lane & (HEAD_DIM - 1)) < half
    kparts = []
    for h in range(qk_w // LANES):
        u = dk[:, h * LANES:(h + 1) * LANES]
        rot = jnp.where(first, pltpu.roll(u, LANES - half, 1), pltpu.roll(u, half, 1))
        kparts.append(u * cos + rot * sins)
    k_ref[0] = jnp.concatenate(kparts, axis=1).astype(BF16)

    dv = jnp.dot(x, w_ref[:, c3:], preferred_element_type=F32)
    vt_ref[0] = dv.T.astype(BF16)


def _inproj0(x, w_bf, cos128, sins128, cost, sint, *, conv_ch, qk_w, v_w, tm):
    bsz, n_pos, d = x.shape
    n_in = w_bf.shape[1]
    kern = functools.partial(_inproj0_kernel, conv_ch=conv_ch, qk_w=qk_w)
    return pl.pallas_call(
        kern,
        grid=(bsz, n_pos // tm),
        in_specs=[
            pl.BlockSpec((1, tm, d), lambda b, i: (b, i, 0)),
            pl.BlockSpec((d, n_in), lambda b, i: (0, 0)),
            pl.BlockSpec((tm, LANES), lambda b, i: (i, 0)),
            pl.BlockSpec((tm, LANES), lambda b, i: (i, 0)),
            pl.BlockSpec((HEAD_DIM, tm), lambda b, i: (0, i)),
            pl.BlockSpec((HEAD_DIM, tm), lambda b, i: (0, i)),
        ],
        out_specs=[
            pl.BlockSpec((1, tm, conv_ch), lambda b, i: (b, i, 0)),
            pl.BlockSpec((1, qk_w, tm), lambda b, i: (b, 0, i)),
            pl.BlockSpec((1, tm, qk_w), lambda b, i: (b, i, 0)),
            pl.BlockSpec((1, v_w, tm), lambda b, i: (b, 0, i)),
        ],
        out_shape=[
            jax.ShapeDtypeStruct((bsz, n_pos, conv_ch), F32),
            jax.ShapeDtypeStruct((bsz, qk_w, n_pos), BF16),
            jax.ShapeDtypeStruct((bsz, n_pos, qk_w), BF16),
            jax.ShapeDtypeStruct((bsz, v_w, n_pos), BF16),
        ],
        compiler_params=_params(("parallel", "arbitrary")),
        name="inproj0",
    )(x, w_bf, cos128, sins128, cost, sint)


CONV_HALO = 32
CONV_ROW_BLOCK = 64


def _conv_kernel(a_ref, cw_ref, cb_ref, g_ref, b_ref, o_ref, win_ref, acc_ref, *, ts):
    i = pl.program_id(1)
    t0 = pl.multiple_of(i * ts, ts)
    h0 = pl.multiple_of(jnp.maximum(t0 - CONV_HALO, 0), CONV_HALO)
    n_ch = a_ref.shape[2]
    n_shift = ts + CONV_HALO - SUBLANES
    for c in range(n_ch // LANES):
        cs = slice(c * LANES, (c + 1) * LANES)
        cur = a_ref[0, pl.ds(t0, ts), cs]
        halo = a_ref[0, pl.ds(h0, CONV_HALO), cs]
        halo = jnp.where(i > 0, halo, 0.0)
        xw = jnp.concatenate([halo, cur], axis=0)
        win_ref[c, 0] = xw
        for r in range(1, SUBLANES):
            win_ref[c, r, 0:n_shift] = xw[r:r + n_shift]
        def row_block(rb, carry, c=c, cs=cs, full=xw.shape[0]):
            base = pl.multiple_of(rb * CONV_ROW_BLOCK, CONV_ROW_BLOCK)
            acc = jnp.zeros((CONV_ROW_BLOCK, LANES), F32) + cb_ref[:, cs]
            for r in range(SUBLANES):
                rows = full if r == 0 else n_shift
                for a in range(CONV_HALO // SUBLANES + 1):
                    w = SUBLANES * a + r - (CONV_HALO - CONV_WIDTH + 1)
                    if 0 <= w < CONV_WIDTH and SUBLANES * a + ts <= rows:
                        tap = win_ref[c, r, pl.ds(base + SUBLANES * a, CONV_ROW_BLOCK)]
                        acc = acc + tap * cw_ref[w:w + 1, cs]
            acc_ref[pl.ds(base, CONV_ROW_BLOCK), cs] = acc
            return carry
        lax.fori_loop(0, ts // CONV_ROW_BLOCK, row_block, 0)
    y = _layer_norm_rows(acc_ref[...], g_ref[...], b_ref[...])
    o_ref[0] = (y * jax.nn.sigmoid(y)).astype(o_ref.dtype)


def _conv_module(a, conv_w, conv_b, ln_g, ln_b, *, ts):
    bsz, n_pos, n_ch = a.shape
    return pl.pallas_call(
        functools.partial(_conv_kernel, ts=ts),
        grid=(bsz, n_pos // ts),
        in_specs=[
            pl.BlockSpec((1, n_pos, n_ch), lambda b, i: (b, 0, 0)),
            pl.BlockSpec((CONV_WIDTH, n_ch), lambda b, i: (0, 0)),
            pl.BlockSpec((1, n_ch), lambda b, i: (0, 0)),
            pl.BlockSpec((1, n_ch), lambda b, i: (0, 0)),
            pl.BlockSpec((1, n_ch), lambda b, i: (0, 0)),
        ],
        out_specs=pl.BlockSpec((1, ts, n_ch), lambda b, i: (b, i, 0)),
        out_shape=jax.ShapeDtypeStruct((bsz, n_pos, n_ch), BF16),
        scratch_shapes=[pltpu.VMEM((n_ch // LANES, SUBLANES, ts + CONV_HALO, LANES), F32),
                        pltpu.VMEM((ts, n_ch), F32)],
        compiler_params=_params(("parallel", "arbitrary")),
        name="conv_module",
    )(a, conv_w, conv_b.reshape(1, n_ch), ln_g.reshape(1, n_ch), ln_b.reshape(1, n_ch))


def _diff_attn_kernel(lam_ref, qt_ref, k_ref, vt_ref, g_ref, o_ref, sa_ref, sb_ref,
                      *, tq, tk, out_scale):
    i = pl.program_id(2)
    lam = lam_ref[0, 0]
    qt = qt_ref[0]
    zeros = jnp.zeros((HEAD_DIM, tq), BF16)
    qa = jnp.concatenate([qt[:HEAD_DIM], zeros], axis=0)
    qb = jnp.concatenate([zeros, qt[HEAD_DIM:]], axis=0)
    dv = vt_ref.shape[1]

    def online(s, vj, m, l, acc):
        m_new = jnp.maximum(m, jnp.max(s, axis=0, keepdims=True))
        alpha = jnp.exp2(m - m_new)
        p = jnp.exp2(s - m_new)
        l_new = alpha * l + jnp.sum(p, axis=0, keepdims=True)
        acc_new = alpha * acc + jnp.dot(vj, p.astype(BF16), preferred_element_type=F32)
        return m_new, l_new, acc_new

    def scores(j, buf):
        kj = k_ref[0, pl.ds(pl.multiple_of(j * tk, tk), tk), :]
        buf[0] = jnp.dot(kj, qa, preferred_element_type=F32)
        buf[1] = jnp.dot(kj, qb, preferred_element_type=F32)

    def consume(j, buf, carry, masked):
        m1, l1, acc1, m2, l2, acc2 = carry
        k0 = pl.multiple_of(j * tk, tk)
        vj = vt_ref[0, :, pl.ds(k0, tk)]
        s1, s2 = buf[0], buf[1]
        if masked:
            key = k0 + lax.broadcasted_iota(jnp.int32, (tk, tq), 0)
            qpos = i * tq + lax.broadcasted_iota(jnp.int32, (tk, tq), 1)
            keep = key <= qpos
            s1 = jnp.where(keep, s1, NEG_BIG)
            s2 = jnp.where(keep, s2, NEG_BIG)
        m1, l1, acc1 = online(s1, vj, m1, l1, acc1)
        m2, l2, acc2 = online(s2, vj, m2, l2, acc2)
        return m1, l1, acc1, m2, l2, acc2

    row = jnp.full((1, tq), NEG_BIG, F32)
    zrow = jnp.zeros((1, tq), F32)
    zacc = jnp.zeros((dv, tq), F32)
    carry = (row, zrow, zacc, row, zrow, zacc)

    scores(0, sa_ref)

    def pair(u, c):
        scores(2 * u + 1, sb_ref)
        c = consume(2 * u, sa_ref, c, False)
        scores(2 * u + 2, sa_ref)
        return consume(2 * u + 1, sb_ref, c, False)

    carry = lax.fori_loop(0, lax.shift_right_logical(i, 1), pair, carry)

    def odd_tail(c):
        scores(i, sb_ref)
        c = consume(i - 1, sa_ref, c, False)
        return consume(i, sb_ref, c, True)

    def even_tail(c):
        return consume(i, sa_ref, c, True)

    m1, l1, acc1, m2, l2, acc2 = lax.cond((i & 1) == 1, odd_tail, even_tail, carry)

    o = acc1 / l1 - lam * (acc2 / l2)
    o = o * lax.rsqrt(jnp.mean(o * o, axis=0, keepdims=True) + RMS_EPS)
    o = o * g_ref[...] * out_scale
    o_ref[0] = o.T.astype(o_ref.dtype)


def _diff_attention(qt, k, vt, lam, subln_g, *, n_heads, lam_init, tq, tk):
    bsz, _, n_pos = qt.shape
    dv = vt.shape[1] // n_heads
    kern = functools.partial(_diff_attn_kernel, tq=tq, tk=tk, out_scale=1.0 - lam_init)
    return pl.pallas_call(
        kern,
        grid=(bsz, n_heads, n_pos // tq),
        in_specs=[
            pl.BlockSpec(memory_space=pltpu.SMEM),
            pl.BlockSpec((1, 2 * HEAD_DIM, tq), lambda b, h, i: (b, h, i)),
            pl.BlockSpec((1, n_pos, 2 * HEAD_DIM), lambda b, h, i: (b, 0, h)),
            pl.BlockSpec((1, dv, n_pos), lambda b, h, i: (b, h, 0)),
            pl.BlockSpec((dv, 1), lambda b, h, i: (0, 0)),
        ],
        out_specs=pl.BlockSpec((1, tq, dv), lambda b, h, i: (b, i, h)),
        out_shape=jax.ShapeDtypeStruct((bsz, n_pos, n_heads * dv), BF16),
        scratch_shapes=[pltpu.VMEM((2, tk, tq), F32), pltpu.VMEM((2, tk, tq), F32)],
        compiler_params=_params(("parallel", "parallel", "arbitrary")),
        name="diff_attention",
    )(lam.reshape(1, 1), qt, k, vt, subln_g.astype(F32).reshape(dv, 1))


def _masked_top2(vals, member, row):
    big = N_EXPERTS
    v = jnp.where(member, vals, -jnp.inf)
    v1 = jnp.max(v, axis=0, keepdims=True)
    i1 = jnp.min(jnp.where(v == v1, row, big), axis=0, keepdims=True)
    v = jnp.where(row == i1, -jnp.inf, v)
    v2 = jnp.max(v, axis=0, keepdims=True)
    i2 = jnp.min(jnp.where(v == v2, row, big), axis=0, keepdims=True)
    return v1, i1, v2, i2


def _top2_route(aff, sel):
    row = lax.broadcasted_iota(jnp.int32, sel.shape, 0)
    group = lax.shift_right_logical(row, EXPERTS_PER_GROUP.bit_length() - 1)
    gbest = None
    for g in range(N_GROUPS):
        v1, _, v2, _ = _masked_top2(sel, group == g, row)
        score = v1 + v2
        if gbest is None:
            gbest, gval = jnp.zeros_like(score, dtype=jnp.int32), score
        else:
            better = score > gval
            gbest = jnp.where(better, g, gbest)
            gval = jnp.where(better, score, gval)
    _, e1, _, e2 = _masked_top2(sel, group == gbest, row)
    a1 = jnp.sum(jnp.where(row == e1, aff, 0.0), axis=0, keepdims=True)
    a2 = jnp.sum(jnp.where(row == e2, aff, 0.0), axis=0, keepdims=True)
    denom = a1 + a2
    return e1, e2, a1 / denom, a2 / denom


ROUTE_ROWS = SUBLANES


def _outproj_kernel(pa_ref, pb_ref, wa_ref, wb_ref, x_ref, g_ref, b_ref, rw_ref, rb_ref,
                    y_ref, route_ref, gate_ref, count_ref, seen_ref):
    i = pl.program_id(0)

    @pl.when(i == 0)
    def _():
        seen_ref[...] = jnp.zeros_like(seen_ref)

    mix = jnp.dot(pa_ref[...], wa_ref[...], preferred_element_type=F32)
    mix = mix + jnp.dot(pb_ref[...], wb_ref[...], preferred_element_type=F32)
    y = _layer_norm_rows(DEEPNORM_ALPHA * x_ref[...] + mix, g_ref[...], b_ref[...])
    _store_token_tiles(y_ref, 0, y)
    tm = y.shape[0]
    logits = lax.dot_general(rw_ref[...], y, (((1,), (1,)), ((), ())),
                             precision=lax.Precision.HIGHEST, preferred_element_type=F32)
    aff = jax.nn.sigmoid(logits)
    e1, e2, g1, g2 = _top2_route(aff, aff + rb_ref[...])

    erow = lax.broadcasted_iota(jnp.int32, (N_EXPERTS, tm), 0)
    hit = jnp.where((erow == e1) | (erow == e2), 1.0, 0.0)
    upto = (lax.broadcasted_iota(jnp.int32, (tm, tm), 0)
            <= lax.broadcasted_iota(jnp.int32, (tm, tm), 1)).astype(BF16)
    incl = jnp.dot(hit.astype(BF16), upto, preferred_element_type=F32)
    before = seen_ref[:, 0:1] + incl - hit
    r1 = jnp.sum(jnp.where(erow == e1, before, 0.0), axis=0, keepdims=True).astype(jnp.int32)
    r2 = jnp.sum(jnp.where(erow == e2, before, 0.0), axis=0, keepdims=True).astype(jnp.int32)
    seen_ref[...] = seen_ref[...] + jnp.sum(hit, axis=1, keepdims=True)
    count_ref[...] = seen_ref[...]

    rrow = lax.broadcasted_iota(jnp.int32, (ROUTE_ROWS, tm), 0)
    route_ref[0] = jnp.where(rrow == 0, e1, jnp.where(rrow == 1, e2,
                             jnp.where(rrow == 2, r1, jnp.where(rrow == 3, r2, 0))))
    grow = lax.broadcasted_iota(jnp.int32, (LANES, tm), 0)
    gate_ref[...] = jnp.where(grow == 0, g1, jnp.where(grow == 1, g2, 0.0)).T


def _outproj_ln_route(pa, pb, w_out, x, ln_g, ln_b, router_w, router_bias, *, tm):
    n_tok, d = x.shape
    ka, kb = pa.shape[1], pb.shape[1]
    wa = w_out[:ka].astype(BF16)
    wb = w_out[ka:].astype(BF16)
    rows = lambda i: (i, 0)
    fixed = lambda i: (0, 0)
    return pl.pallas_call(
        _outproj_kernel,
        grid=(n_tok // tm,),
        in_specs=[
            pl.BlockSpec((tm, ka), rows),
            pl.BlockSpec((tm, kb), rows),
            pl.BlockSpec((ka, d), fixed),
            pl.BlockSpec((kb, d), fixed),
            pl.BlockSpec((tm, d), rows),
            pl.BlockSpec((1, d), fixed),
            pl.BlockSpec((1, d), fixed),
            pl.BlockSpec((N_EXPERTS, d), fixed),
            pl.BlockSpec((N_EXPERTS, 1), fixed),
        ],
        out_specs=[
            pl.BlockSpec((tm * d // LANES, LANES), rows),
            pl.BlockSpec((1, ROUTE_ROWS, tm), lambda i: (i, 0, 0)),
            pl.BlockSpec((tm, LANES), rows),
            pl.BlockSpec((N_EXPERTS, LANES), fixed),
        ],
        out_shape=[
            jax.ShapeDtypeStruct((n_tok * d // LANES, LANES), F32),
            jax.ShapeDtypeStruct((n_tok // tm, ROUTE_ROWS, tm), jnp.int32),
            jax.ShapeDtypeStruct((n_tok, LANES), F32),
            jax.ShapeDtypeStruct((N_EXPERTS, LANES), F32),
        ],
        scratch_shapes=[pltpu.VMEM((N_EXPERTS, LANES), F32)],
        compiler_params=_params(("arbitrary",)),
        name="outproj_ln_route",
    )(pa, pb, wa, wb, x, ln_g.reshape(1, d), ln_b.reshape(1, d),
      router_w.astype(F32).T, router_bias.astype(F32).reshape(N_EXPERTS, 1))


MOE_ROW_TILE = 512
DMA_UNROLL = 8


def _row_copy_wait(src_hbm, dst, sem, n_rows):
    pltpu.make_async_copy(src_hbm.at[pl.ds(0, n_rows)], dst, sem).wait()


def _dispatch_kernel(offend_ref, pos_ref, y_ref, xs_hbm, zero_ref, sem, zsem, *, tmd, tr):
    i = pl.program_id(0)
    tile_rows = MOE_ROW_TILE * tr

    @pl.when(i == 0)
    def _():
        zero_ref[...] = jnp.zeros_like(zero_ref)
        def tail_copy(e):
            start = pl.multiple_of((offend_ref[e] - MOE_ROW_TILE) * tr, tile_rows)
            return pltpu.make_async_copy(zero_ref, xs_hbm.at[pl.ds(start, tile_rows)], zsem)
        def nonempty(e):
            return offend_ref[e] > (offend_ref[e - 1] if e > 0 else 0)
        for e in range(N_EXPERTS):
            @pl.when(nonempty(e))
            def _():
                tail_copy(e).start()
        for e in range(N_EXPERTS):
            @pl.when(nonempty(e))
            def _():
                tail_copy(e).wait()
        def unused_copy(j):
            return pltpu.make_async_copy(
                zero_ref, xs_hbm.at[pl.ds(pl.multiple_of(j * tile_rows, tile_rows), tile_rows)],
                zsem)
        first_unused = lax.shift_right_logical(offend_ref[N_EXPERTS - 1],
                                               MOE_ROW_TILE.bit_length() - 1)
        n_tiles = xs_hbm.shape[0] // tile_rows
        lax.fori_loop(first_unused, n_tiles, lambda j, c: (unused_copy(j).start(), c)[1], 0)
        lax.fori_loop(first_unused, n_tiles, lambda j, c: (unused_copy(j).wait(), c)[1], 0)

    def issue(g, c):
        r0 = g * DMA_UNROLL
        ps = [[pos_ref[0, 0, k * tmd + r0 + j] for k in range(2)] for j in range(DMA_UNROLL)]
        for j in range(DMA_UNROLL):
            src = y_ref.at[pl.ds(pl.multiple_of((r0 + j) * tr, tr), tr)]
            for k in range(2):
                dst = xs_hbm.at[pl.ds(pl.multiple_of(ps[j][k] * tr, tr), tr)]
                pltpu.make_async_copy(src, dst, sem).start(priority=k)
        return c

    lax.fori_loop(0, tmd // DMA_UNROLL, issue, 0)
    for k in range(2):
        pltpu.make_async_copy(y_ref, xs_hbm.at[pl.ds(0, tmd * tr)], sem).wait()


def _dispatch(y_tok, pos, off_end, *, n_sorted, tmd, tr):
    n_tok = y_tok.shape[0] // tr
    return pl.pallas_call(
        functools.partial(_dispatch_kernel, tmd=tmd, tr=tr),
        grid_spec=pltpu.PrefetchScalarGridSpec(
            num_scalar_prefetch=1,
            grid=(n_tok // tmd,),
            in_specs=[
                pl.BlockSpec((1, 1, 2 * tmd), lambda i, off: (i, 0, 0), memory_space=pltpu.SMEM),
                pl.BlockSpec((tmd * tr, LANES), lambda i, off: (i, 0)),
            ],
            out_specs=pl.BlockSpec(memory_space=pl.ANY),
            scratch_shapes=[pltpu.VMEM((MOE_ROW_TILE * tr, LANES), F32),
                            pltpu.SemaphoreType.DMA, pltpu.SemaphoreType.DMA],
        ),
        out_shape=jax.ShapeDtypeStruct((n_sorted * tr, LANES), F32),
        compiler_params=_params(("arbitrary",)),
        name="moe_dispatch",
    )(off_end, pos, y_tok)


def _experts_kernel(texp_ref, nused_ref, xs_ref, wg_ref, wu_ref, wd_ref, ys_ref,
                    wg_bf, wu_bf, wd_bf):
    i = pl.program_id(0)
    used = i < nused_ref[0]

    @pl.when(jnp.logical_not(used))
    def _():
        ys_ref[...] = jnp.zeros_like(ys_ref)

    new_expert = jnp.logical_or(i == 0, texp_ref[i] != texp_ref[jnp.maximum(i - 1, 0)])

    @pl.when(jnp.logical_and(used, new_expert))
    def _():
        wg_bf[...] = wg_ref[0].astype(BF16)
        wu_bf[...] = wu_ref[0].astype(BF16)
        wd_bf[...] = wd_ref[0].astype(BF16)

    @pl.when(used)
    def _():
        d = wg_ref.shape[1]
        x = _load_token_tiles(xs_ref, 0, MOE_ROW_TILE, d).astype(BF16)
        hg = jnp.dot(x, wg_bf[...], preferred_element_type=F32)
        hu = jnp.dot(x, wu_bf[...], preferred_element_type=F32)
        h = (hg * jax.nn.sigmoid(hg)) * hu
        _store_token_tiles(ys_ref, 0, jnp.dot(h.astype(BF16), wd_bf[...],
                                              preferred_element_type=F32))


def _experts(xs, tile_expert, n_used, w_gate, w_up, w_down):
    _, d, dff = w_gate.shape
    tile_rows = MOE_ROW_TILE * d // LANES
    tile = lambda i, te, nu: (jnp.minimum(i, nu[0] - 1), 0)
    every = lambda i, te, nu: (i, 0)
    wsel = lambda i, te, nu: (te[jnp.minimum(i, nu[0] - 1)], 0, 0)
    return pl.pallas_call(
        _experts_kernel,
        grid_spec=pltpu.PrefetchScalarGridSpec(
            num_scalar_prefetch=2,
            grid=(xs.shape[0] // tile_rows,),
            in_specs=[
                pl.BlockSpec((tile_rows, LANES), tile),
                pl.BlockSpec((1, d, dff), wsel),
                pl.BlockSpec((1, d, dff), wsel),
                pl.BlockSpec((1, dff, d), wsel),
            ],
            out_specs=pl.BlockSpec((tile_rows, LANES), every),
            scratch_shapes=[pltpu.VMEM((d, dff), BF16), pltpu.VMEM((d, dff), BF16),
                            pltpu.VMEM((dff, d), BF16)],
        ),
        out_shape=jax.ShapeDtypeStruct(xs.shape, F32),
        compiler_params=_params(("arbitrary",)),
        name="moe_experts",
    )(tile_expert, n_used, xs, w_gate, w_up, w_down)


def _combine_kernel(pos_ref, posn_ref, y_ref, gate_ref, g_ref, b_ref, ys_hbm, o_ref,
                    buf, sem, *, tmc, tr):
    i = pl.program_id(0)
    n = pl.num_programs(0)
    d = o_ref.shape[1]
    half = tmc * tr
    slot_rows = 2 * half

    def gather(p_ref, slot):
        base = slot * slot_rows
        def issue(g, c):
            r0 = g * DMA_UNROLL
            ps = [[p_ref[0, 0, k * tmc + r0 + j] for k in range(2)] for j in range(DMA_UNROLL)]
            for j in range(DMA_UNROLL):
                for k in range(2):
                    src = ys_hbm.at[pl.ds(pl.multiple_of(ps[j][k] * tr, tr), tr)]
                    dst = buf.at[pl.ds(pl.multiple_of(base + k * half + (r0 + j) * tr, tr), tr)]
                    pltpu.make_async_copy(src, dst, sem.at[slot]).start(priority=k)
            return c
        lax.fori_loop(0, tmc // DMA_UNROLL, issue, 0)

    @pl.when(i == 0)
    def _():
        gather(pos_ref, 0)

    @pl.when(i + 1 < n)
    def _():
        gather(posn_ref, (i + 1) % 2)

    slot = i % 2
    base = pl.multiple_of(slot * slot_rows, slot_rows)
    _row_copy_wait(ys_hbm, buf.at[pl.ds(base, slot_rows)], sem.at[slot], slot_rows)
    gate = gate_ref[...]
    ffn = (gate[:, 0:1] * _load_token_tiles(buf, base, tmc, d)
           + gate[:, 1:2] * _load_token_tiles(buf, base + half, tmc, d))
    y = _load_token_tiles(y_ref, 0, tmc, d)
    o_ref[...] = _layer_norm_rows(DEEPNORM_ALPHA * y + ffn, g_ref[...], b_ref[...])


def _combine(ys, pos, y_tok, gates, ln_g, ln_b, *, tmc, tr):
    d = tr * LANES
    n_tok = y_tok.shape[0] // tr
    n = n_tok // tmc
    rows = lambda i: (i, 0)
    fixed = lambda i: (0, 0)
    return pl.pallas_call(
        functools.partial(_combine_kernel, tmc=tmc, tr=tr),
        grid=(n,),
        in_specs=[
            pl.BlockSpec((1, 1, 2 * tmc), lambda i: (i, 0, 0), memory_space=pltpu.SMEM),
            pl.BlockSpec((1, 1, 2 * tmc), lambda i: (jnp.minimum(i + 1, n - 1), 0, 0),
                         memory_space=pltpu.SMEM),
            pl.BlockSpec((tmc * tr, LANES), rows),
            pl.BlockSpec((tmc, LANES), rows),
            pl.BlockSpec((1, d), fixed),
            pl.BlockSpec((1, d), fixed),
            pl.BlockSpec(memory_space=pl.ANY),
        ],
        out_specs=pl.BlockSpec((tmc, d), rows),
        out_shape=jax.ShapeDtypeStruct((n_tok, d), F32),
        scratch_shapes=[pltpu.VMEM((2 * 2 * tmc * tr, LANES), F32),
                        pltpu.SemaphoreType.DMA((2,))],
        compiler_params=_params(("arbitrary",)),
        name="moe_combine",
    )(pos, pos, y_tok, gates, ln_g.reshape(1, d), ln_b.reshape(1, d), ys)


def _tile_positions(route, off, t):
    e = route[:, 0:2, :]
    hit = e[..., None] == jnp.arange(N_EXPERTS, dtype=jnp.int32)
    pos = jnp.sum(jnp.where(hit, off, 0), axis=-1) + route[:, 2:4, :]
    pos = pos.transpose(1, 0, 2).reshape(2, -1, t)
    return pos.transpose(1, 0, 2).reshape(-1, 1, 2 * t)


def _moe_sparse(y_tok, route, gates, counts, w_gate, w_up, w_down, ln_g, ln_b, *, tmd, tmc):
    d = w_gate.shape[1]
    tr = d // LANES
    n_tok = y_tok.shape[0] // tr
    cnt = counts[:, 0].astype(jnp.int32)
    padded = (cnt + MOE_ROW_TILE - 1) // MOE_ROW_TILE * MOE_ROW_TILE
    off_end = jnp.cumsum(padded)
    off = off_end - padded
    n_sorted = 2 * n_tok + N_EXPERTS * MOE_ROW_TILE
    n_tiles = n_sorted // MOE_ROW_TILE
    n_used = (off_end[-1:] // MOE_ROW_TILE).astype(jnp.int32)
    tile_start = jnp.arange(n_tiles, dtype=jnp.int32) * MOE_ROW_TILE
    tile_expert = jnp.minimum(jnp.sum(tile_start[:, None] >= off_end[None, :], axis=1),
                              N_EXPERTS - 1).astype(jnp.int32)
    xs = _dispatch(y_tok, _tile_positions(route, off, tmd), off_end.astype(jnp.int32),
                   n_sorted=n_sorted, tmd=tmd, tr=tr)
    ys = _experts(xs, tile_expert, n_used, w_gate, w_up, w_down)
    return _combine(ys, _tile_positions(route, off, tmc), y_tok, gates, ln_g, ln_b,
                    tmc=tmc, tr=tr)


def _inproj1_kernel(x_ref, w_ref, qt_ref, k_ref, vt_ref, u_ref, *, sb_w):
    x = x_ref[0].astype(BF16)
    q = jnp.dot(x, w_ref[:, 0:sb_w], preferred_element_type=F32) * ATTN_SCALE
    qt_ref[0] = q.T.astype(BF16)
    k_ref[0] = jnp.dot(x, w_ref[:, sb_w:2 * sb_w], preferred_element_type=F32).astype(BF16)
    v = jnp.dot(x, w_ref[:, 2 * sb_w:3 * sb_w], preferred_element_type=F32)
    vt_ref[0] = v.T.astype(BF16)
    u_ref[0] = jnp.dot(x, w_ref[:, 3 * sb_w:], preferred_element_type=F32)


def _inproj1(x, w_bf, *, sb_w, tm):
    bsz, n_pos, d = x.shape
    n_in = w_bf.shape[1]
    pool_w = n_in - 3 * sb_w
    return pl.pallas_call(
        functools.partial(_inproj1_kernel, sb_w=sb_w),
        grid=(bsz, n_pos // tm),
        in_specs=[
            pl.BlockSpec((1, tm, d), lambda b, i: (b, i, 0)),
            pl.BlockSpec((d, n_in), lambda b, i: (0, 0)),
        ],
        out_specs=[
            pl.BlockSpec((1, sb_w, tm), lambda b, i: (b, 0, i)),
            pl.BlockSpec((1, tm, sb_w), lambda b, i: (b, i, 0)),
            pl.BlockSpec((1, sb_w, tm), lambda b, i: (b, 0, i)),
            pl.BlockSpec((1, tm, pool_w), lambda b, i: (b, i, 0)),
        ],
        out_shape=[
            jax.ShapeDtypeStruct((bsz, sb_w, n_pos), BF16),
            jax.ShapeDtypeStruct((bsz, n_pos, sb_w), BF16),
            jax.ShapeDtypeStruct((bsz, sb_w, n_pos), BF16),
            jax.ShapeDtypeStruct((bsz, n_pos, pool_w), F32),
        ],
        compiler_params=_params(("parallel", "arbitrary")),
        name="inproj1",
    )(x, w_bf)


def _sb_attn_kernel(qt_ref, k_ref, vt_ref, o_ref, za_ref, zb_ref, *, tq, tk):
    i = pl.program_id(2)
    qt = qt_ref[0]
    zeros = jnp.zeros((HEAD_DIM, tq), BF16)
    qpads = (jnp.concatenate([qt[:HEAD_DIM], zeros], axis=0),
             jnp.concatenate([zeros, qt[HEAD_DIM:]], axis=0))
    cb = SB_SCAN_ROWS
    upper2 = ((lax.broadcasted_iota(jnp.int32, (cb, 2 * cb), 1) & (cb - 1))
              > lax.broadcasted_iota(jnp.int32, (cb, 2 * cb), 0)).astype(BF16)

    heads = range(2)

    def scores(j, buf):
        kj = k_ref[0, pl.ds(pl.multiple_of(j * tk, tk), tk), :]
        for hh in heads:
            buf[hh] = jnp.dot(kj, qpads[hh], preferred_element_type=F32)

    def consume(j, buf, carry, masked):
        k0 = pl.multiple_of(j * tk, tk)
        if masked:
            key = k0 + lax.broadcasted_iota(jnp.int32, (tk, tq), 0)
            qpos = i * tq + lax.broadcasted_iota(jnp.int32, (tk, tq), 1)
            keep = key < qpos
        zs = [buf[hh] for hh in heads]
        sps = [jnp.maximum(z, 0.0) + jnp.log(1.0 + jnp.exp2(jnp.abs(z) * (-LOG2E))) for z in zs]
        if masked:
            sps = [jnp.where(keep, sp, 0.0) for sp in sps]
        his = [sp.astype(BF16) for sp in sps]
        los = [(sp - hi.astype(F32)).astype(BF16) for sp, hi in zip(sps, his)]
        laters = [carry[hh][1] for hh in heads]
        pieces = [[None] * (tk // cb) for _ in heads]
        for c in reversed(range(tk // cb)):
            rs = slice(c * cb, (c + 1) * cb)
            sufs = [jnp.dot(upper2, jnp.concatenate([his[hh][rs], los[hh][rs]], axis=0),
                            preferred_element_type=F32) for hh in heads]
            for hh in heads:
                spc = sps[hh][rs]
                pieces[hh][c] = jnp.exp2((zs[hh][rs] - (spc + sufs[hh] + laters[hh])) * LOG2E)
                laters[hh] = laters[hh] + sufs[hh][0:1] + spc[0:1]
        out = []
        for hh in heads:
            a = jnp.concatenate(pieces[hh], axis=0)
            if masked:
                a = jnp.where(keep, a, 0.0)
            vj = vt_ref[0, hh * HEAD_DIM:(hh + 1) * HEAD_DIM, pl.ds(k0, tk)]
            acc = carry[hh][0] + jnp.dot(vj, a.astype(BF16), preferred_element_type=F32)
            out.append((acc, laters[hh]))
        return tuple(out)

    init = (jnp.zeros((HEAD_DIM, tq), F32), jnp.zeros((1, tq), F32))
    carry = (init, init)
    n_diag = tq // tk
    last = (i + 1) * n_diag - 1
    n_old = i * n_diag
    bufs = (za_ref, zb_ref)
    scores(last, bufs[0])
    for d in range(n_diag):
        nxt = last - d - 1 if d + 1 < n_diag else jnp.maximum(n_old - 1, 0)
        scores(nxt, bufs[(d + 1) % 2])
        carry = consume(last - d, bufs[d % 2], carry, True)
    first_old = bufs[n_diag % 2]
    spare = bufs[(n_diag + 1) % 2]

    def reach(c):
        return jnp.min(jnp.minimum(c[0][1], c[1][1]))

    def live(state):
        t, r, _ = state
        return jnp.logical_and(t < n_old, r < SB_UNDERFLOW)

    def older(state):
        t, _, c = state
        scores(n_old - 1 - t, spare)
        c = consume(n_old - 1 - t, spare, c, False)
        return t + 1, reach(c), c

    r0 = reach(carry)
    carry = lax.cond(jnp.logical_and(n_old > 0, r0 < SB_UNDERFLOW),
                     lambda c: consume(n_old - 1, first_old, c, False), lambda c: c, carry)
    _, _, carry = lax.while_loop(live, older, (jnp.int32(1), reach(carry), carry))
    o_ref[0] = jnp.concatenate([carry[0][0], carry[1][0]], axis=0).T.astype(o_ref.dtype)


def _sb_attention(qt, k, vt, *, tq, tk):
    bsz, sb_w, n_pos = qt.shape
    return pl.pallas_call(
        functools.partial(_sb_attn_kernel, tq=tq, tk=tk),
        grid=(bsz, sb_w // LANES, n_pos // tq),
        in_specs=[
            pl.BlockSpec((1, LANES, tq), lambda b, h, i: (b, h, i)),
            pl.BlockSpec((1, n_pos, LANES), lambda b, h, i: (b, 0, h)),
            pl.BlockSpec((1, LANES, n_pos), lambda b, h, i: (b, h, 0)),
        ],
        out_specs=pl.BlockSpec((1, tq, LANES), lambda b, h, i: (b, i, h)),
        out_shape=jax.ShapeDtypeStruct((bsz, n_pos, sb_w), BF16),
        scratch_shapes=[pltpu.VMEM((2, tk, tq), F32), pltpu.VMEM((2, tk, tq), F32)],
        compiler_params=_params(("parallel", "parallel", "arbitrary")),
        name="sb_attention",
    )(qt, k, vt)


POOL_HALO = 32


def _pool_kernel(u_ref, pw_ref, ps_ref, o_ref, *, ts):
    i = pl.program_id(1)
    t0 = pl.multiple_of(i * ts, ts)
    h0 = pl.multiple_of(jnp.maximum(t0 - POOL_HALO, 0), POOL_HALO)
    tpos = t0 + lax.broadcasted_iota(jnp.int32, (ts, 1), 0)
    outs = []
    for g, win in enumerate(POOL_WINDOWS):
        cs = slice(g * POOL_CH, (g + 1) * POOL_CH)
        cur = u_ref[0, pl.ds(t0, ts), cs]
        halo = u_ref[0, pl.ds(h0, POOL_HALO), cs]
        halo = jnp.where(i > 0, halo, 0.0)
        run = jnp.concatenate([halo, cur], axis=0)
        span = 1
        while span < win:
            n = run.shape[0] - span
            run = run[span:span + n] + run[0:n]
            span *= 2
        off = POOL_HALO - (win - 1)
        wsum = run[off:off + ts]
        count = jnp.minimum(tpos + 1, win).astype(F32)
        pooled = wsum / count - cur
        mixed = jnp.dot(pooled.astype(BF16), pw_ref[g], preferred_element_type=F32)
        outs.append(mixed * ps_ref[:, cs])
    o_ref[0] = jnp.concatenate(outs, axis=1).astype(o_ref.dtype)


def _pool_module(u, pool_w, pool_scale, *, ts):
    bsz, n_pos, n_ch = u.shape
    n_g = len(POOL_WINDOWS)
    return pl.pallas_call(
        functools.partial(_pool_kernel, ts=ts),
        grid=(bsz, n_pos // ts),
        in_specs=[
            pl.BlockSpec((1, n_pos, n_ch), lambda b, i: (b, 0, 0)),
            pl.BlockSpec((n_g, POOL_CH, POOL_CH), lambda b, i: (0, 0, 0)),
            pl.BlockSpec((1, n_ch), lambda b, i: (0, 0)),
        ],
        out_specs=pl.BlockSpec((1, ts, n_ch), lambda b, i: (b, i, 0)),
        out_shape=jax.ShapeDtypeStruct((bsz, n_pos, n_ch), BF16),
        compiler_params=_params(("parallel", "arbitrary")),
        name="pool_module",
    )(u, pool_w.astype(BF16), pool_scale.astype(F32).reshape(1, n_ch))


def _rope_tables(n_pos):
    inv = ROPE_THETA ** (-jnp.arange(0, HEAD_DIM, 2, dtype=F32) / HEAD_DIM)
    ang = jnp.arange(n_pos, dtype=F32)[:, None] * inv[None, :]
    ang = jnp.concatenate([ang, ang], axis=-1)
    return jnp.cos(ang), jnp.sin(ang)


def _lambda_init(layer_idx):
    return 0.8 - 0.6 * math.exp(-0.3 * layer_idx)


def kernel(x, router_w, router_bias, l0_in_proj, l0_conv_w, l0_conv_b, l0_conv_ln_g, l0_conv_ln_b, l0_lambda_q1, l0_lambda_k1, l0_lambda_q2, l0_lambda_k2, l0_subln_g, l0_out_proj, l0_ln_mix_g, l0_ln_mix_b, l0_w_gate, l0_w_up, l0_w_down, l0_ln_ffn_g, l0_ln_ffn_b, l1_in_proj, l1_pool_w, l1_pool_scale, l1_out_proj, l1_ln_mix_g, l1_ln_mix_b, l1_w_gate, l1_w_up, l1_w_down, l1_ln_ffn_g, l1_ln_ffn_b):
    bsz, n_pos, d = x.shape
    n_tok = bsz * n_pos
    x = x.astype(F32)

    conv_ch = l0_conv_w.shape[1]
    dv = l0_subln_g.shape[0]
    n_diff = (l0_out_proj.shape[0] - conv_ch) // dv
    qk_w = n_diff * 2 * HEAD_DIM
    v_w = n_diff * dv

    tm = min(512, n_pos)
    tq_diff = min(512, n_pos)
    tq_sb = min(512, n_pos)
    tk_sb = min(256, n_pos)
    ts = min(256, n_pos)
    tmc = min(256, n_pos)
    tm_out = min(1024, n_tok)

    cos, sin = _rope_tables(n_pos)
    sign = jnp.where(jnp.arange(HEAD_DIM) < HEAD_DIM // 2, -1.0, 1.0).astype(F32)
    cos128 = jnp.concatenate([cos, cos], axis=-1)
    sins128 = jnp.concatenate([sin * sign, sin * sign], axis=-1)
    cost = cos.T
    sint = sin.T

    glu, qt, k, vt = _inproj0(x, l0_in_proj.astype(BF16), cos128, sins128, cost, sint,
                              conv_ch=conv_ch, qk_w=qk_w, v_w=v_w, tm=tm)
    a = _conv_module(glu, l0_conv_w.astype(F32), l0_conv_b.astype(F32),
                     l0_conv_ln_g.astype(F32), l0_conv_ln_b.astype(F32), ts=ts)
    lam_init = _lambda_init(0)
    lam = (jnp.exp(jnp.sum(l0_lambda_q1.astype(F32) * l0_lambda_k1.astype(F32)))
           - jnp.exp(jnp.sum(l0_lambda_q2.astype(F32) * l0_lambda_k2.astype(F32))) + lam_init)
    o = _diff_attention(qt, k, vt, lam, l0_subln_g, n_heads=n_diff, lam_init=lam_init,
                        tq=tq_diff, tk=tq_diff)
    xf = x.reshape(n_tok, d)
    y, route, gates, counts = _outproj_ln_route(
        a.reshape(n_tok, conv_ch), o.reshape(n_tok, v_w), l0_out_proj, xf,
        l0_ln_mix_g.astype(F32), l0_ln_mix_b.astype(F32), router_w, router_bias, tm=tm_out)
    xf = _moe_sparse(y, route, gates, counts, l0_w_gate, l0_w_up, l0_w_down,
                     l0_ln_ffn_g.astype(F32), l0_ln_ffn_b.astype(F32), tmd=tm_out, tmc=tmc)

    sb_w = l1_in_proj.shape[1] - len(POOL_WINDOWS) * POOL_CH
    sb_w = sb_w // 3
    qt, k, vt, u = _inproj1(xf.reshape(bsz, n_pos, d), l1_in_proj.astype(BF16), sb_w=sb_w, tm=tm)
    o = _sb_attention(qt, k, vt, tq=tq_sb, tk=tk_sb)
    dpool = _pool_module(u, l1_pool_w, l1_pool_scale, ts=ts)
    y, route, gates, counts = _outproj_ln_route(
        o.reshape(n_tok, sb_w), dpool.reshape(n_tok, -1), l1_out_proj, xf,
        l1_ln_mix_g.astype(F32), l1_ln_mix_b.astype(F32), router_w, router_bias, tm=tm_out)
    xf = _moe_sparse(y, route, gates, counts, l1_w_gate, l1_w_up, l1_w_down,
                     l1_ln_ffn_g.astype(F32), l1_ln_ffn_b.astype(F32), tmd=tm_out, tmc=tmc)
    return xf.reshape(bsz, n_pos, d)
```

```python
import functools
import math

import jax
import jax.numpy as jnp
from jax import lax
from jax.experimental import pallas as pl
from jax.experimental.pallas import tpu as pltpu

F32 = jnp.float32
BF16 = jnp.bfloat16

DEPTH = 2
HEAD_DIM = 64
ROPE_THETA = 10000.0
LN_EPS = 1e-5
RMS_EPS = 1e-5
DEEPNORM_ALPHA = (2 * DEPTH) ** 0.25
CONV_WIDTH = 31
POOL_WINDOWS = (2, 4, 8, 16)
POOL_CH = 128
N_EXPERTS = 16
N_GROUPS = 4
EXPERTS_PER_GROUP = N_EXPERTS // N_GROUPS
ATTN_SCALE = HEAD_DIM ** -0.5
LOG2E = math.log2(math.e)

LANES = 128
SUBLANES = 8
MIB = 1024 * 1024
VMEM_LIMIT = 52 * MIB

NEG_BIG = -1e30
SB_SCAN_ROWS = 128
SB_UNDERFLOW = 110.0


def _params(semantics, vmem=VMEM_LIMIT):
    return pltpu.CompilerParams(dimension_semantics=semantics, vmem_limit_bytes=vmem)


def _layer_norm_rows(h, g, b):
    mu = jnp.mean(h, axis=-1, keepdims=True)
    d = h - mu
    var = jnp.mean(d * d, axis=-1, keepdims=True)
    return d * lax.rsqrt(var + LN_EPS) * g + b


def _load_token_tiles(ref, row0, n, d):
    k = d // LANES
    return jnp.concatenate([ref[pl.ds(row0 + c, n, stride=k), :] for c in range(k)], axis=1)


def _store_token_tiles(ref, row0, val):
    n, d = val.shape
    k = d // LANES
    for c in range(k):
        ref[pl.ds(row0 + c, n, stride=k), :] = val[:, c * LANES:(c + 1) * LANES]


def _inproj0_kernel(x_ref, w_ref, cos_ref, sins_ref, cost_ref, sint_ref,
                    glu_ref, qt_ref, k_ref, vt_ref, *, conv_ch, qk_w):
    x = x_ref[0].astype(BF16)
    c0, c1, c2, c3 = conv_ch, 2 * conv_ch, 2 * conv_ch + qk_w, 2 * conv_ch + 2 * qk_w

    a_val = jnp.dot(x, w_ref[:, 0:c0], preferred_element_type=F32)
    a_gate = jnp.dot(x, w_ref[:, c0:c1], preferred_element_type=F32)
    glu_ref[0] = a_val * jax.nn.sigmoid(a_gate)

    dq = jnp.dot(x, w_ref[:, c1:c2], preferred_element_type=F32)
    dqt = dq.T
    cost = cost_ref[...]
    sint = sint_ref[...]
    half = HEAD_DIM // 2
    pieces = []
    for blk in range(qk_w // HEAD_DIM):
        u = dqt[blk * HEAD_DIM:(blk + 1) * HEAD_DIM]
        rot = jnp.concatenate([-u[half:], u[:half]], axis=0)
        pieces.append((u * cost + rot * sint) * (ATTN_SCALE * LOG2E))
    qt_ref[0] = jnp.concatenate(pieces, axis=0).astype(BF16)

    dk = jnp.dot(x, w_ref[:, c2:c3], preferred_element_type=F32)
    cos = cos_ref[...]
    sins = sins_ref[...]
    lane = lax.broadcasted_iota(jnp.int32, cos.shape, 1)
    first = (lane & (HEAD_DIM - 1)) < half
    kparts = []
    for h in range(qk_w // LANES):
        u = dk[:, h * LANES:(h + 1) * LANES]
        rot = jnp.where(first, pltpu.roll(u, LANES - half, 1), pltpu.roll(u, half, 1))
        kparts.append(u * cos + rot * sins)
    k_ref[0] = jnp.concatenate(kparts, axis=1).astype(BF16)

    dv = jnp.dot(x, w_ref[:, c3:], preferred_element_type=F32)
    vt_ref[0] = dv.T.astype(BF16)


def _inproj0(x, w_bf, cos128, sins128, cost, sint, *, conv_ch, qk_w, v_w, tm):
    bsz, n_pos, d = x.shape
    n_in = w_bf.shape[1]
    kern = functools.partial(_inproj0_kernel, conv_ch=conv_ch, qk_w=qk_w)
    return pl.pallas_call(
        kern,
        grid=(bsz, n_pos // tm),
        in_specs=[
            pl.BlockSpec((1, tm, d), lambda b, i: (b, i, 0)),
            pl.BlockSpec((d, n_in), lambda b, i: (0, 0)),
            pl.BlockSpec((tm, LANES), lambda b, i: (i, 0)),
            pl.BlockSpec((tm, LANES), lambda b, i: (i, 0)),
            pl.BlockSpec((HEAD_DIM, tm), lambda b, i: (0, i)),
            pl.BlockSpec((HEAD_DIM, tm), lambda b, i: (0, i)),
        ],
        out_specs=[
            pl.BlockSpec((1, tm, conv_ch), lambda b, i: (b, i, 0)),
            pl.BlockSpec((1, qk_w, tm), lambda b, i: (b, 0, i)),
            pl.BlockSpec((1, tm, qk_w), lambda b, i: (b, i, 0)),
            pl.BlockSpec((1, v_w, tm), lambda b, i: (b, 0, i)),
        ],
        out_shape=[
            jax.ShapeDtypeStruct((bsz, n_pos, conv_ch), F32),
            jax.ShapeDtypeStruct((bsz, qk_w, n_pos), BF16),
            jax.ShapeDtypeStruct((bsz, n_pos, qk_w), BF16),
            jax.ShapeDtypeStruct((bsz, v_w, n_pos), BF16),
        ],
        compiler_params=_params(("parallel", "arbitrary")),
        name="inproj0",
    )(x, w_bf, cos128, sins128, cost, sint)


CONV_HALO = 32
CONV_ROW_BLOCK = 64


def _conv_kernel(a_ref, cw_ref, cb_ref, g_ref, b_ref, o_ref, win_ref, acc_ref, *, ts):
    i = pl.program_id(1)
    t0 = pl.multiple_of(i * ts, ts)
    h0 = pl.multiple_of(jnp.maximum(t0 - CONV_HALO, 0), CONV_HALO)
    n_ch = a_ref.shape[2]
    n_shift = ts + CONV_HALO - SUBLANES
    for c in range(n_ch // LANES):
        cs = slice(c * LANES, (c + 1) * LANES)
        cur = a_ref[0, pl.ds(t0, ts), cs]
        halo = a_ref[0, pl.ds(h0, CONV_HALO), cs]
        halo = jnp.where(i > 0, halo, 0.0)
        xw = jnp.concatenate([halo, cur], axis=0)
        win_ref[c, 0] = xw
        for r in range(1, SUBLANES):
            win_ref[c, r, 0:n_shift] = xw[r:r + n_shift]
        def row_block(rb, carry, c=c, cs=cs, full=xw.shape[0]):
            base = pl.multiple_of(rb * CONV_ROW_BLOCK, CONV_ROW_BLOCK)
            acc = jnp.zeros((CONV_ROW_BLOCK, LANES), F32) + cb_ref[:, cs]
            for r in range(SUBLANES):
                rows = full if r == 0 else n_shift
                for a in range(CONV_HALO // SUBLANES + 1):
                    w = SUBLANES * a + r - (CONV_HALO - CONV_WIDTH + 1)
                    if 0 <= w < CONV_WIDTH and SUBLANES * a + ts <= rows:
                        tap = win_ref[c, r, pl.ds(base + SUBLANES * a, CONV_ROW_BLOCK)]
                        acc = acc + tap * cw_ref[w:w + 1, cs]
            acc_ref[pl.ds(base, CONV_ROW_BLOCK), cs] = acc
            return carry
        lax.fori_loop(0, ts // CONV_ROW_BLOCK, row_block, 0)
    y = _layer_norm_rows(acc_ref[...], g_ref[...], b_ref[...])
    o_ref[0] = (y * jax.nn.sigmoid(y)).astype(o_ref.dtype)


def _conv_module(a, conv_w, conv_b, ln_g, ln_b, *, ts):
    bsz, n_pos, n_ch = a.shape
    return pl.pallas_call(
        functools.partial(_conv_kernel, ts=ts),
        grid=(bsz, n_pos // ts),
        in_specs=[
            pl.BlockSpec((1, n_pos, n_ch), lambda b, i: (b, 0, 0)),
            pl.BlockSpec((CONV_WIDTH, n_ch), lambda b, i: (0, 0)),
            pl.BlockSpec((1, n_ch), lambda b, i: (0, 0)),
            pl.BlockSpec((1, n_ch), lambda b, i: (0, 0)),
            pl.BlockSpec((1, n_ch), lambda b, i: (0, 0)),
        ],
        out_specs=pl.BlockSpec((1, ts, n_ch), lambda b, i: (b, i, 0)),
        out_shape=jax.ShapeDtypeStruct((bsz, n_pos, n_ch), BF16),
        scratch_shapes=[pltpu.VMEM((n_ch // LANES, SUBLANES, ts + CONV_HALO, LANES), F32),
                        pltpu.VMEM((ts, n_ch), F32)],
        compiler_params=_params(("parallel", "arbitrary")),
        name="conv_module",
    )(a, conv_w, conv_b.reshape(1, n_ch), ln_g.reshape(1, n_ch), ln_b.reshape(1, n_ch))


def _diff_attn_kernel(lam_ref, qt_ref, k_ref, vt_ref, g_ref, o_ref, sa_ref, sb_ref,
                      *, tq, tk, out_scale):
    i = pl.program_id(2)
    lam = lam_ref[0, 0]
    qt = qt_ref[0]
    zeros = jnp.zeros((HEAD_DIM, tq), BF16)
    qa = jnp.concatenate([qt[:HEAD_DIM], zeros], axis=0)
    qb = jnp.concatenate([zeros, qt[HEAD_DIM:]], axis=0)
    dv = vt_ref.shape[1]

    def online(s, vj, m, l, acc):
        m_new = jnp.maximum(m, jnp.max(s, axis=0, keepdims=True))
        alpha = jnp.exp2(m - m_new)
        p = jnp.exp2(s - m_new)
        l_new = alpha * l + jnp.sum(p, axis=0, keepdims=True)
        acc_new = alpha * acc + jnp.dot(vj, p.astype(BF16), preferred_element_type=F32)
        return m_new, l_new, acc_new

    def scores(j, buf):
        kj = k_ref[0, pl.ds(pl.multiple_of(j * tk, tk), tk), :]
        buf[0] = jnp.dot(kj, qa, preferred_element_type=F32)
        buf[1] = jnp.dot(kj, qb, preferred_element_type=F32)

    def consume(j, buf, carry, masked):
        m1, l1, acc1, m2, l2, acc2 = carry
        k0 = pl.multiple_of(j * tk, tk)
        vj = vt_ref[0, :, pl.ds(k0, tk)]
        s1, s2 = buf[0], buf[1]
        if masked:
            key = k0 + lax.broadcasted_iota(jnp.int32, (tk, tq), 0)
            qpos = i * tq + lax.broadcasted_iota(jnp.int32, (tk, tq), 1)
            keep = key <= qpos
            s1 = jnp.where(keep, s1, NEG_BIG)
            s2 = jnp.where(keep, s2, NEG_BIG)
        m1, l1, acc1 = online(s1, vj, m1, l1, acc1)
        m2, l2, acc2 = online(s2, vj, m2, l2, acc2)
        return m1, l1, acc1, m2, l2, acc2

    row = jnp.full((1, tq), NEG_BIG, F32)
    zrow = jnp.zeros((1, tq), F32)
    zacc = jnp.zeros((dv, tq), F32)
    carry = (row, zrow, zacc, row, zrow, zacc)

    scores(0, sa_ref)

    def pair(u, c):
        scores(2 * u + 1, sb_ref)
        c = consume(2 * u, sa_ref, c, False)
        scores(2 * u + 2, sa_ref)
        return consume(2 * u + 1, sb_ref, c, False)

    carry = lax.fori_loop(0, lax.shift_right_logical(i, 1), pair, carry)

    def odd_tail(c):
        scores(i, sb_ref)
        c = consume(i - 1, sa_ref, c, False)
        return consume(i, sb_ref, c, True)

    def even_tail(c):
        return consume(i, sa_ref, c, True)

    m1, l1, acc1, m2, l2, acc2 = lax.cond((i & 1) == 1, odd_tail, even_tail, carry)

    o = acc1 / l1 - lam * (acc2 / l2)
    o = o * lax.rsqrt(jnp.mean(o * o, axis=0, keepdims=True) + RMS_EPS)
    o = o * g_ref[...] * out_scale
    o_ref[0] = o.T.astype(o_ref.dtype)


def _diff_attention(qt, k, vt, lam, subln_g, *, n_heads, lam_init, tq, tk):
    bsz, _, n_pos = qt.shape
    dv = vt.shape[1] // n_heads
    kern = functools.partial(_diff_attn_kernel, tq=tq, tk=tk, out_scale=1.0 - lam_init)
    return pl.pallas_call(
        kern,
        grid=(bsz, n_heads, n_pos // tq),
        in_specs=[
            pl.BlockSpec(memory_space=pltpu.SMEM),
            pl.BlockSpec((1, 2 * HEAD_DIM, tq), lambda b, h, i: (b, h, i)),
            pl.BlockSpec((1, n_pos, 2 * HEAD_DIM), lambda b, h, i: (b, 0, h)),
            pl.BlockSpec((1, dv, n_pos), lambda b, h, i: (b, h, 0)),
            pl.BlockSpec((dv, 1), lambda b, h, i: (0, 0)),
        ],
        out_specs=pl.BlockSpec((1, tq, dv), lambda b, h, i: (b, i, h)),
        out_shape=jax.ShapeDtypeStruct((bsz, n_pos, n_heads * dv), BF16),
        scratch_shapes=[pltpu.VMEM((2, tk, tq), F32), pltpu.VMEM((2, tk, tq), F32)],
        compiler_params=_params(("parallel", "parallel", "arbitrary")),
        name="diff_attention",
    )(lam.reshape(1, 1), qt, k, vt, subln_g.astype(F32).reshape(dv, 1))


def _masked_top2(vals, member, row):
    big = N_EXPERTS
    v = jnp.where(member, vals, -jnp.inf)
    v1 = jnp.max(v, axis=0, keepdims=True)
    i1 = jnp.min(jnp.where(v == v1, row, big), axis=0, keepdims=True)
    v = jnp.where(row == i1, -jnp.inf, v)
    v2 = jnp.max(v, axis=0, keepdims=True)
    i2 = jnp.min(jnp.where(v == v2, row, big), axis=0, keepdims=True)
    return v1, i1, v2, i2


def _top2_route(aff, sel):
    row = lax.broadcasted_iota(jnp.int32, sel.shape, 0)
    group = lax.shift_right_logical(row, EXPERTS_PER_GROUP.bit_length() - 1)
    gbest = None
    for g in range(N_GROUPS):
        v1, _, v2, _ = _masked_top2(sel, group == g, row)
        score = v1 + v2
        if gbest is None:
            gbest, gval = jnp.zeros_like(score, dtype=jnp.int32), score
        else:
            better = score > gval
            gbest = jnp.where(better, g, gbest)
            gval = jnp.where(better, score, gval)
    _, e1, _, e2 = _masked_top2(sel, group == gbest, row)
    a1 = jnp.sum(jnp.where(row == e1, aff, 0.0), axis=0, keepdims=True)
    a2 = jnp.sum(jnp.where(row == e2, aff, 0.0), axis=0, keepdims=True)
    denom = a1 + a2
    return e1, e2, a1 / denom, a2 / denom


ROUTE_ROWS = SUBLANES


def _outproj_kernel(pa_ref, pb_ref, wa_ref, wb_ref, x_ref, g_ref, b_ref, rw_ref, rb_ref,
                    y_ref, route_ref, gate_ref, count_ref, seen_ref):
    i = pl.program_id(0)

    @pl.when(i == 0)
    def _():
        seen_ref[...] = jnp.zeros_like(seen_ref)

    mix = jnp.dot(pa_ref[...], wa_ref[...], preferred_element_type=F32)
    mix = mix + jnp.dot(pb_ref[...], wb_ref[...], preferred_element_type=F32)
    y = _layer_norm_rows(DEEPNORM_ALPHA * x_ref[...] + mix, g_ref[...], b_ref[...])
    _store_token_tiles(y_ref, 0, y)
    tm = y.shape[0]
    yh = y.astype(BF16)
    yl = (y - yh.astype(F32)).astype(BF16)
    nt = (((1,), (1,)), ((), ()))
    both = lax.dot_general(rw_ref[...], yh, nt, preferred_element_type=F32)
    logits = (both[:N_EXPERTS] + both[N_EXPERTS:]
              + lax.dot_general(rw_ref[0:N_EXPERTS], yl, nt, preferred_element_type=F32))
    aff = jax.nn.sigmoid(logits)
    e1, e2, g1, g2 = _top2_route(aff, aff + rb_ref[...])

    erow = lax.broadcasted_iota(jnp.int32, (N_EXPERTS, tm), 0)
    hit = jnp.where((erow == e1) | (erow == e2), 1.0, 0.0)
    upto = (lax.broadcasted_iota(jnp.int32, (tm, tm), 0)
            <= lax.broadcasted_iota(jnp.int32, (tm, tm), 1)).astype(BF16)
    incl = jnp.dot(hit.astype(BF16), upto, preferred_element_type=F32)
    before = seen_ref[:, 0:1] + incl - hit
    r1 = jnp.sum(jnp.where(erow == e1, before, 0.0), axis=0, keepdims=True).astype(jnp.int32)
    r2 = jnp.sum(jnp.where(erow == e2, before, 0.0), axis=0, keepdims=True).astype(jnp.int32)
    seen_ref[...] = seen_ref[...] + jnp.sum(hit, axis=1, keepdims=True)
    count_ref[...] = seen_ref[...]

    rrow = lax.broadcasted_iota(jnp.int32, (ROUTE_ROWS, tm), 0)
    route_ref[0] = jnp.where(rrow == 0, e1, jnp.where(rrow == 1, e2,
                             jnp.where(rrow == 2, r1, jnp.where(rrow == 3, r2, 0))))
    grow = lax.broadcasted_iota(jnp.int32, (LANES, tm), 0)
    gate_ref[...] = jnp.where(grow == 0, g1, jnp.where(grow == 1, g2, 0.0)).T


def _outproj_ln_route(pa, pb, w_out, x, ln_g, ln_b, router_w, router_bias, *, tm):
    n_tok, d = x.shape
    ka, kb = pa.shape[1], pb.shape[1]
    wa = w_out[:ka].astype(BF16)
    wb = w_out[ka:].astype(BF16)
    rw = router_w.astype(F32).T
    rw_hi = rw.astype(BF16)
    rw_lo = (rw - rw_hi.astype(F32)).astype(BF16)
    rows = lambda i: (i, 0)
    fixed = lambda i: (0, 0)
    return pl.pallas_call(
        _outproj_kernel,
        grid=(n_tok // tm,),
        in_specs=[
            pl.BlockSpec((tm, ka), rows),
            pl.BlockSpec((tm, kb), rows),
            pl.BlockSpec((ka, d), fixed),
            pl.BlockSpec((kb, d), fixed),
            pl.BlockSpec((tm, d), rows),
            pl.BlockSpec((1, d), fixed),
            pl.BlockSpec((1, d), fixed),
            pl.BlockSpec((2 * N_EXPERTS, d), fixed),
            pl.BlockSpec((N_EXPERTS, 1), fixed),
        ],
        out_specs=[
            pl.BlockSpec((tm * d // LANES, LANES), rows),
            pl.BlockSpec((1, ROUTE_ROWS, tm), lambda i: (i, 0, 0)),
            pl.BlockSpec((tm, LANES), rows),
            pl.BlockSpec((N_EXPERTS, LANES), fixed),
        ],
        out_shape=[
            jax.ShapeDtypeStruct((n_tok * d // LANES, LANES), F32),
            jax.ShapeDtypeStruct((n_tok // tm, ROUTE_ROWS, tm), jnp.int32),
            jax.ShapeDtypeStruct((n_tok, LANES), F32),
            jax.ShapeDtypeStruct((N_EXPERTS, LANES), F32),
        ],
        scratch_shapes=[pltpu.VMEM((N_EXPERTS, LANES), F32)],
        compiler_params=_params(("arbitrary",)),
        name="outproj_ln_route",
    )(pa, pb, wa, wb, x, ln_g.reshape(1, d), ln_b.reshape(1, d),
      jnp.concatenate([rw_hi, rw_lo], axis=0), router_bias.astype(F32).reshape(N_EXPERTS, 1))


MOE_ROW_TILE = 512
DMA_UNROLL = 8


def _row_copy_wait(src_hbm, dst, sem, n_rows):
    pltpu.make_async_copy(src_hbm.at[pl.ds(0, n_rows)], dst, sem).wait()


def _dispatch_kernel(offend_ref, pos_ref, y_ref, xs_hbm, zero_ref, sem, zsem, *, tmd, tr):
    i = pl.program_id(0)
    tile_rows = MOE_ROW_TILE * tr

    @pl.when(i == 0)
    def _():
        zero_ref[...] = jnp.zeros_like(zero_ref)
        def tail_copy(e):
            start = pl.multiple_of((offend_ref[e] - MOE_ROW_TILE) * tr, tile_rows)
            return pltpu.make_async_copy(zero_ref, xs_hbm.at[pl.ds(start, tile_rows)], zsem)
        def nonempty(e):
            return offend_ref[e] > (offend_ref[e - 1] if e > 0 else 0)
        for e in range(N_EXPERTS):
            @pl.when(nonempty(e))
            def _():
                tail_copy(e).start()
        for e in range(N_EXPERTS):
            @pl.when(nonempty(e))
            def _():
                tail_copy(e).wait()
        def unused_copy(j):
            return pltpu.make_async_copy(
                zero_ref, xs_hbm.at[pl.ds(pl.multiple_of(j * tile_rows, tile_rows), tile_rows)],
                zsem)
        first_unused = lax.shift_right_logical(offend_ref[N_EXPERTS - 1],
                                               MOE_ROW_TILE.bit_length() - 1)
        n_tiles = xs_hbm.shape[0] // tile_rows
        lax.fori_loop(first_unused, n_tiles, lambda j, c: (unused_copy(j).start(), c)[1], 0)
        lax.fori_loop(first_unused, n_tiles, lambda j, c: (unused_copy(j).wait(), c)[1], 0)

    def issue(g, c):
        r0 = g * DMA_UNROLL
        ps = [[pos_ref[0, 0, k * tmd + r0 + j] for k in range(2)] for j in range(DMA_UNROLL)]
        for j in range(DMA_UNROLL):
            src = y_ref.at[pl.ds(pl.multiple_of((r0 + j) * tr, tr), tr)]
            for k in range(2):
                dst = xs_hbm.at[pl.ds(pl.multiple_of(ps[j][k], tr), tr)]
                pltpu.make_async_copy(src, dst, sem).start(priority=k)
        return c

    lax.fori_loop(0, tmd // DMA_UNROLL, issue, 0)
    for k in range(2):
        pltpu.make_async_copy(y_ref, xs_hbm.at[pl.ds(0, tmd * tr)], sem).wait()


def _dispatch(y_tok, pos, off_end, *, n_sorted, tmd, tr):
    n_tok = y_tok.shape[0] // tr
    return pl.pallas_call(
        functools.partial(_dispatch_kernel, tmd=tmd, tr=tr),
        grid_spec=pltpu.PrefetchScalarGridSpec(
            num_scalar_prefetch=1,
            grid=(n_tok // tmd,),
            in_specs=[
                pl.BlockSpec((1, 1, 2 * tmd), lambda i, off: (i, 0, 0), memory_space=pltpu.SMEM),
                pl.BlockSpec((tmd * tr, LANES), lambda i, off: (i, 0)),
            ],
            out_specs=pl.BlockSpec(memory_space=pl.ANY),
            scratch_shapes=[pltpu.VMEM((MOE_ROW_TILE * tr, LANES), F32),
                            pltpu.SemaphoreType.DMA, pltpu.SemaphoreType.DMA],
        ),
        out_shape=jax.ShapeDtypeStruct((n_sorted * tr, LANES), F32),
        compiler_params=_params(("arbitrary",)),
        name="moe_dispatch",
    )(off_end, pos, y_tok)


def _experts_kernel(texp_ref, nused_ref, xs_ref, wg_ref, wu_ref, wd_ref, ys_ref,
                    wg_bf, wu_bf, wd_bf):
    i = pl.program_id(0)
    used = i < nused_ref[0]

    @pl.when(jnp.logical_not(used))
    def _():
        ys_ref[...] = jnp.zeros_like(ys_ref)

    new_expert = jnp.logical_or(i == 0, texp_ref[i] != texp_ref[jnp.maximum(i - 1, 0)])

    @pl.when(jnp.logical_and(used, new_expert))
    def _():
        wg_bf[...] = wg_ref[0].astype(BF16)
        wu_bf[...] = wu_ref[0].astype(BF16)
        wd_bf[...] = wd_ref[0].astype(BF16)

    @pl.when(used)
    def _():
        d = wg_ref.shape[1]
        x = _load_token_tiles(xs_ref, 0, MOE_ROW_TILE, d).astype(BF16)
        hg = jnp.dot(x, wg_bf[...], preferred_element_type=F32)
        hu = jnp.dot(x, wu_bf[...], preferred_element_type=F32)
        h = (hg * jax.nn.sigmoid(hg)) * hu
        _store_token_tiles(ys_ref, 0, jnp.dot(h.astype(BF16), wd_bf[...],
                                              preferred_element_type=F32))


def _experts(xs, tile_expert, n_used, w_gate, w_up, w_down):
    _, d, dff = w_gate.shape
    tile_rows = MOE_ROW_TILE * d // LANES
    tile = lambda i, te, nu: (jnp.minimum(i, nu[0] - 1), 0)
    every = lambda i, te, nu: (i, 0)
    wsel = lambda i, te, nu: (te[jnp.minimum(i, nu[0] - 1)], 0, 0)
    return pl.pallas_call(
        _experts_kernel,
        grid_spec=pltpu.PrefetchScalarGridSpec(
            num_scalar_prefetch=2,
            grid=(xs.shape[0] // tile_rows,),
            in_specs=[
                pl.BlockSpec((tile_rows, LANES), tile),
                pl.BlockSpec((1, d, dff), wsel),
                pl.BlockSpec((1, d, dff), wsel),
                pl.BlockSpec((1, dff, d), wsel),
            ],
            out_specs=pl.BlockSpec((tile_rows, LANES), every),
            scratch_shapes=[pltpu.VMEM((d, dff), BF16), pltpu.VMEM((d, dff), BF16),
                            pltpu.VMEM((dff, d), BF16)],
        ),
        out_shape=jax.ShapeDtypeStruct(xs.shape, F32),
        compiler_params=_params(("arbitrary",)),
        name="moe_experts",
    )(tile_expert, n_used, xs, w_gate, w_up, w_down)


def _combine_kernel(pos_ref, posn_ref, y_ref, gate_ref, g_ref, b_ref, ys_hbm, o_ref,
                    buf, sem, *, tmc, tr):
    i = pl.program_id(0)
    n = pl.num_programs(0)
    d = o_ref.shape[1]
    half = tmc * tr
    slot_rows = 2 * half

    def gather(p_ref, slot):
        base = slot * slot_rows
        def issue(g, c):
            r0 = g * DMA_UNROLL
            ps = [[p_ref[0, 0, k * tmc + r0 + j] for k in range(2)] for j in range(DMA_UNROLL)]
            for j in range(DMA_UNROLL):
                for k in range(2):
                    src = ys_hbm.at[pl.ds(pl.multiple_of(ps[j][k], tr), tr)]
                    dst = buf.at[pl.ds(pl.multiple_of(base + k * half + (r0 + j) * tr, tr), tr)]
                    pltpu.make_async_copy(src, dst, sem.at[slot]).start(priority=k)
            return c
        lax.fori_loop(0, tmc // DMA_UNROLL, issue, 0)

    @pl.when(i == 0)
    def _():
        gather(pos_ref, 0)

    @pl.when(i + 1 < n)
    def _():
        gather(posn_ref, (i + 1) % 2)

    slot = i % 2
    base = pl.multiple_of(slot * slot_rows, slot_rows)
    _row_copy_wait(ys_hbm, buf.at[pl.ds(base, slot_rows)], sem.at[slot], slot_rows)
    gate = gate_ref[...]
    ffn = (gate[:, 0:1] * _load_token_tiles(buf, base, tmc, d)
           + gate[:, 1:2] * _load_token_tiles(buf, base + half, tmc, d))
    y = _load_token_tiles(y_ref, 0, tmc, d)
    o_ref[...] = _layer_norm_rows(DEEPNORM_ALPHA * y + ffn, g_ref[...], b_ref[...])


def _combine(ys, pos, y_tok, gates, ln_g, ln_b, *, tmc, tr):
    d = tr * LANES
    n_tok = y_tok.shape[0] // tr
    n = n_tok // tmc
    rows = lambda i: (i, 0)
    fixed = lambda i: (0, 0)
    return pl.pallas_call(
        functools.partial(_combine_kernel, tmc=tmc, tr=tr),
        grid=(n,),
        in_specs=[
            pl.BlockSpec((1, 1, 2 * tmc), lambda i: (i, 0, 0), memory_space=pltpu.SMEM),
            pl.BlockSpec((1, 1, 2 * tmc), lambda i: (jnp.minimum(i + 1, n - 1), 0, 0),
                         memory_space=pltpu.SMEM),
            pl.BlockSpec((tmc * tr, LANES), rows),
            pl.BlockSpec((tmc, LANES), rows),
            pl.BlockSpec((1, d), fixed),
            pl.BlockSpec((1, d), fixed),
            pl.BlockSpec(memory_space=pl.ANY),
        ],
        out_specs=pl.BlockSpec((tmc, d), rows),
        out_shape=jax.ShapeDtypeStruct((n_tok, d), F32),
        scratch_shapes=[pltpu.VMEM((2 * 2 * tmc * tr, LANES), F32),
                        pltpu.SemaphoreType.DMA((2,))],
        compiler_params=_params(("arbitrary",)),
        name="moe_combine",
    )(pos, pos, y_tok, gates, ln_g.reshape(1, d), ln_b.reshape(1, d), ys)


def _tile_positions(route, off, t, tr):
    e = route[:, 0:2, :]
    hit = e[..., None] == jnp.arange(N_EXPERTS, dtype=jnp.int32)
    pos = (jnp.sum(jnp.where(hit, off, 0), axis=-1) + route[:, 2:4, :]) * tr
    pos = pos.transpose(1, 0, 2).reshape(2, -1, t)
    return pos.transpose(1, 0, 2).reshape(-1, 1, 2 * t)


def _moe_sparse(y_tok, route, gates, counts, w_gate, w_up, w_down, ln_g, ln_b, *, tmd, tmc):
    d = w_gate.shape[1]
    tr = d // LANES
    n_tok = y_tok.shape[0] // tr
    cnt = counts[:, 0].astype(jnp.int32)
    padded = (cnt + MOE_ROW_TILE - 1) // MOE_ROW_TILE * MOE_ROW_TILE
    off_end = jnp.cumsum(padded)
    off = off_end - padded
    n_sorted = 2 * n_tok + N_EXPERTS * MOE_ROW_TILE
    n_tiles = n_sorted // MOE_ROW_TILE
    n_used = (off_end[-1:] // MOE_ROW_TILE).astype(jnp.int32)
    tile_start = jnp.arange(n_tiles, dtype=jnp.int32) * MOE_ROW_TILE
    tile_expert = jnp.minimum(jnp.sum(tile_start[:, None] >= off_end[None, :], axis=1),
                              N_EXPERTS - 1).astype(jnp.int32)
    xs = _dispatch(y_tok, _tile_positions(route, off, tmd, tr), off_end.astype(jnp.int32),
                   n_sorted=n_sorted, tmd=tmd, tr=tr)
    ys = _experts(xs, tile_expert, n_used, w_gate, w_up, w_down)
    return _combine(ys, _tile_positions(route, off, tmc, tr), y_tok, gates, ln_g, ln_b,
                    tmc=tmc, tr=tr)


POOL_HALO = 32


def _inproj1_kernel(x_ref, w_ref, pw_ref, ps_ref, qt_ref, k_ref, vt_ref, d_ref, halo_ref,
                    *, sb_w):
    i = pl.program_id(1)
    x = x_ref[0].astype(BF16)
    tm = x.shape[0]
    q = jnp.dot(x, w_ref[:, 0:sb_w], preferred_element_type=F32) * ATTN_SCALE
    qt_ref[0] = q.T.astype(BF16)
    k_ref[0] = jnp.dot(x, w_ref[:, sb_w:2 * sb_w], preferred_element_type=F32).astype(BF16)
    v = jnp.dot(x, w_ref[:, 2 * sb_w:3 * sb_w], preferred_element_type=F32)
    vt_ref[0] = v.T.astype(BF16)
    u = jnp.dot(x, w_ref[:, 3 * sb_w:], preferred_element_type=F32)

    @pl.when(i == 0)
    def _():
        halo_ref[...] = jnp.zeros_like(halo_ref)

    tpos = i * tm + lax.broadcasted_iota(jnp.int32, (tm, 1), 0)
    outs = []
    for g, win in enumerate(POOL_WINDOWS):
        cs = slice(g * POOL_CH, (g + 1) * POOL_CH)
        cur = u[:, cs]
        run = jnp.concatenate([halo_ref[:, cs], cur], axis=0)
        span = 1
        while span < win:
            n = run.shape[0] - span
            run = run[span:span + n] + run[0:n]
            span *= 2
        off = POOL_HALO - (win - 1)
        wsum = run[off:off + tm]
        count = jnp.minimum(tpos + 1, win).astype(F32)
        pooled = wsum / count - cur
        mixed = jnp.dot(pooled.astype(BF16), pw_ref[g], preferred_element_type=F32)
        outs.append(mixed * ps_ref[:, cs])
    halo_ref[...] = u[tm - POOL_HALO:tm, :]
    d_ref[0] = jnp.concatenate(outs, axis=1).astype(d_ref.dtype)


def _inproj1(x, w_bf, pool_w_bf, pool_scale, *, sb_w, tm):
    bsz, n_pos, d = x.shape
    n_in = w_bf.shape[1]
    pool_w = n_in - 3 * sb_w
    n_g = len(POOL_WINDOWS)
    return pl.pallas_call(
        functools.partial(_inproj1_kernel, sb_w=sb_w),
        grid=(bsz, n_pos // tm),
        in_specs=[
            pl.BlockSpec((1, tm, d), lambda b, i: (b, i, 0)),
            pl.BlockSpec((d, n_in), lambda b, i: (0, 0)),
            pl.BlockSpec((n_g, POOL_CH, POOL_CH), lambda b, i: (0, 0, 0)),
            pl.BlockSpec((1, pool_w), lambda b, i: (0, 0)),
        ],
        out_specs=[
            pl.BlockSpec((1, sb_w, tm), lambda b, i: (b, 0, i)),
            pl.BlockSpec((1, tm, sb_w), lambda b, i: (b, i, 0)),
            pl.BlockSpec((1, sb_w, tm), lambda b, i: (b, 0, i)),
            pl.BlockSpec((1, tm, pool_w), lambda b, i: (b, i, 0)),
        ],
        out_shape=[
            jax.ShapeDtypeStruct((bsz, sb_w, n_pos), BF16),
            jax.ShapeDtypeStruct((bsz, n_pos, sb_w), BF16),
            jax.ShapeDtypeStruct((bsz, sb_w, n_pos), BF16),
            jax.ShapeDtypeStruct((bsz, n_pos, pool_w), BF16),
        ],
        scratch_shapes=[pltpu.VMEM((POOL_HALO, pool_w), F32)],
        compiler_params=_params(("parallel", "arbitrary")),
        name="inproj1",
    )(x, w_bf, pool_w_bf, pool_scale.astype(F32).reshape(1, pool_w))


def _sb_attn_kernel(qt_ref, k_ref, vt_ref, o_ref, za_ref, zb_ref, *, tq, tk):
    i = pl.program_id(2)
    qt = qt_ref[0]
    zeros = jnp.zeros((HEAD_DIM, tq), BF16)
    qpads = (jnp.concatenate([qt[:HEAD_DIM], zeros], axis=0),
             jnp.concatenate([zeros, qt[HEAD_DIM:]], axis=0))
    cb = SB_SCAN_ROWS
    upper2 = ((lax.broadcasted_iota(jnp.int32, (cb, 2 * cb), 1) & (cb - 1))
              > lax.broadcasted_iota(jnp.int32, (cb, 2 * cb), 0)).astype(BF16)

    heads = range(2)

    def scores(j, buf):
        kj = k_ref[0, pl.ds(pl.multiple_of(j * tk, tk), tk), :]
        for hh in heads:
            buf[hh] = jnp.dot(kj, qpads[hh], preferred_element_type=F32)

    def consume(j, buf, carry, masked):
        k0 = pl.multiple_of(j * tk, tk)
        if masked:
            key = k0 + lax.broadcasted_iota(jnp.int32, (tk, tq), 0)
            qpos = i * tq + lax.broadcasted_iota(jnp.int32, (tk, tq), 1)
            keep = key < qpos
        zs = [buf[hh] for hh in heads]
        sps = [jnp.maximum(z, 0.0) + jnp.log(1.0 + jnp.exp2(jnp.abs(z) * (-LOG2E))) for z in zs]
        if masked:
            sps = [jnp.where(keep, sp, 0.0) for sp in sps]
        his = [sp.astype(BF16) for sp in sps]
        los = [(sp - hi.astype(F32)).astype(BF16) for sp, hi in zip(sps, his)]
        laters = [carry[hh][1] for hh in heads]
        pieces = [[None] * (tk // cb) for _ in heads]
        for c in reversed(range(tk // cb)):
            rs = slice(c * cb, (c + 1) * cb)
            sufs = [jnp.dot(upper2, jnp.concatenate([his[hh][rs], los[hh][rs]], axis=0),
                            preferred_element_type=F32) for hh in heads]
            for hh in heads:
                spc = sps[hh][rs]
                pieces[hh][c] = jnp.exp2((zs[hh][rs] - (spc + sufs[hh] + laters[hh])) * LOG2E)
                laters[hh] = laters[hh] + sufs[hh][0:1] + spc[0:1]
        out = []
        for hh in heads:
            a = jnp.concatenate(pieces[hh], axis=0)
            if masked:
                a = jnp.where(keep, a, 0.0)
            vj = vt_ref[0, hh * HEAD_DIM:(hh + 1) * HEAD_DIM, pl.ds(k0, tk)]
            acc = carry[hh][0] + jnp.dot(vj, a.astype(BF16), preferred_element_type=F32)
            out.append((acc, laters[hh]))
        return tuple(out)

    init = (jnp.zeros((HEAD_DIM, tq), F32), jnp.zeros((1, tq), F32))
    carry = (init, init)
    n_diag = tq // tk
    last = (i + 1) * n_diag - 1
    n_old = i * n_diag
    bufs = (za_ref, zb_ref)
    scores(last, bufs[0])
    for d in range(n_diag):
        nxt = last - d - 1 if d + 1 < n_diag else jnp.maximum(n_old - 1, 0)
        scores(nxt, bufs[(d + 1) % 2])
        carry = consume(last - d, bufs[d % 2], carry, True)
    first_old = bufs[n_diag % 2]
    spare = bufs[(n_diag + 1) % 2]

    def reach(c):
        return jnp.min(jnp.minimum(c[0][1], c[1][1]))

    def live(state):
        t, r, _ = state
        return jnp.logical_and(t < n_old, r < SB_UNDERFLOW)

    def older(state):
        t, _, c = state
        scores(n_old - 1 - t, spare)
        c = consume(n_old - 1 - t, spare, c, False)
        return t + 1, reach(c), c

    r0 = reach(carry)
    carry = lax.cond(jnp.logical_and(n_old > 0, r0 < SB_UNDERFLOW),
                     lambda c: consume(n_old - 1, first_old, c, False), lambda c: c, carry)
    _, _, carry = lax.while_loop(live, older, (jnp.int32(1), reach(carry), carry))
    o_ref[0] = jnp.concatenate([carry[0][0], carry[1][0]], axis=0).T.astype(o_ref.dtype)


def _sb_attention(qt, k, vt, *, tq, tk):
    bsz, sb_w, n_pos = qt.shape
    return pl.pallas_call(
        functools.partial(_sb_attn_kernel, tq=tq, tk=tk),
        grid=(bsz, sb_w // LANES, n_pos // tq),
        in_specs=[
            pl.BlockSpec((1, LANES, tq), lambda b, h, i: (b, h, i)),
            pl.BlockSpec((1, n_pos, LANES), lambda b, h, i: (b, 0, h)),
            pl.BlockSpec((1, LANES, n_pos), lambda b, h, i: (b, h, 0)),
        ],
        out_specs=pl.BlockSpec((1, tq, LANES), lambda b, h, i: (b, i, h)),
        out_shape=jax.ShapeDtypeStruct((bsz, n_pos, sb_w), BF16),
        scratch_shapes=[pltpu.VMEM((2, tk, tq), F32), pltpu.VMEM((2, tk, tq), F32)],
        compiler_params=_params(("parallel", "parallel", "arbitrary")),
        name="sb_attention",
    )(qt, k, vt)


def _rope_tables(n_pos):
    inv = ROPE_THETA ** (-jnp.arange(0, HEAD_DIM, 2, dtype=F32) / HEAD_DIM)
    ang = jnp.arange(n_pos, dtype=F32)[:, None] * inv[None, :]
    ang = jnp.concatenate([ang, ang], axis=-1)
    return jnp.cos(ang), jnp.sin(ang)


def _lambda_init(layer_idx):
    return 0.8 - 0.6 * math.exp(-0.3 * layer_idx)


def kernel(x, router_w, router_bias, l0_in_proj, l0_conv_w, l0_conv_b, l0_conv_ln_g, l0_conv_ln_b, l0_lambda_q1, l0_lambda_k1, l0_lambda_q2, l0_lambda_k2, l0_subln_g, l0_out_proj, l0_ln_mix_g, l0_ln_mix_b, l0_w_gate, l0_w_up, l0_w_down, l0_ln_ffn_g, l0_ln_ffn_b, l1_in_proj, l1_pool_w, l1_pool_scale, l1_out_proj, l1_ln_mix_g, l1_ln_mix_b, l1_w_gate, l1_w_up, l1_w_down, l1_ln_ffn_g, l1_ln_ffn_b):
    bsz, n_pos, d = x.shape
    n_tok = bsz * n_pos
    x = x.astype(F32)

    conv_ch = l0_conv_w.shape[1]
    dv = l0_subln_g.shape[0]
    n_diff = (l0_out_proj.shape[0] - conv_ch) // dv
    qk_w = n_diff * 2 * HEAD_DIM
    v_w = n_diff * dv

    tm = min(512, n_pos)
    tq_diff = min(512, n_pos)
    tq_sb = min(512, n_pos)
    tk_sb = min(256, n_pos)
    ts = min(256, n_pos)
    tmc = min(256, n_pos)
    tm_out = min(1024, n_tok)

    cos, sin = _rope_tables(n_pos)
    sign = jnp.where(jnp.arange(HEAD_DIM) < HEAD_DIM // 2, -1.0, 1.0).astype(F32)
    cos128 = jnp.concatenate([cos, cos], axis=-1)
    sins128 = jnp.concatenate([sin * sign, sin * sign], axis=-1)
    cost = cos.T
    sint = sin.T

    glu, qt, k, vt = _inproj0(x, l0_in_proj.astype(BF16), cos128, sins128, cost, sint,
                              conv_ch=conv_ch, qk_w=qk_w, v_w=v_w, tm=tm)
    a = _conv_module(glu, l0_conv_w.astype(F32), l0_conv_b.astype(F32),
                     l0_conv_ln_g.astype(F32), l0_conv_ln_b.astype(F32), ts=ts)
    lam_init = _lambda_init(0)
    lam = (jnp.exp(jnp.sum(l0_lambda_q1.astype(F32) * l0_lambda_k1.astype(F32)))
           - jnp.exp(jnp.sum(l0_lambda_q2.astype(F32) * l0_lambda_k2.astype(F32))) + lam_init)
    o = _diff_attention(qt, k, vt, lam, l0_subln_g, n_heads=n_diff, lam_init=lam_init,
                        tq=tq_diff, tk=tq_diff)
    xf = x.reshape(n_tok, d)
    y, route, gates, counts = _outproj_ln_route(
        a.reshape(n_tok, conv_ch), o.reshape(n_tok, v_w), l0_out_proj, xf,
        l0_ln_mix_g.astype(F32), l0_ln_mix_b.astype(F32), router_w, router_bias, tm=tm_out)
    xf = _moe_sparse(y, route, gates, counts, l0_w_gate, l0_w_up, l0_w_down,
                     l0_ln_ffn_g.astype(F32), l0_ln_ffn_b.astype(F32), tmd=tm_out, tmc=tmc)

    sb_w = l1_in_proj.shape[1] - len(POOL_WINDOWS) * POOL_CH
    sb_w = sb_w // 3
    qt, k, vt, dpool = _inproj1(xf.reshape(bsz, n_pos, d), l1_in_proj.astype(BF16),
                                l1_pool_w.astype(BF16), l1_pool_scale, sb_w=sb_w, tm=tm)
    o = _sb_attention(qt, k, vt, tq=tq_sb, tk=tk_sb)
    y, route, gates, counts = _outproj_ln_route(
        o.reshape(n_tok, sb_w), dpool.reshape(n_tok, -1), l1_out_proj, xf,
        l1_ln_mix_g.astype(F32), l1_ln_mix_b.astype(F32), router_w, router_bias, tm=tm_out)
    xf = _moe_sparse(y, route, gates, counts, l1_w_gate, l1_w_up, l1_w_down,
                     l1_ln_ffn_g.astype(F32), l1_ln_ffn_b.astype(F32), tmd=tm_out, tmc=tmc)
    return xf.reshape(bsz, n_pos, d)
```

```python
import functools
import math

import jax
import jax.numpy as jnp
from jax import lax
from jax.experimental import pallas as pl
from jax.experimental.pallas import tpu as pltpu

F32 = jnp.float32
BF16 = jnp.bfloat16

DEPTH = 2
HEAD_DIM = 64
ROPE_THETA = 10000.0
LN_EPS = 1e-5
RMS_EPS = 1e-5
DEEPNORM_ALPHA = (2 * DEPTH) ** 0.25
CONV_WIDTH = 31
POOL_WINDOWS = (2, 4, 8, 16)
POOL_CH = 128
N_EXPERTS = 16
N_GROUPS = 4
EXPERTS_PER_GROUP = N_EXPERTS // N_GROUPS
ATTN_SCALE = HEAD_DIM ** -0.5
LOG2E = math.log2(math.e)

LANES = 128
SUBLANES = 8
MIB = 1024 * 1024
VMEM_LIMIT = 52 * MIB

NEG_BIG = -1e30
SB_SCAN_ROWS = 256
SB_UNDERFLOW = 110.0


def _params(semantics, vmem=VMEM_LIMIT):
    return pltpu.CompilerParams(dimension_semantics=semantics, vmem_limit_bytes=vmem)


def _layer_norm_rows(h, g, b):
    mu = jnp.mean(h, axis=-1, keepdims=True)
    d = h - mu
    var = jnp.mean(d * d, axis=-1, keepdims=True)
    return d * lax.rsqrt(var + LN_EPS) * g + b


def _load_token_tiles(ref, row0, n, d):
    k = d // LANES
    return jnp.concatenate([ref[pl.ds(row0 + c, n, stride=k), :] for c in range(k)], axis=1)


def _store_token_tiles(ref, row0, val):
    n, d = val.shape
    k = d // LANES
    for c in range(k):
        ref[pl.ds(row0 + c, n, stride=k), :] = val[:, c * LANES:(c + 1) * LANES]


def _inproj0_kernel(x_ref, w_ref, cos_ref, sins_ref, cost_ref, sint_ref,
                    glu_ref, qt_ref, k_ref, vt_ref, *, conv_ch, qk_w):
    x = x_ref[0].astype(BF16)
    c0, c1, c2, c3 = conv_ch, 2 * conv_ch, 2 * conv_ch + qk_w, 2 * conv_ch + 2 * qk_w

    a_val = jnp.dot(x, w_ref[:, 0:c0], preferred_element_type=F32)
    a_gate = jnp.dot(x, w_ref[:, c0:c1], preferred_element_type=F32)
    glu_ref[0] = a_val * jax.nn.sigmoid(a_gate)

    dq = jnp.dot(x, w_ref[:, c1:c2], preferred_element_type=F32)
    dqt = dq.T
    cost = cost_ref[...]
    sint = sint_ref[...]
    half = HEAD_DIM // 2
    pieces = []
    for blk in range(qk_w // HEAD_DIM):
        u = dqt[blk * HEAD_DIM:(blk + 1) * HEAD_DIM]
        rot = jnp.concatenate([-u[half:], u[:half]], axis=0)
        pieces.append((u * cost + rot * sint) * (ATTN_SCALE * LOG2E))
    qt_ref[0] = jnp.concatenate(pieces, axis=0).astype(BF16)

    dk = jnp.dot(x, w_ref[:, c2:c3], preferred_element_type=F32)
    cos = cos_ref[...]
    sins = sins_ref[...]
    lane = lax.broadcasted_iota(jnp.int32, cos.shape, 1)
    first = (lane & (HEAD_DIM - 1)) < half
    kparts = []
    for h in range(qk_w // LANES):
        u = dk[:, h * LANES:(h + 1) * LANES]
        rot = jnp.where(first, pltpu.roll(u, LANES - half, 1), pltpu.roll(u, half, 1))
        kparts.append(u * cos + rot * sins)
    k_ref[0] = jnp.concatenate(kparts, axis=1).astype(BF16)

    dv = jnp.dot(x, w_ref[:, c3:], preferred_element_type=F32)
    vt_ref[0] = dv.astype(BF16)


def _inproj0(x, w_bf, cos128, sins128, cost, sint, *, conv_ch, qk_w, v_w, tm):
    bsz, n_pos, d = x.shape
    n_in = w_bf.shape[1]
    kern = functools.partial(_inproj0_kernel, conv_ch=conv_ch, qk_w=qk_w)
    return pl.pallas_call(
        kern,
        grid=(bsz, n_pos // tm),
        in_specs=[
            pl.BlockSpec((1, tm, d), lambda b, i: (b, i, 0)),
            pl.BlockSpec((d, n_in), lambda b, i: (0, 0)),
            pl.BlockSpec((tm, LANES), lambda b, i: (i, 0)),
            pl.BlockSpec((tm, LANES), lambda b, i: (i, 0)),
            pl.BlockSpec((HEAD_DIM, tm), lambda b, i: (0, i)),
            pl.BlockSpec((HEAD_DIM, tm), lambda b, i: (0, i)),
        ],
        out_specs=[
            pl.BlockSpec((1, tm, conv_ch), lambda b, i: (b, i, 0)),
            pl.BlockSpec((1, qk_w, tm), lambda b, i: (b, 0, i)),
            pl.BlockSpec((1, tm, qk_w), lambda b, i: (b, i, 0)),
            pl.BlockSpec((1, tm, v_w), lambda b, i: (b, i, 0)),
        ],
        out_shape=[
            jax.ShapeDtypeStruct((bsz, n_pos, conv_ch), F32),
            jax.ShapeDtypeStruct((bsz, qk_w, n_pos), BF16),
            jax.ShapeDtypeStruct((bsz, n_pos, qk_w), BF16),
            jax.ShapeDtypeStruct((bsz, n_pos, v_w), BF16),
        ],
        compiler_params=_params(("parallel", "arbitrary")),
        name="inproj0",
    )(x, w_bf, cos128, sins128, cost, sint)


CONV_HALO = 32
CONV_ROW_BLOCK = 64


def _conv_kernel(a_ref, cw_ref, cb_ref, g_ref, b_ref, o_ref, win_ref, acc_ref, *, ts):
    i = pl.program_id(1)
    t0 = pl.multiple_of(i * ts, ts)
    h0 = pl.multiple_of(jnp.maximum(t0 - CONV_HALO, 0), CONV_HALO)
    n_ch = a_ref.shape[2]
    n_shift = ts + CONV_HALO - SUBLANES
    for c in range(n_ch // LANES):
        cs = slice(c * LANES, (c + 1) * LANES)
        cur = a_ref[0, pl.ds(t0, ts), cs]
        halo = a_ref[0, pl.ds(h0, CONV_HALO), cs]
        halo = jnp.where(i > 0, halo, 0.0)
        xw = jnp.concatenate([halo, cur], axis=0)
        win_ref[c, 0] = xw
        for r in range(1, SUBLANES):
            win_ref[c, r, 0:n_shift] = xw[r:r + n_shift]
        def row_block(rb, carry, c=c, cs=cs, full=xw.shape[0]):
            base = pl.multiple_of(rb * CONV_ROW_BLOCK, CONV_ROW_BLOCK)
            acc = jnp.zeros((CONV_ROW_BLOCK, LANES), F32) + cb_ref[:, cs]
            for r in range(SUBLANES):
                rows = full if r == 0 else n_shift
                for a in range(CONV_HALO // SUBLANES + 1):
                    w = SUBLANES * a + r - (CONV_HALO - CONV_WIDTH + 1)
                    if 0 <= w < CONV_WIDTH and SUBLANES * a + ts <= rows:
                        tap = win_ref[c, r, pl.ds(base + SUBLANES * a, CONV_ROW_BLOCK)]
                        acc = acc + tap * cw_ref[w:w + 1, cs]
            acc_ref[pl.ds(base, CONV_ROW_BLOCK), cs] = acc
            return carry
        lax.fori_loop(0, ts // CONV_ROW_BLOCK, row_block, 0)
    y = _layer_norm_rows(acc_ref[...], g_ref[...], b_ref[...])
    o_ref[0] = (y * jax.nn.sigmoid(y)).astype(o_ref.dtype)


def _conv_module(a, conv_w, conv_b, ln_g, ln_b, *, ts):
    bsz, n_pos, n_ch = a.shape
    return pl.pallas_call(
        functools.partial(_conv_kernel, ts=ts),
        grid=(bsz, n_pos // ts),
        in_specs=[
            pl.BlockSpec((1, n_pos, n_ch), lambda b, i: (b, 0, 0)),
            pl.BlockSpec((CONV_WIDTH, n_ch), lambda b, i: (0, 0)),
            pl.BlockSpec((1, n_ch), lambda b, i: (0, 0)),
            pl.BlockSpec((1, n_ch), lambda b, i: (0, 0)),
            pl.BlockSpec((1, n_ch), lambda b, i: (0, 0)),
        ],
        out_specs=pl.BlockSpec((1, ts, n_ch), lambda b, i: (b, i, 0)),
        out_shape=jax.ShapeDtypeStruct((bsz, n_pos, n_ch), BF16),
        scratch_shapes=[pltpu.VMEM((n_ch // LANES, SUBLANES, ts + CONV_HALO, LANES), F32),
                        pltpu.VMEM((ts, n_ch), F32)],
        compiler_params=_params(("parallel", "arbitrary")),
        name="conv_module",
    )(a, conv_w, conv_b.reshape(1, n_ch), ln_g.reshape(1, n_ch), ln_b.reshape(1, n_ch))


def _diff_attn_kernel(lam_ref, qt_ref, k_ref, v_ref, g_ref, o_ref, sa_ref, sb_ref,
                      *, tq, tk, out_scale):
    i = pl.program_id(2)
    lam = lam_ref[0, 0]
    qt = qt_ref[0]
    zeros = jnp.zeros((HEAD_DIM, tq), BF16)
    qa = jnp.concatenate([qt[:HEAD_DIM], zeros], axis=0)
    qb = jnp.concatenate([zeros, qt[HEAD_DIM:]], axis=0)
    dv = v_ref.shape[2]

    def online(s, vj1, m, l, acc):
        m_new = jnp.maximum(m, jnp.max(s, axis=0, keepdims=True))
        alpha = jnp.exp2(m - m_new)
        p = jnp.exp2((s - m_new).astype(BF16))
        pv = lax.dot_general(p, vj1, (((0,), (0,)), ((), ())),
                             preferred_element_type=F32)
        a_col = jnp.broadcast_to(alpha, (SUBLANES, tq)).T[:, 0:1]
        return m_new, a_col * l + pv[:, dv:dv + 1], a_col * acc + pv[:, :dv]

    def scores(j, buf):
        kj = k_ref[0, pl.ds(pl.multiple_of(j * tk, tk), tk), :]
        buf[0] = jnp.dot(kj, qa, preferred_element_type=F32)
        buf[1] = jnp.dot(kj, qb, preferred_element_type=F32)

    def consume(j, buf, carry, masked):
        m1, l1, acc1, m2, l2, acc2 = carry
        k0 = pl.multiple_of(j * tk, tk)
        vj = jnp.concatenate([v_ref[0, pl.ds(k0, tk), :],
                              jnp.ones((tk, dv), BF16)], axis=1)
        s1, s2 = buf[0], buf[1]
        if masked:
            key = k0 + lax.broadcasted_iota(jnp.int32, (tk, tq), 0)
            qpos = i * tq + lax.broadcasted_iota(jnp.int32, (tk, tq), 1)
            keep = key <= qpos
            s1 = jnp.where(keep, s1, NEG_BIG)
            s2 = jnp.where(keep, s2, NEG_BIG)
        m1, l1, acc1 = online(s1, vj, m1, l1, acc1)
        m2, l2, acc2 = online(s2, vj, m2, l2, acc2)
        return m1, l1, acc1, m2, l2, acc2

    row = jnp.full((1, tq), NEG_BIG, F32)
    zcol = jnp.zeros((tq, 1), F32)
    zacc = jnp.zeros((tq, dv), F32)
    carry = (row, zcol, zacc, row, zcol, zacc)

    scores(0, sa_ref)

    def pair(u, c):
        scores(2 * u + 1, sb_ref)
        c = consume(2 * u, sa_ref, c, False)
        scores(2 * u + 2, sa_ref)
        return consume(2 * u + 1, sb_ref, c, False)

    carry = lax.fori_loop(0, lax.shift_right_logical(i, 1), pair, carry)

    def odd_tail(c):
        scores(i, sb_ref)
        c = consume(i - 1, sa_ref, c, False)
        return consume(i, sb_ref, c, True)

    def even_tail(c):
        return consume(i, sa_ref, c, True)

    m1, l1, acc1, m2, l2, acc2 = lax.cond((i & 1) == 1, odd_tail, even_tail, carry)

    o = acc1 / l1 - lam * (acc2 / l2)
    o = o * lax.rsqrt(jnp.mean(o * o, axis=1, keepdims=True) + RMS_EPS)
    o = o * g_ref[...] * out_scale
    o_ref[0] = o.astype(o_ref.dtype)


def _diff_attention(qt, k, v, lam, subln_g, *, n_heads, lam_init, tq, tk):
    bsz, _, n_pos = qt.shape
    dv = v.shape[2] // n_heads
    kern = functools.partial(_diff_attn_kernel, tq=tq, tk=tk, out_scale=1.0 - lam_init)
    return pl.pallas_call(
        kern,
        grid=(bsz, n_heads, n_pos // tq),
        in_specs=[
            pl.BlockSpec(memory_space=pltpu.SMEM),
            pl.BlockSpec((1, 2 * HEAD_DIM, tq), lambda b, h, i: (b, h, i)),
            pl.BlockSpec((1, n_pos, 2 * HEAD_DIM), lambda b, h, i: (b, 0, h)),
            pl.BlockSpec((1, n_pos, dv), lambda b, h, i: (b, 0, h)),
            pl.BlockSpec((1, dv), lambda b, h, i: (0, 0)),
        ],
        out_specs=pl.BlockSpec((1, tq, dv), lambda b, h, i: (b, i, h)),
        out_shape=jax.ShapeDtypeStruct((bsz, n_pos, n_heads * dv), BF16),
        scratch_shapes=[pltpu.VMEM((2, tk, tq), F32), pltpu.VMEM((2, tk, tq), F32)],
        compiler_params=_params(("parallel", "parallel", "arbitrary")),
        name="diff_attention",
    )(lam.reshape(1, 1), qt, k, v, subln_g.astype(F32).reshape(1, dv))


def _masked_top2(vals, member, row):
    big = N_EXPERTS
    v = jnp.where(member, vals, -jnp.inf)
    v1 = jnp.max(v, axis=0, keepdims=True)
    i1 = jnp.min(jnp.where(v == v1, row, big), axis=0, keepdims=True)
    v = jnp.where(row == i1, -jnp.inf, v)
    v2 = jnp.max(v, axis=0, keepdims=True)
    i2 = jnp.min(jnp.where(v == v2, row, big), axis=0, keepdims=True)
    return v1, i1, v2, i2


def _top2_route(aff, sel):
    row = lax.broadcasted_iota(jnp.int32, sel.shape, 0)
    group = lax.shift_right_logical(row, EXPERTS_PER_GROUP.bit_length() - 1)
    gbest = None
    for g in range(N_GROUPS):
        v1, _, v2, _ = _masked_top2(sel, group == g, row)
        score = v1 + v2
        if gbest is None:
            gbest, gval = jnp.zeros_like(score, dtype=jnp.int32), score
        else:
            better = score > gval
            gbest = jnp.where(better, g, gbest)
            gval = jnp.where(better, score, gval)
    _, e1, _, e2 = _masked_top2(sel, group == gbest, row)
    a1 = jnp.sum(jnp.where(row == e1, aff, 0.0), axis=0, keepdims=True)
    a2 = jnp.sum(jnp.where(row == e2, aff, 0.0), axis=0, keepdims=True)
    denom = a1 + a2
    return e1, e2, a1 / denom, a2 / denom


ROUTE_ROWS = SUBLANES


def _outproj_kernel(pa_ref, pb_ref, wa_ref, wb_ref, x_ref, g_ref, b_ref, rw_ref, rb_ref,
                    y_ref, route_ref, gate_ref, count_ref, seen_ref):
    i = pl.program_id(0)

    @pl.when(i == 0)
    def _():
        seen_ref[...] = jnp.zeros_like(seen_ref)

    mix = jnp.dot(pa_ref[...], wa_ref[...], preferred_element_type=F32)
    mix = mix + jnp.dot(pb_ref[...], wb_ref[...], preferred_element_type=F32)
    y = _layer_norm_rows(DEEPNORM_ALPHA * x_ref[...] + mix, g_ref[...], b_ref[...])
    _store_token_tiles(y_ref, 0, y)
    tm = y.shape[0]
    yh = y.astype(BF16)
    yl = (y - yh.astype(F32)).astype(BF16)
    nt = (((1,), (1,)), ((), ()))
    both = lax.dot_general(rw_ref[...], yh, nt, preferred_element_type=F32)
    logits = (both[:N_EXPERTS] + both[N_EXPERTS:]
              + lax.dot_general(rw_ref[0:N_EXPERTS], yl, nt, preferred_element_type=F32))
    aff = jax.nn.sigmoid(logits)
    e1, e2, g1, g2 = _top2_route(aff, aff + rb_ref[...])

    erow = lax.broadcasted_iota(jnp.int32, (N_EXPERTS, tm), 0)
    hit = jnp.where((erow == e1) | (erow == e2), 1.0, 0.0)
    upto = (lax.broadcasted_iota(jnp.int32, (tm, tm), 0)
            <= lax.broadcasted_iota(jnp.int32, (tm, tm), 1)).astype(BF16)
    incl = jnp.dot(hit.astype(BF16), upto, preferred_element_type=F32)
    before = seen_ref[:, 0:1] + incl - hit
    r1 = jnp.sum(jnp.where(erow == e1, before, 0.0), axis=0, keepdims=True).astype(jnp.int32)
    r2 = jnp.sum(jnp.where(erow == e2, before, 0.0), axis=0, keepdims=True).astype(jnp.int32)
    seen_ref[...] = seen_ref[...] + jnp.sum(hit, axis=1, keepdims=True)
    count_ref[...] = seen_ref[...]

    rrow = lax.broadcasted_iota(jnp.int32, (ROUTE_ROWS, tm), 0)
    route_ref[0] = jnp.where(rrow == 0, e1, jnp.where(rrow == 1, e2,
                             jnp.where(rrow == 2, r1, jnp.where(rrow == 3, r2, 0))))
    grow = lax.broadcasted_iota(jnp.int32, (LANES, tm), 0)
    gate_ref[...] = jnp.where(grow == 0, g1, jnp.where(grow == 1, g2, 0.0)).T


def _outproj_ln_route(pa, pb, w_out, x, ln_g, ln_b, router_w, router_bias, *, tm):
    n_tok, d = x.shape
    ka, kb = pa.shape[1], pb.shape[1]
    wa = w_out[:ka].astype(BF16)
    wb = w_out[ka:].astype(BF16)
    rw = router_w.astype(F32).T
    rw_hi = rw.astype(BF16)
    rw_lo = (rw - rw_hi.astype(F32)).astype(BF16)
    rows = lambda i: (i, 0)
    fixed = lambda i: (0, 0)
    return pl.pallas_call(
        _outproj_kernel,
        grid=(n_tok // tm,),
        in_specs=[
            pl.BlockSpec((tm, ka), rows),
            pl.BlockSpec((tm, kb), rows),
            pl.BlockSpec((ka, d), fixed),
            pl.BlockSpec((kb, d), fixed),
            pl.BlockSpec((tm, d), rows),
            pl.BlockSpec((1, d), fixed),
            pl.BlockSpec((1, d), fixed),
            pl.BlockSpec((2 * N_EXPERTS, d), fixed),
            pl.BlockSpec((N_EXPERTS, 1), fixed),
        ],
        out_specs=[
            pl.BlockSpec((tm * d // LANES, LANES), rows),
            pl.BlockSpec((1, ROUTE_ROWS, tm), lambda i: (i, 0, 0)),
            pl.BlockSpec((tm, LANES), rows),
            pl.BlockSpec((N_EXPERTS, LANES), fixed),
        ],
        out_shape=[
            jax.ShapeDtypeStruct((n_tok * d // LANES, LANES), F32),
            jax.ShapeDtypeStruct((n_tok // tm, ROUTE_ROWS, tm), jnp.int32),
            jax.ShapeDtypeStruct((n_tok, LANES), F32),
            jax.ShapeDtypeStruct((N_EXPERTS, LANES), F32),
        ],
        scratch_shapes=[pltpu.VMEM((N_EXPERTS, LANES), F32)],
        compiler_params=_params(("arbitrary",)),
        name="outproj_ln_route",
    )(pa, pb, wa, wb, x, ln_g.reshape(1, d), ln_b.reshape(1, d),
      jnp.concatenate([rw_hi, rw_lo], axis=0), router_bias.astype(F32).reshape(N_EXPERTS, 1))


MOE_ROW_TILE = 512
DMA_UNROLL = 8


def _row_copy_wait(src_hbm, dst, sem, n_rows):
    pltpu.make_async_copy(src_hbm.at[pl.ds(0, n_rows)], dst, sem).wait()


def _dispatch_kernel(offend_ref, pos_ref, y_ref, xs_hbm, zero_ref, sem, zsem, *, tmd, tr):
    i = pl.program_id(0)
    tile_rows = MOE_ROW_TILE * tr

    @pl.when(i == 0)
    def _():
        zero_ref[...] = jnp.zeros_like(zero_ref)
        def tail_copy(e):
            start = pl.multiple_of((offend_ref[e] - MOE_ROW_TILE) * tr, tile_rows)
            return pltpu.make_async_copy(zero_ref, xs_hbm.at[pl.ds(start, tile_rows)], zsem)
        def nonempty(e):
            return offend_ref[e] > (offend_ref[e - 1] if e > 0 else 0)
        for e in range(N_EXPERTS):
            @pl.when(nonempty(e))
            def _():
                tail_copy(e).start()
        for e in range(N_EXPERTS):
            @pl.when(nonempty(e))
            def _():
                tail_copy(e).wait()
        def unused_copy(j):
            return pltpu.make_async_copy(
                zero_ref, xs_hbm.at[pl.ds(pl.multiple_of(j * tile_rows, tile_rows), tile_rows)],
                zsem)
        first_unused = lax.shift_right_logical(offend_ref[N_EXPERTS - 1],
                                               MOE_ROW_TILE.bit_length() - 1)
        n_tiles = xs_hbm.shape[0] // tile_rows
        lax.fori_loop(first_unused, n_tiles, lambda j, c: (unused_copy(j).start(), c)[1], 0)
        lax.fori_loop(first_unused, n_tiles, lambda j, c: (unused_copy(j).wait(), c)[1], 0)

    def issue(g, c):
        r0 = g * DMA_UNROLL
        ps = [[pos_ref[0, 0, k * tmd + r0 + j] for k in range(2)] for j in range(DMA_UNROLL)]
        for j in range(DMA_UNROLL):
            src = y_ref.at[pl.ds(pl.multiple_of((r0 + j) * tr, tr), tr)]
            for k in range(2):
                dst = xs_hbm.at[pl.ds(pl.multiple_of(ps[j][k], tr), tr)]
                pltpu.make_async_copy(src, dst, sem).start(priority=k)
        return c

    lax.fori_loop(0, tmd // DMA_UNROLL, issue, 0)
    for k in range(2):
        pltpu.make_async_copy(y_ref, xs_hbm.at[pl.ds(0, tmd * tr)], sem).wait()


def _dispatch(y_tok, pos, off_end, *, n_sorted, tmd, tr):
    n_tok = y_tok.shape[0] // tr
    return pl.pallas_call(
        functools.partial(_dispatch_kernel, tmd=tmd, tr=tr),
        grid_spec=pltpu.PrefetchScalarGridSpec(
            num_scalar_prefetch=1,
            grid=(n_tok // tmd,),
            in_specs=[
                pl.BlockSpec((1, 1, 2 * tmd), lambda i, off: (i, 0, 0), memory_space=pltpu.SMEM),
                pl.BlockSpec((tmd * tr, LANES), lambda i, off: (i, 0)),
            ],
            out_specs=pl.BlockSpec(memory_space=pl.ANY),
            scratch_shapes=[pltpu.VMEM((MOE_ROW_TILE * tr, LANES), F32),
                            pltpu.SemaphoreType.DMA, pltpu.SemaphoreType.DMA],
        ),
        out_shape=jax.ShapeDtypeStruct((n_sorted * tr, LANES), F32),
        compiler_params=_params(("arbitrary",)),
        name="moe_dispatch",
    )(off_end, pos, y_tok)


def _experts_kernel(texp_ref, nused_ref, xs_ref, wg_ref, wu_ref, wd_ref, ys_ref,
                    wg_bf, wu_bf, wd_bf):
    i = pl.program_id(0)
    used = i < nused_ref[0]

    @pl.when(jnp.logical_not(used))
    def _():
        ys_ref[...] = jnp.zeros_like(ys_ref)

    new_expert = jnp.logical_or(i == 0, texp_ref[i] != texp_ref[jnp.maximum(i - 1, 0)])

    @pl.when(jnp.logical_and(used, new_expert))
    def _():
        wg_bf[...] = wg_ref[0].astype(BF16)
        wu_bf[...] = wu_ref[0].astype(BF16)
        wd_bf[...] = wd_ref[0].astype(BF16)

    @pl.when(used)
    def _():
        d = wg_ref.shape[1]
        x = _load_token_tiles(xs_ref, 0, MOE_ROW_TILE, d).astype(BF16)
        hg = jnp.dot(x, wg_bf[...], preferred_element_type=F32)
        hu = jnp.dot(x, wu_bf[...], preferred_element_type=F32)
        h = (hg * jax.nn.sigmoid(hg)) * hu
        _store_token_tiles(ys_ref, 0, jnp.dot(h.astype(BF16), wd_bf[...],
                                              preferred_element_type=F32))


def _experts(xs, tile_expert, n_used, w_gate, w_up, w_down):
    _, d, dff = w_gate.shape
    tile_rows = MOE_ROW_TILE * d // LANES
    tile = lambda i, te, nu: (jnp.minimum(i, nu[0] - 1), 0)
    every = lambda i, te, nu: (i, 0)
    wsel = lambda i, te, nu: (te[jnp.minimum(i, nu[0] - 1)], 0, 0)
    return pl.pallas_call(
        _experts_kernel,
        grid_spec=pltpu.PrefetchScalarGridSpec(
            num_scalar_prefetch=2,
            grid=(xs.shape[0] // tile_rows,),
            in_specs=[
                pl.BlockSpec((tile_rows, LANES), tile),
                pl.BlockSpec((1, d, dff), wsel),
                pl.BlockSpec((1, d, dff), wsel),
                pl.BlockSpec((1, dff, d), wsel),
            ],
            out_specs=pl.BlockSpec((tile_rows, LANES), every),
            scratch_shapes=[pltpu.VMEM((d, dff), BF16), pltpu.VMEM((d, dff), BF16),
                            pltpu.VMEM((dff, d), BF16)],
        ),
        out_shape=jax.ShapeDtypeStruct(xs.shape, F32),
        compiler_params=_params(("arbitrary",)),
        name="moe_experts",
    )(tile_expert, n_used, xs, w_gate, w_up, w_down)


def _combine_kernel(pos_ref, posn_ref, y_ref, gate_ref, g_ref, b_ref, ys_hbm, o_ref,
                    buf, sem, *, tmc, tr):
    i = pl.program_id(0)
    n = pl.num_programs(0)
    d = o_ref.shape[1]
    half = tmc * tr
    slot_rows = 2 * half

    def gather(p_ref, slot):
        base = slot * slot_rows
        def issue(g, c):
            r0 = g * DMA_UNROLL
            ps = [[p_ref[0, 0, k * tmc + r0 + j] for k in range(2)] for j in range(DMA_UNROLL)]
            for j in range(DMA_UNROLL):
                for k in range(2):
                    src = ys_hbm.at[pl.ds(pl.multiple_of(ps[j][k], tr), tr)]
                    dst = buf.at[pl.ds(pl.multiple_of(base + k * half + (r0 + j) * tr, tr), tr)]
                    pltpu.make_async_copy(src, dst, sem.at[slot]).start(priority=k)
            return c
        lax.fori_loop(0, tmc // DMA_UNROLL, issue, 0)

    @pl.when(i == 0)
    def _():
        gather(pos_ref, 0)

    @pl.when(i + 1 < n)
    def _():
        gather(posn_ref, (i + 1) % 2)

    slot = i % 2
    base = pl.multiple_of(slot * slot_rows, slot_rows)
    _row_copy_wait(ys_hbm, buf.at[pl.ds(base, slot_rows)], sem.at[slot], slot_rows)
    gate = gate_ref[...]
    ffn = (gate[:, 0:1] * _load_token_tiles(buf, base, tmc, d)
           + gate[:, 1:2] * _load_token_tiles(buf, base + half, tmc, d))
    y = _load_token_tiles(y_ref, 0, tmc, d)
    o_ref[...] = _layer_norm_rows(DEEPNORM_ALPHA * y + ffn, g_ref[...], b_ref[...])


def _combine(ys, pos, y_tok, gates, ln_g, ln_b, *, tmc, tr):
    d = tr * LANES
    n_tok = y_tok.shape[0] // tr
    n = n_tok // tmc
    rows = lambda i: (i, 0)
    fixed = lambda i: (0, 0)
    return pl.pallas_call(
        functools.partial(_combine_kernel, tmc=tmc, tr=tr),
        grid=(n,),
        in_specs=[
            pl.BlockSpec((1, 1, 2 * tmc), lambda i: (i, 0, 0), memory_space=pltpu.SMEM),
            pl.BlockSpec((1, 1, 2 * tmc), lambda i: (jnp.minimum(i + 1, n - 1), 0, 0),
                         memory_space=pltpu.SMEM),
            pl.BlockSpec((tmc * tr, LANES), rows),
            pl.BlockSpec((tmc, LANES), rows),
            pl.BlockSpec((1, d), fixed),
            pl.BlockSpec((1, d), fixed),
            pl.BlockSpec(memory_space=pl.ANY),
        ],
        out_specs=pl.BlockSpec((tmc, d), rows),
        out_shape=jax.ShapeDtypeStruct((n_tok, d), F32),
        scratch_shapes=[pltpu.VMEM((2 * 2 * tmc * tr, LANES), F32),
                        pltpu.SemaphoreType.DMA((2,))],
        compiler_params=_params(("arbitrary",)),
        name="moe_combine",
    )(pos, pos, y_tok, gates, ln_g.reshape(1, d), ln_b.reshape(1, d), ys)


def _tile_positions(route, off, t, tr):
    e = route[:, 0:2, :]
    hit = e[..., None] == jnp.arange(N_EXPERTS, dtype=jnp.int32)
    pos = (jnp.sum(jnp.where(hit, off, 0), axis=-1) + route[:, 2:4, :]) * tr
    pos = pos.transpose(1, 0, 2).reshape(2, -1, t)
    return pos.transpose(1, 0, 2).reshape(-1, 1, 2 * t)


def _moe_sparse(y_tok, route, gates, counts, w_gate, w_up, w_down, ln_g, ln_b, *, tmd, tmc):
    d = w_gate.shape[1]
    tr = d // LANES
    n_tok = y_tok.shape[0] // tr
    cnt = counts[:, 0].astype(jnp.int32)
    padded = (cnt + MOE_ROW_TILE - 1) // MOE_ROW_TILE * MOE_ROW_TILE
    off_end = jnp.cumsum(padded)
    off = off_end - padded
    n_sorted = 2 * n_tok + N_EXPERTS * MOE_ROW_TILE
    n_tiles = n_sorted // MOE_ROW_TILE
    n_used = (off_end[-1:] // MOE_ROW_TILE).astype(jnp.int32)
    tile_start = jnp.arange(n_tiles, dtype=jnp.int32) * MOE_ROW_TILE
    tile_expert = jnp.minimum(jnp.sum(tile_start[:, None] >= off_end[None, :], axis=1),
                              N_EXPERTS - 1).astype(jnp.int32)
    xs = _dispatch(y_tok, _tile_positions(route, off, tmd, tr), off_end.astype(jnp.int32),
                   n_sorted=n_sorted, tmd=tmd, tr=tr)
    ys = _experts(xs, tile_expert, n_used, w_gate, w_up, w_down)
    return _combine(ys, _tile_positions(route, off, tmc, tr), y_tok, gates, ln_g, ln_b,
                    tmc=tmc, tr=tr)


POOL_HALO = 32


def _inproj1_kernel(x_ref, w_ref, pw_ref, ps_ref, qt_ref, k_ref, vt_ref, d_ref, halo_ref,
                    *, sb_w):
    i = pl.program_id(1)
    x = x_ref[0].astype(BF16)
    tm = x.shape[0]
    q = jnp.dot(x, w_ref[:, 0:sb_w], preferred_element_type=F32) * ATTN_SCALE
    qt_ref[0] = q.T.astype(BF16)
    k_ref[0] = jnp.dot(x, w_ref[:, sb_w:2 * sb_w], preferred_element_type=F32).astype(BF16)
    v = jnp.dot(x, w_ref[:, 2 * sb_w:3 * sb_w], preferred_element_type=F32)
    vt_ref[0] = v.T.astype(BF16)
    u = jnp.dot(x, w_ref[:, 3 * sb_w:], preferred_element_type=F32)

    @pl.when(i == 0)
    def _():
        halo_ref[...] = jnp.zeros_like(halo_ref)

    tpos = i * tm + lax.broadcasted_iota(jnp.int32, (tm, 1), 0)
    outs = []
    for g, win in enumerate(POOL_WINDOWS):
        cs = slice(g * POOL_CH, (g + 1) * POOL_CH)
        cur = u[:, cs]
        run = jnp.concatenate([halo_ref[:, cs], cur], axis=0)
        span = 1
        while span < win:
            n = run.shape[0] - span
            run = run[span:span + n] + run[0:n]
            span *= 2
        off = POOL_HALO - (win - 1)
        wsum = run[off:off + tm]
        count = jnp.minimum(tpos + 1, win).astype(F32)
        pooled = wsum / count - cur
        mixed = jnp.dot(pooled.astype(BF16), pw_ref[g], preferred_element_type=F32)
        outs.append(mixed * ps_ref[:, cs])
    halo_ref[...] = u[tm - POOL_HALO:tm, :]
    d_ref[0] = jnp.concatenate(outs, axis=1).astype(d_ref.dtype)


def _inproj1(x, w_bf, pool_w_bf, pool_scale, *, sb_w, tm):
    bsz, n_pos, d = x.shape
    n_in = w_bf.shape[1]
    pool_w = n_in - 3 * sb_w
    n_g = len(POOL_WINDOWS)
    return pl.pallas_call(
        functools.partial(_inproj1_kernel, sb_w=sb_w),
        grid=(bsz, n_pos // tm),
        in_specs=[
            pl.BlockSpec((1, tm, d), lambda b, i: (b, i, 0)),
            pl.BlockSpec((d, n_in), lambda b, i: (0, 0)),
            pl.BlockSpec((n_g, POOL_CH, POOL_CH), lambda b, i: (0, 0, 0)),
            pl.BlockSpec((1, pool_w), lambda b, i: (0, 0)),
        ],
        out_specs=[
            pl.BlockSpec((1, sb_w, tm), lambda b, i: (b, 0, i)),
            pl.BlockSpec((1, tm, sb_w), lambda b, i: (b, i, 0)),
            pl.BlockSpec((1, sb_w, tm), lambda b, i: (b, 0, i)),
            pl.BlockSpec((1, tm, pool_w), lambda b, i: (b, i, 0)),
        ],
        out_shape=[
            jax.ShapeDtypeStruct((bsz, sb_w, n_pos), BF16),
            jax.ShapeDtypeStruct((bsz, n_pos, sb_w), BF16),
            jax.ShapeDtypeStruct((bsz, sb_w, n_pos), BF16),
            jax.ShapeDtypeStruct((bsz, n_pos, pool_w), BF16),
        ],
        scratch_shapes=[pltpu.VMEM((POOL_HALO, pool_w), F32)],
        compiler_params=_params(("parallel", "arbitrary")),
        name="inproj1",
    )(x, w_bf, pool_w_bf, pool_scale.astype(F32).reshape(1, pool_w))


def _sb_attn_kernel(qt_ref, k_ref, vt_ref, o_ref, za_ref, zb_ref, *, tq, tk):
    i = pl.program_id(2)
    qt = qt_ref[0]
    zeros = jnp.zeros((HEAD_DIM, tq), BF16)
    qpads = (jnp.concatenate([qt[:HEAD_DIM], zeros], axis=0),
             jnp.concatenate([zeros, qt[HEAD_DIM:]], axis=0))
    cb = min(SB_SCAN_ROWS, tk)
    upper = (lax.broadcasted_iota(jnp.int32, (cb, cb), 1)
             > lax.broadcasted_iota(jnp.int32, (cb, cb), 0)).astype(BF16)

    heads = range(2)

    def scores(j, buf):
        kj = k_ref[0, pl.ds(pl.multiple_of(j * tk, tk), tk), :]
        for hh in heads:
            buf[hh] = jnp.dot(kj, qpads[hh], preferred_element_type=F32)

    def consume(j, buf, carry, masked):
        k0 = pl.multiple_of(j * tk, tk)
        if masked:
            key = k0 + lax.broadcasted_iota(jnp.int32, (tk, tq), 0)
            qpos = i * tq + lax.broadcasted_iota(jnp.int32, (tk, tq), 1)
            keep = key < qpos
        zs = [buf[hh] for hh in heads]
        sps = [jnp.maximum(z, 0.0) + jnp.log(1.0 + jnp.exp2(jnp.abs(z) * (-LOG2E))) for z in zs]
        if masked:
            sps = [jnp.where(keep, sp, 0.0) for sp in sps]
        his = [sp.astype(BF16) for sp in sps]
        laters = [carry[hh][1] for hh in heads]
        pieces = [[None] * (tk // cb) for _ in heads]
        for c in reversed(range(tk // cb)):
            rs = slice(c * cb, (c + 1) * cb)
            sufs = [jnp.dot(upper, his[hh][rs], preferred_element_type=F32)
                    for hh in heads]
            for hh in heads:
                spc = sps[hh][rs]
                pieces[hh][c] = jnp.exp2((zs[hh][rs] - (spc + sufs[hh] + laters[hh])) * LOG2E)
                laters[hh] = laters[hh] + sufs[hh][0:1] + spc[0:1]
        out = []
        for hh in heads:
            a = jnp.concatenate(pieces[hh], axis=0)
            if masked:
                a = jnp.where(keep, a, 0.0)
            vj = vt_ref[0, hh * HEAD_DIM:(hh + 1) * HEAD_DIM, pl.ds(k0, tk)]
            acc = carry[hh][0] + jnp.dot(vj, a.astype(BF16), preferred_element_type=F32)
            out.append((acc, laters[hh]))
        return tuple(out)

    init = (jnp.zeros((HEAD_DIM, tq), F32), jnp.zeros((1, tq), F32))
    carry = (init, init)
    n_diag = tq // tk
    last = (i + 1) * n_diag - 1
    n_old = i * n_diag
    bufs = (za_ref, zb_ref)
    scores(last, bufs[0])
    for d in range(n_diag):
        nxt = last - d - 1 if d + 1 < n_diag else jnp.maximum(n_old - 1, 0)
        scores(nxt, bufs[(d + 1) % 2])
        carry = consume(last - d, bufs[d % 2], carry, True)
    first_old = bufs[n_diag % 2]
    spare = bufs[(n_diag + 1) % 2]

    def reach(c):
        return jnp.min(jnp.minimum(c[0][1], c[1][1]))

    def live(state):
        t, r, _ = state
        return jnp.logical_and(t < n_old, r < SB_UNDERFLOW)

    def older(state):
        t, _, c = state
        scores(n_old - 1 - t, spare)
        c = consume(n_old - 1 - t, spare, c, False)
        return t + 1, reach(c), c

    r0 = reach(carry)
    carry = lax.cond(jnp.logical_and(n_old > 0, r0 < SB_UNDERFLOW),
                     lambda c: consume(n_old - 1, first_old, c, False), lambda c: c, carry)
    _, _, carry = lax.while_loop(live, older, (jnp.int32(1), reach(carry), carry))
    o_ref[0] = jnp.concatenate([carry[0][0], carry[1][0]], axis=0).T.astype(o_ref.dtype)


def _sb_attention(qt, k, vt, *, tq, tk):
    bsz, sb_w, n_pos = qt.shape
    return pl.pallas_call(
        functools.partial(_sb_attn_kernel, tq=tq, tk=tk),
        grid=(bsz, sb_w // LANES, n_pos // tq),
        in_specs=[
            pl.BlockSpec((1, LANES, tq), lambda b, h, i: (b, h, i)),
            pl.BlockSpec((1, n_pos, LANES), lambda b, h, i: (b, 0, h)),
            pl.BlockSpec((1, LANES, n_pos), lambda b, h, i: (b, h, 0)),
        ],
        out_specs=pl.BlockSpec((1, tq, LANES), lambda b, h, i: (b, i, h)),
        out_shape=jax.ShapeDtypeStruct((bsz, n_pos, sb_w), BF16),
        scratch_shapes=[pltpu.VMEM((2, tk, tq), F32), pltpu.VMEM((2, tk, tq), F32)],
        compiler_params=_params(("parallel", "parallel", "arbitrary")),
        name="sb_attention",
    )(qt, k, vt)


def _rope_tables(n_pos):
    inv = ROPE_THETA ** (-jnp.arange(0, HEAD_DIM, 2, dtype=F32) / HEAD_DIM)
    ang = jnp.arange(n_pos, dtype=F32)[:, None] * inv[None, :]
    ang = jnp.concatenate([ang, ang], axis=-1)
    return jnp.cos(ang), jnp.sin(ang)


def _lambda_init(layer_idx):
    return 0.8 - 0.6 * math.exp(-0.3 * layer_idx)


def kernel(x, router_w, router_bias, l0_in_proj, l0_conv_w, l0_conv_b, l0_conv_ln_g, l0_conv_ln_b, l0_lambda_q1, l0_lambda_k1, l0_lambda_q2, l0_lambda_k2, l0_subln_g, l0_out_proj, l0_ln_mix_g, l0_ln_mix_b, l0_w_gate, l0_w_up, l0_w_down, l0_ln_ffn_g, l0_ln_ffn_b, l1_in_proj, l1_pool_w, l1_pool_scale, l1_out_proj, l1_ln_mix_g, l1_ln_mix_b, l1_w_gate, l1_w_up, l1_w_down, l1_ln_ffn_g, l1_ln_ffn_b):
    bsz, n_pos, d = x.shape
    n_tok = bsz * n_pos
    x = x.astype(F32)

    conv_ch = l0_conv_w.shape[1]
    dv = l0_subln_g.shape[0]
    n_diff = (l0_out_proj.shape[0] - conv_ch) // dv
    qk_w = n_diff * 2 * HEAD_DIM
    v_w = n_diff * dv

    tm = min(512, n_pos)
    tq_diff = min(512, n_pos)
    tq_sb = min(512, n_pos)
    tk_sb = min(256, n_pos)
    ts = min(256, n_pos)
    tmc = min(256, n_pos)
    tm_out = min(1024, n_tok)

    cos, sin = _rope_tables(n_pos)
    sign = jnp.where(jnp.arange(HEAD_DIM) < HEAD_DIM // 2, -1.0, 1.0).astype(F32)
    cos128 = jnp.concatenate([cos, cos], axis=-1)
    sins128 = jnp.concatenate([sin * sign, sin * sign], axis=-1)
    cost = cos.T
    sint = sin.T

    glu, qt, k, vt = _inproj0(x, l0_in_proj.astype(BF16), cos128, sins128, cost, sint,
                              conv_ch=conv_ch, qk_w=qk_w, v_w=v_w, tm=tm)
    a = _conv_module(glu, l0_conv_w.astype(F32), l0_conv_b.astype(F32),
                     l0_conv_ln_g.astype(F32), l0_conv_ln_b.astype(F32), ts=ts)
    lam_init = _lambda_init(0)
    lam = (jnp.exp(jnp.sum(l0_lambda_q1.astype(F32) * l0_lambda_k1.astype(F32)))
           - jnp.exp(jnp.sum(l0_lambda_q2.astype(F32) * l0_lambda_k2.astype(F32))) + lam_init)
    o = _diff_attention(qt, k, vt, lam, l0_subln_g, n_heads=n_diff, lam_init=lam_init,
                        tq=tq_diff, tk=tq_diff)
    xf = x.reshape(n_tok, d)
    y, route, gates, counts = _outproj_ln_route(
        a.reshape(n_tok, conv_ch), o.reshape(n_tok, v_w), l0_out_proj, xf,
        l0_ln_mix_g.astype(F32), l0_ln_mix_b.astype(F32), router_w, router_bias, tm=tm_out)
    xf = _moe_sparse(y, route, gates, counts, l0_w_gate, l0_w_up, l0_w_down,
                     l0_ln_ffn_g.astype(F32), l0_ln_ffn_b.astype(F32), tmd=tm_out, tmc=tmc)

    sb_w = l1_in_proj.shape[1] - len(POOL_WINDOWS) * POOL_CH
    sb_w = sb_w // 3
    qt, k, vt, dpool = _inproj1(xf.reshape(bsz, n_pos, d), l1_in_proj.astype(BF16),
                                l1_pool_w.astype(BF16), l1_pool_scale, sb_w=sb_w, tm=tm)
    o = _sb_attention(qt, k, vt, tq=tq_sb, tk=tk_sb)
    y, route, gates, counts = _outproj_ln_route(
        o.reshape(n_tok, sb_w), dpool.reshape(n_tok, -1), l1_out_proj, xf,
        l1_ln_mix_g.astype(F32), l1_ln_mix_b.astype(F32), router_w, router_bias, tm=tm_out)
    xf = _moe_sparse(y, route, gates, counts, l1_w_gate, l1_w_up, l1_w_down,
                     l1_ln_ffn_g.astype(F32), l1_ln_ffn_b.astype(F32), tmd=tm_out, tmc=tmc)
    return xf.reshape(bsz, n_pos, d)
```

```python
import functools
import math

import jax
import jax.numpy as jnp
from jax import lax
from jax.experimental import pallas as pl
from jax.experimental.pallas import tpu as pltpu

F32 = jnp.float32
BF16 = jnp.bfloat16

DEPTH = 2
HEAD_DIM = 64
ROPE_THETA = 10000.0
LN_EPS = 1e-5
RMS_EPS = 1e-5
DEEPNORM_ALPHA = (2 * DEPTH) ** 0.25
CONV_WIDTH = 31
POOL_WINDOWS = (2, 4, 8, 16)
POOL_CH = 128
N_EXPERTS = 16
N_GROUPS = 4
EXPERTS_PER_GROUP = N_EXPERTS // N_GROUPS
ATTN_SCALE = HEAD_DIM ** -0.5
LOG2E = math.log2(math.e)

LANES = 128
SUBLANES = 8
MIB = 1024 * 1024
VMEM_LIMIT = 52 * MIB

NEG_BIG = -1e30
SB_SCAN_ROWS = 256
SB_UNDERFLOW = 110.0


def _params(semantics, vmem=VMEM_LIMIT):
    return pltpu.CompilerParams(dimension_semantics=semantics, vmem_limit_bytes=vmem)


def _layer_norm_rows(h, g, b):
    mu = jnp.mean(h, axis=-1, keepdims=True)
    d = h - mu
    var = jnp.mean(d * d, axis=-1, keepdims=True)
    return d * lax.rsqrt(var + LN_EPS) * g + b


def _load_token_tiles(ref, row0, n, d):
    k = d // LANES
    return jnp.concatenate([ref[pl.ds(row0 + c, n, stride=k), :] for c in range(k)], axis=1)


def _store_token_tiles(ref, row0, val):
    n, d = val.shape
    k = d // LANES
    for c in range(k):
        ref[pl.ds(row0 + c, n, stride=k), :] = val[:, c * LANES:(c + 1) * LANES]


def _inproj0_kernel(x_ref, w_ref, cos_ref, sins_ref, cost_ref, sint_ref,
                    glu_ref, qt_ref, k_ref, vt_ref, *, conv_ch, qk_w):
    x = x_ref[0].astype(BF16)
    c0, c1, c2, c3 = conv_ch, 2 * conv_ch, 2 * conv_ch + qk_w, 2 * conv_ch + 2 * qk_w

    a_val = jnp.dot(x, w_ref[:, 0:c0], preferred_element_type=F32)
    a_gate = jnp.dot(x, w_ref[:, c0:c1], preferred_element_type=F32)
    glu_ref[0] = a_val * jax.nn.sigmoid(a_gate)

    dq = jnp.dot(x, w_ref[:, c1:c2], preferred_element_type=F32)
    dqt = dq.T
    cost = cost_ref[...]
    sint = sint_ref[...]
    half = HEAD_DIM // 2
    pieces = []
    for blk in range(qk_w // HEAD_DIM):
        u = dqt[blk * HEAD_DIM:(blk + 1) * HEAD_DIM]
        rot = jnp.concatenate([-u[half:], u[:half]], axis=0)
        pieces.append((u * cost + rot * sint) * (ATTN_SCALE * LOG2E))
    qt_ref[0] = jnp.concatenate(pieces, axis=0).astype(BF16)

    dk = jnp.dot(x, w_ref[:, c2:c3], preferred_element_type=F32)
    cos = cos_ref[...]
    sins = sins_ref[...]
    lane = lax.broadcasted_iota(jnp.int32, cos.shape, 1)
    first = (lane & (HEAD_DIM - 1)) < half
    kparts = []
    for h in range(qk_w // LANES):
        u = dk[:, h * LANES:(h + 1) * LANES]
        rot = jnp.where(first, pltpu.roll(u, LANES - half, 1), pltpu.roll(u, half, 1))
        kparts.append(u * cos + rot * sins)
    k_ref[0] = jnp.concatenate(kparts, axis=1).astype(BF16)

    dv = jnp.dot(x, w_ref[:, c3:], preferred_element_type=F32)
    vt_ref[0] = dv.T.astype(BF16)


def _inproj0(x, w_bf, cos128, sins128, cost, sint, *, conv_ch, qk_w, v_w, tm):
    bsz, n_pos, d = x.shape
    n_in = w_bf.shape[1]
    kern = functools.partial(_inproj0_kernel, conv_ch=conv_ch, qk_w=qk_w)
    return pl.pallas_call(
        kern,
        grid=(bsz, n_pos // tm),
        in_specs=[
            pl.BlockSpec((1, tm, d), lambda b, i: (b, i, 0)),
            pl.BlockSpec((d, n_in), lambda b, i: (0, 0)),
            pl.BlockSpec((tm, LANES), lambda b, i: (i, 0)),
            pl.BlockSpec((tm, LANES), lambda b, i: (i, 0)),
            pl.BlockSpec((HEAD_DIM, tm), lambda b, i: (0, i)),
            pl.BlockSpec((HEAD_DIM, tm), lambda b, i: (0, i)),
        ],
        out_specs=[
            pl.BlockSpec((1, tm, conv_ch), lambda b, i: (b, i, 0)),
            pl.BlockSpec((1, qk_w, tm), lambda b, i: (b, 0, i)),
            pl.BlockSpec((1, tm, qk_w), lambda b, i: (b, i, 0)),
            pl.BlockSpec((1, v_w, tm), lambda b, i: (b, 0, i)),
        ],
        out_shape=[
            jax.ShapeDtypeStruct((bsz, n_pos, conv_ch), F32),
            jax.ShapeDtypeStruct((bsz, qk_w, n_pos), BF16),
            jax.ShapeDtypeStruct((bsz, n_pos, qk_w), BF16),
            jax.ShapeDtypeStruct((bsz, v_w, n_pos), BF16),
        ],
        compiler_params=_params(("parallel", "arbitrary")),
        name="inproj0",
    )(x, w_bf, cos128, sins128, cost, sint)


CONV_HALO = 32
CONV_ROW_BLOCK = 64


def _conv_kernel(a_ref, cw_ref, cb_ref, g_ref, b_ref, o_ref, win_ref, acc_ref, *, ts):
    i = pl.program_id(1)
    t0 = pl.multiple_of(i * ts, ts)
    h0 = pl.multiple_of(jnp.maximum(t0 - CONV_HALO, 0), CONV_HALO)
    n_ch = a_ref.shape[2]
    n_shift = ts + CONV_HALO - SUBLANES
    for c in range(n_ch // LANES):
        cs = slice(c * LANES, (c + 1) * LANES)
        cur = a_ref[0, pl.ds(t0, ts), cs]
        halo = a_ref[0, pl.ds(h0, CONV_HALO), cs]
        halo = jnp.where(i > 0, halo, 0.0)
        xw = jnp.concatenate([halo, cur], axis=0)
        win_ref[c, 0] = xw
        for r in range(1, SUBLANES):
            win_ref[c, r, 0:n_shift] = xw[r:r + n_shift]
        def row_block(rb, carry, c=c, cs=cs, full=xw.shape[0]):
            base = pl.multiple_of(rb * CONV_ROW_BLOCK, CONV_ROW_BLOCK)
            acc = jnp.zeros((CONV_ROW_BLOCK, LANES), F32) + cb_ref[:, cs]
            for r in range(SUBLANES):
                rows = full if r == 0 else n_shift
                for a in range(CONV_HALO // SUBLANES + 1):
                    w = SUBLANES * a + r - (CONV_HALO - CONV_WIDTH + 1)
                    if 0 <= w < CONV_WIDTH and SUBLANES * a + ts <= rows:
                        tap = win_ref[c, r, pl.ds(base + SUBLANES * a, CONV_ROW_BLOCK)]
                        acc = acc + tap * cw_ref[w:w + 1, cs]
            acc_ref[pl.ds(base, CONV_ROW_BLOCK), cs] = acc
            return carry
        lax.fori_loop(0, ts // CONV_ROW_BLOCK, row_block, 0)
    y = _layer_norm_rows(acc_ref[...], g_ref[...], b_ref[...])
    o_ref[0] = (y * jax.nn.sigmoid(y)).astype(o_ref.dtype)


def _conv_module(a, conv_w, conv_b, ln_g, ln_b, *, ts):
    bsz, n_pos, n_ch = a.shape
    return pl.pallas_call(
        functools.partial(_conv_kernel, ts=ts),
        grid=(bsz, n_pos // ts),
        in_specs=[
            pl.BlockSpec((1, n_pos, n_ch), lambda b, i: (b, 0, 0)),
            pl.BlockSpec((CONV_WIDTH, n_ch), lambda b, i: (0, 0)),
            pl.BlockSpec((1, n_ch), lambda b, i: (0, 0)),
            pl.BlockSpec((1, n_ch), lambda b, i: (0, 0)),
            pl.BlockSpec((1, n_ch), lambda b, i: (0, 0)),
        ],
        out_specs=pl.BlockSpec((1, ts, n_ch), lambda b, i: (b, i, 0)),
        out_shape=jax.ShapeDtypeStruct((bsz, n_pos, n_ch), BF16),
        scratch_shapes=[pltpu.VMEM((n_ch // LANES, SUBLANES, ts + CONV_HALO, LANES), F32),
                        pltpu.VMEM((ts, n_ch), F32)],
        compiler_params=_params(("parallel", "arbitrary")),
        name="conv_module",
    )(a, conv_w, conv_b.reshape(1, n_ch), ln_g.reshape(1, n_ch), ln_b.reshape(1, n_ch))


def _diff_attn_kernel(lam_ref, qt_ref, k_ref, vt_ref, g_ref, o_ref, sa_ref, sb_ref,
                      *, tq, tk, out_scale):
    i = pl.program_id(2)
    lam = lam_ref[0, 0]
    qt = qt_ref[0]
    zeros = jnp.zeros((HEAD_DIM, tq), BF16)
    qa = jnp.concatenate([qt[:HEAD_DIM], zeros], axis=0)
    qb = jnp.concatenate([zeros, qt[HEAD_DIM:]], axis=0)
    dv = vt_ref.shape[1]

    def online(s, vj, m, l, acc):
        m_new = jnp.maximum(m, jnp.max(s, axis=0, keepdims=True))
        alpha = jnp.exp2(m - m_new)
        p = jnp.exp2(s - m_new)
        l_new = alpha * l + jnp.sum(p, axis=0, keepdims=True)
        acc_new = alpha * acc + jnp.dot(vj, p.astype(BF16), preferred_element_type=F32)
        return m_new, l_new, acc_new

    def scores(j, buf):
        kj = k_ref[0, pl.ds(pl.multiple_of(j * tk, tk), tk), :]
        buf[0] = jnp.dot(kj, qa, preferred_element_type=F32)
        buf[1] = jnp.dot(kj, qb, preferred_element_type=F32)

    def consume(j, buf, carry, masked):
        m1, l1, acc1, m2, l2, acc2 = carry
        k0 = pl.multiple_of(j * tk, tk)
        vj = vt_ref[0, :, pl.ds(k0, tk)]
        s1, s2 = buf[0], buf[1]
        if masked:
            key = k0 + lax.broadcasted_iota(jnp.int32, (tk, tq), 0)
            qpos = i * tq + lax.broadcasted_iota(jnp.int32, (tk, tq), 1)
            keep = key <= qpos
            s1 = jnp.where(keep, s1, NEG_BIG)
            s2 = jnp.where(keep, s2, NEG_BIG)
        m1, l1, acc1 = online(s1, vj, m1, l1, acc1)
        m2, l2, acc2 = online(s2, vj, m2, l2, acc2)
        return m1, l1, acc1, m2, l2, acc2

    row = jnp.full((1, tq), NEG_BIG, F32)
    zrow = jnp.zeros((1, tq), F32)
    zacc = jnp.zeros((dv, tq), F32)
    carry = (row, zrow, zacc, row, zrow, zacc)

    scores(0, sa_ref)

    def pair(u, c):
        scores(2 * u + 1, sb_ref)
        c = consume(2 * u, sa_ref, c, False)
        scores(2 * u + 2, sa_ref)
        return consume(2 * u + 1, sb_ref, c, False)

    carry = lax.fori_loop(0, lax.shift_right_logical(i, 1), pair, carry)

    def odd_tail(c):
        scores(i, sb_ref)
        c = consume(i - 1, sa_ref, c, False)
        return consume(i, sb_ref, c, True)

    def even_tail(c):
        return consume(i, sa_ref, c, True)

    m1, l1, acc1, m2, l2, acc2 = lax.cond((i & 1) == 1, odd_tail, even_tail, carry)

    o = acc1 / l1 - lam * (acc2 / l2)
    o = o * lax.rsqrt(jnp.mean(o * o, axis=0, keepdims=True) + RMS_EPS)
    o = o * g_ref[...] * out_scale
    o_ref[0] = o.T.astype(o_ref.dtype)


def _diff_attention(qt, k, vt, lam, subln_g, *, n_heads, lam_init, tq, tk):
    bsz, _, n_pos = qt.shape
    dv = vt.shape[1] // n_heads
    kern = functools.partial(_diff_attn_kernel, tq=tq, tk=tk, out_scale=1.0 - lam_init)
    return pl.pallas_call(
        kern,
        grid=(bsz, n_heads, n_pos // tq),
        in_specs=[
            pl.BlockSpec(memory_space=pltpu.SMEM),
            pl.BlockSpec((1, 2 * HEAD_DIM, tq), lambda b, h, i: (b, h, i)),
            pl.BlockSpec((1, n_pos, 2 * HEAD_DIM), lambda b, h, i: (b, 0, h)),
            pl.BlockSpec((1, dv, n_pos), lambda b, h, i: (b, h, 0)),
            pl.BlockSpec((dv, 1), lambda b, h, i: (0, 0)),
        ],
        out_specs=pl.BlockSpec((1, tq, dv), lambda b, h, i: (b, i, h)),
        out_shape=jax.ShapeDtypeStruct((bsz, n_pos, n_heads * dv), BF16),
        scratch_shapes=[pltpu.VMEM((2, tk, tq), F32), pltpu.VMEM((2, tk, tq), F32)],
        compiler_params=_params(("parallel", "parallel", "arbitrary")),
        name="diff_attention",
    )(lam.reshape(1, 1), qt, k, vt, subln_g.astype(F32).reshape(dv, 1))


def _masked_top2(vals, member, row):
    big = N_EXPERTS
    v = jnp.where(member, vals, -jnp.inf)
    v1 = jnp.max(v, axis=0, keepdims=True)
    i1 = jnp.min(jnp.where(v == v1, row, big), axis=0, keepdims=True)
    v = jnp.where(row == i1, -jnp.inf, v)
    v2 = jnp.max(v, axis=0, keepdims=True)
    i2 = jnp.min(jnp.where(v == v2, row, big), axis=0, keepdims=True)
    return v1, i1, v2, i2


def _top2_route(aff, sel):
    row = lax.broadcasted_iota(jnp.int32, sel.shape, 0)
    group = lax.shift_right_logical(row, EXPERTS_PER_GROUP.bit_length() - 1)
    gbest = None
    for g in range(N_GROUPS):
        v1, _, v2, _ = _masked_top2(sel, group == g, row)
        score = v1 + v2
        if gbest is None:
            gbest, gval = jnp.zeros_like(score, dtype=jnp.int32), score
        else:
            better = score > gval
            gbest = jnp.where(better, g, gbest)
            gval = jnp.where(better, score, gval)
    _, e1, _, e2 = _masked_top2(sel, group == gbest, row)
    a1 = jnp.sum(jnp.where(row == e1, aff, 0.0), axis=0, keepdims=True)
    a2 = jnp.sum(jnp.where(row == e2, aff, 0.0), axis=0, keepdims=True)
    denom = a1 + a2
    return e1, e2, a1 / denom, a2 / denom


ROUTE_ROWS = SUBLANES


def _outproj_kernel(pa_ref, pb_ref, wa_ref, wb_ref, x_ref, g_ref, b_ref, rw_ref, rb_ref,
                    y_ref, route_ref, gate_ref, count_ref, seen_ref):
    i = pl.program_id(0)

    @pl.when(i == 0)
    def _():
        seen_ref[...] = jnp.zeros_like(seen_ref)

    mix = jnp.dot(pa_ref[...], wa_ref[...], preferred_element_type=F32)
    mix = mix + jnp.dot(pb_ref[...], wb_ref[...], preferred_element_type=F32)
    y = _layer_norm_rows(DEEPNORM_ALPHA * x_ref[...] + mix, g_ref[...], b_ref[...])
    _store_token_tiles(y_ref, 0, y)
    tm = y.shape[0]
    yh = y.astype(BF16)
    yl = (y - yh.astype(F32)).astype(BF16)
    nt = (((1,), (1,)), ((), ()))
    both = lax.dot_general(rw_ref[...], yh, nt, preferred_element_type=F32)
    logits = (both[:N_EXPERTS] + both[N_EXPERTS:]
              + lax.dot_general(rw_ref[0:N_EXPERTS], yl, nt, preferred_element_type=F32))
    aff = jax.nn.sigmoid(logits)
    e1, e2, g1, g2 = _top2_route(aff, aff + rb_ref[...])

    erow = lax.broadcasted_iota(jnp.int32, (N_EXPERTS, tm), 0)
    hit = jnp.where((erow == e1) | (erow == e2), 1.0, 0.0)
    upto = (lax.broadcasted_iota(jnp.int32, (tm, tm), 0)
            <= lax.broadcasted_iota(jnp.int32, (tm, tm), 1)).astype(BF16)
    incl = jnp.dot(hit.astype(BF16), upto, preferred_element_type=F32)
    before = seen_ref[:, 0:1] + incl - hit
    r1 = jnp.sum(jnp.where(erow == e1, before, 0.0), axis=0, keepdims=True).astype(jnp.int32)
    r2 = jnp.sum(jnp.where(erow == e2, before, 0.0), axis=0, keepdims=True).astype(jnp.int32)
    seen_ref[...] = seen_ref[...] + jnp.sum(hit, axis=1, keepdims=True)
    count_ref[...] = seen_ref[...]

    rrow = lax.broadcasted_iota(jnp.int32, (ROUTE_ROWS, tm), 0)
    route_ref[0] = jnp.where(rrow == 0, e1, jnp.where(rrow == 1, e2,
                             jnp.where(rrow == 2, r1, jnp.where(rrow == 3, r2, 0))))
    grow = lax.broadcasted_iota(jnp.int32, (LANES, tm), 0)
    gate_ref[...] = jnp.where(grow == 0, g1, jnp.where(grow == 1, g2, 0.0)).T


def _outproj_ln_route(pa, pb, w_out, x, ln_g, ln_b, router_w, router_bias, *, tm):
    n_tok, d = x.shape
    ka, kb = pa.shape[1], pb.shape[1]
    wa = w_out[:ka].astype(BF16)
    wb = w_out[ka:].astype(BF16)
    rw = router_w.astype(F32).T
    rw_hi = rw.astype(BF16)
    rw_lo = (rw - rw_hi.astype(F32)).astype(BF16)
    rows = lambda i: (i, 0)
    fixed = lambda i: (0, 0)
    return pl.pallas_call(
        _outproj_kernel,
        grid=(n_tok // tm,),
        in_specs=[
            pl.BlockSpec((tm, ka), rows),
            pl.BlockSpec((tm, kb), rows),
            pl.BlockSpec((ka, d), fixed),
            pl.BlockSpec((kb, d), fixed),
            pl.BlockSpec((tm, d), rows),
            pl.BlockSpec((1, d), fixed),
            pl.BlockSpec((1, d), fixed),
            pl.BlockSpec((2 * N_EXPERTS, d), fixed),
            pl.BlockSpec((N_EXPERTS, 1), fixed),
        ],
        out_specs=[
            pl.BlockSpec((tm * d // LANES, LANES), rows),
            pl.BlockSpec((1, ROUTE_ROWS, tm), lambda i: (i, 0, 0)),
            pl.BlockSpec((tm, LANES), rows),
            pl.BlockSpec((N_EXPERTS, LANES), fixed),
        ],
        out_shape=[
            jax.ShapeDtypeStruct((n_tok * d // LANES, LANES), F32),
            jax.ShapeDtypeStruct((n_tok // tm, ROUTE_ROWS, tm), jnp.int32),
            jax.ShapeDtypeStruct((n_tok, LANES), F32),
            jax.ShapeDtypeStruct((N_EXPERTS, LANES), F32),
        ],
        scratch_shapes=[pltpu.VMEM((N_EXPERTS, LANES), F32)],
        compiler_params=_params(("arbitrary",)),
        name="outproj_ln_route",
    )(pa, pb, wa, wb, x, ln_g.reshape(1, d), ln_b.reshape(1, d),
      jnp.concatenate([rw_hi, rw_lo], axis=0), router_bias.astype(F32).reshape(N_EXPERTS, 1))


MOE_ROW_TILE = 512
DMA_UNROLL = 8


def _row_copy_wait(src_hbm, dst, sem, n_rows):
    pltpu.make_async_copy(src_hbm.at[pl.ds(0, n_rows)], dst, sem).wait()


def _dispatch_kernel(offend_ref, pos_ref, y_ref, xs_hbm, zero_ref, sem, zsem, *, tmd, tr):
    i = pl.program_id(0)
    tile_rows = MOE_ROW_TILE * tr

    @pl.when(i == 0)
    def _():
        zero_ref[...] = jnp.zeros_like(zero_ref)
        def tail_copy(e):
            start = pl.multiple_of((offend_ref[e] - MOE_ROW_TILE) * tr, tile_rows)
            return pltpu.make_async_copy(zero_ref, xs_hbm.at[pl.ds(start, tile_rows)], zsem)
        def nonempty(e):
            return offend_ref[e] > (offend_ref[e - 1] if e > 0 else 0)
        for e in range(N_EXPERTS):
            @pl.when(nonempty(e))
            def _():
                tail_copy(e).start()
        for e in range(N_EXPERTS):
            @pl.when(nonempty(e))
            def _():
                tail_copy(e).wait()
        def unused_copy(j):
            return pltpu.make_async_copy(
                zero_ref, xs_hbm.at[pl.ds(pl.multiple_of(j * tile_rows, tile_rows), tile_rows)],
                zsem)
        first_unused = lax.shift_right_logical(offend_ref[N_EXPERTS - 1],
                                               MOE_ROW_TILE.bit_length() - 1)
        n_tiles = xs_hbm.shape[0] // tile_rows
        lax.fori_loop(first_unused, n_tiles, lambda j, c: (unused_copy(j).start(), c)[1], 0)
        lax.fori_loop(first_unused, n_tiles, lambda j, c: (unused_copy(j).wait(), c)[1], 0)

    def issue(g, c):
        r0 = g * DMA_UNROLL
        ps = [[pos_ref[0, 0, k * tmd + r0 + j] for k in range(2)] for j in range(DMA_UNROLL)]
        for j in range(DMA_UNROLL):
            src = y_ref.at[pl.ds(pl.multiple_of((r0 + j) * tr, tr), tr)]
            for k in range(2):
                dst = xs_hbm.at[pl.ds(pl.multiple_of(ps[j][k], tr), tr)]
                pltpu.make_async_copy(src, dst, sem).start(priority=k)
        return c

    lax.fori_loop(0, tmd // DMA_UNROLL, issue, 0)
    for k in range(2):
        pltpu.make_async_copy(y_ref, xs_hbm.at[pl.ds(0, tmd * tr)], sem).wait()


def _dispatch(y_tok, pos, off_end, *, n_sorted, tmd, tr):
    n_tok = y_tok.shape[0] // tr
    return pl.pallas_call(
        functools.partial(_dispatch_kernel, tmd=tmd, tr=tr),
        grid_spec=pltpu.PrefetchScalarGridSpec(
            num_scalar_prefetch=1,
            grid=(n_tok // tmd,),
            in_specs=[
                pl.BlockSpec((1, 1, 2 * tmd), lambda i, off: (i, 0, 0), memory_space=pltpu.SMEM),
                pl.BlockSpec((tmd * tr, LANES), lambda i, off: (i, 0)),
            ],
            out_specs=pl.BlockSpec(memory_space=pl.ANY),
            scratch_shapes=[pltpu.VMEM((MOE_ROW_TILE * tr, LANES), F32),
                            pltpu.SemaphoreType.DMA, pltpu.SemaphoreType.DMA],
        ),
        out_shape=jax.ShapeDtypeStruct((n_sorted * tr, LANES), F32),
        compiler_params=_params(("arbitrary",)),
        name="moe_dispatch",
    )(off_end, pos, y_tok)


def _experts_kernel(texp_ref, nused_ref, xs_ref, wg_ref, wu_ref, wd_ref, ys_ref,
                    wg_bf, wu_bf, wd_bf):
    i = pl.program_id(0)
    used = i < nused_ref[0]

    @pl.when(jnp.logical_not(used))
    def _():
        ys_ref[...] = jnp.zeros_like(ys_ref)

    new_expert = jnp.logical_or(i == 0, texp_ref[i] != texp_ref[jnp.maximum(i - 1, 0)])

    @pl.when(jnp.logical_and(used, new_expert))
    def _():
        wg_bf[...] = wg_ref[0].astype(BF16)
        wu_bf[...] = wu_ref[0].astype(BF16)
        wd_bf[...] = wd_ref[0].astype(BF16)

    @pl.when(used)
    def _():
        d = wg_ref.shape[1]
        x = _load_token_tiles(xs_ref, 0, MOE_ROW_TILE, d).astype(BF16)
        hg = jnp.dot(x, wg_bf[...], preferred_element_type=F32)
        hu = jnp.dot(x, wu_bf[...], preferred_element_type=F32)
        h = (hg * jax.nn.sigmoid(hg)) * hu
        _store_token_tiles(ys_ref, 0, jnp.dot(h.astype(BF16), wd_bf[...],
                                              preferred_element_type=F32))


def _experts(xs, tile_expert, n_used, w_gate, w_up, w_down):
    _, d, dff = w_gate.shape
    tile_rows = MOE_ROW_TILE * d // LANES
    tile = lambda i, te, nu: (jnp.minimum(i, nu[0] - 1), 0)
    every = lambda i, te, nu: (i, 0)
    wsel = lambda i, te, nu: (te[jnp.minimum(i, nu[0] - 1)], 0, 0)
    return pl.pallas_call(
        _experts_kernel,
        grid_spec=pltpu.PrefetchScalarGridSpec(
            num_scalar_prefetch=2,
            grid=(xs.shape[0] // tile_rows,),
            in_specs=[
                pl.BlockSpec((tile_rows, LANES), tile),
                pl.BlockSpec((1, d, dff), wsel),
                pl.BlockSpec((1, d, dff), wsel),
                pl.BlockSpec((1, dff, d), wsel),
            ],
            out_specs=pl.BlockSpec((tile_rows, LANES), every),
            scratch_shapes=[pltpu.VMEM((d, dff), BF16), pltpu.VMEM((d, dff), BF16),
                            pltpu.VMEM((dff, d), BF16)],
        ),
        out_shape=jax.ShapeDtypeStruct(xs.shape, F32),
        compiler_params=_params(("arbitrary",)),
        name="moe_experts",
    )(tile_expert, n_used, xs, w_gate, w_up, w_down)


def _combine_kernel(pos_ref, posn_ref, y_ref, gate_ref, g_ref, b_ref, ys_hbm, o_ref,
                    buf, sem, *, tmc, tr):
    i = pl.program_id(0)
    n = pl.num_programs(0)
    d = o_ref.shape[1]
    half = tmc * tr
    slot_rows = 2 * half

    def gather(p_ref, slot):
        base = slot * slot_rows
        def issue(g, c):
            r0 = g * DMA_UNROLL
            ps = [[p_ref[0, 0, k * tmc + r0 + j] for k in range(2)] for j in range(DMA_UNROLL)]
            for j in range(DMA_UNROLL):
                for k in range(2):
                    src = ys_hbm.at[pl.ds(pl.multiple_of(ps[j][k], tr), tr)]
                    dst = buf.at[pl.ds(pl.multiple_of(base + k * half + (r0 + j) * tr, tr), tr)]
                    pltpu.make_async_copy(src, dst, sem.at[slot]).start(priority=k)
            return c
        lax.fori_loop(0, tmc // DMA_UNROLL, issue, 0)

    @pl.when(i == 0)
    def _():
        gather(pos_ref, 0)

    @pl.when(i + 1 < n)
    def _():
        gather(posn_ref, (i + 1) % 2)

    slot = i % 2
    base = pl.multiple_of(slot * slot_rows, slot_rows)
    _row_copy_wait(ys_hbm, buf.at[pl.ds(base, slot_rows)], sem.at[slot], slot_rows)
    gate = gate_ref[...]
    ffn = (gate[:, 0:1] * _load_token_tiles(buf, base, tmc, d)
           + gate[:, 1:2] * _load_token_tiles(buf, base + half, tmc, d))
    y = _load_token_tiles(y_ref, 0, tmc, d)
    o_ref[...] = _layer_norm_rows(DEEPNORM_ALPHA * y + ffn, g_ref[...], b_ref[...])


def _combine(ys, pos, y_tok, gates, ln_g, ln_b, *, tmc, tr):
    d = tr * LANES
    n_tok = y_tok.shape[0] // tr
    n = n_tok // tmc
    rows = lambda i: (i, 0)
    fixed = lambda i: (0, 0)
    return pl.pallas_call(
        functools.partial(_combine_kernel, tmc=tmc, tr=tr),
        grid=(n,),
        in_specs=[
            pl.BlockSpec((1, 1, 2 * tmc), lambda i: (i, 0, 0), memory_space=pltpu.SMEM),
            pl.BlockSpec((1, 1, 2 * tmc), lambda i: (jnp.minimum(i + 1, n - 1), 0, 0),
                         memory_space=pltpu.SMEM),
            pl.BlockSpec((tmc * tr, LANES), rows),
            pl.BlockSpec((tmc, LANES), rows),
            pl.BlockSpec((1, d), fixed),
            pl.BlockSpec((1, d), fixed),
            pl.BlockSpec(memory_space=pl.ANY),
        ],
        out_specs=pl.BlockSpec((tmc, d), rows),
        out_shape=jax.ShapeDtypeStruct((n_tok, d), F32),
        scratch_shapes=[pltpu.VMEM((2 * 2 * tmc * tr, LANES), F32),
                        pltpu.SemaphoreType.DMA((2,))],
        compiler_params=_params(("arbitrary",)),
        name="moe_combine",
    )(pos, pos, y_tok, gates, ln_g.reshape(1, d), ln_b.reshape(1, d), ys)


def _tile_positions(route, off, t, tr):
    e = route[:, 0:2, :]
    hit = e[..., None] == jnp.arange(N_EXPERTS, dtype=jnp.int32)
    pos = (jnp.sum(jnp.where(hit, off, 0), axis=-1) + route[:, 2:4, :]) * tr
    pos = pos.transpose(1, 0, 2).reshape(2, -1, t)
    return pos.transpose(1, 0, 2).reshape(-1, 1, 2 * t)


def _moe_sparse(y_tok, route, gates, counts, w_gate, w_up, w_down, ln_g, ln_b, *, tmd, tmc):
    d = w_gate.shape[1]
    tr = d // LANES
    n_tok = y_tok.shape[0] // tr
    cnt = counts[:, 0].astype(jnp.int32)
    padded = (cnt + MOE_ROW_TILE - 1) // MOE_ROW_TILE * MOE_ROW_TILE
    off_end = jnp.cumsum(padded)
    off = off_end - padded
    n_sorted = 2 * n_tok + N_EXPERTS * MOE_ROW_TILE
    n_tiles = n_sorted // MOE_ROW_TILE
    n_used = (off_end[-1:] // MOE_ROW_TILE).astype(jnp.int32)
    tile_start = jnp.arange(n_tiles, dtype=jnp.int32) * MOE_ROW_TILE
    tile_expert = jnp.minimum(jnp.sum(tile_start[:, None] >= off_end[None, :], axis=1),
                              N_EXPERTS - 1).astype(jnp.int32)
    xs = _dispatch(y_tok, _tile_positions(route, off, tmd, tr), off_end.astype(jnp.int32),
                   n_sorted=n_sorted, tmd=tmd, tr=tr)
    ys = _experts(xs, tile_expert, n_used, w_gate, w_up, w_down)
    return _combine(ys, _tile_positions(route, off, tmc, tr), y_tok, gates, ln_g, ln_b,
                    tmc=tmc, tr=tr)


POOL_HALO = 32


def _inproj1_kernel(x_ref, w_ref, pw_ref, ps_ref, qt_ref, k_ref, vt_ref, d_ref, halo_ref,
                    *, sb_w):
    i = pl.program_id(1)
    x = x_ref[0].astype(BF16)
    tm = x.shape[0]
    q = jnp.dot(x, w_ref[:, 0:sb_w], preferred_element_type=F32) * ATTN_SCALE
    qt_ref[0] = q.T.astype(BF16)
    k_ref[0] = jnp.dot(x, w_ref[:, sb_w:2 * sb_w], preferred_element_type=F32).astype(BF16)
    v = jnp.dot(x, w_ref[:, 2 * sb_w:3 * sb_w], preferred_element_type=F32)
    vt_ref[0] = v.T.astype(BF16)
    u = jnp.dot(x, w_ref[:, 3 * sb_w:], preferred_element_type=F32)

    @pl.when(i == 0)
    def _():
        halo_ref[...] = jnp.zeros_like(halo_ref)

    tpos = i * tm + lax.broadcasted_iota(jnp.int32, (tm, 1), 0)
    outs = []
    for g, win in enumerate(POOL_WINDOWS):
        cs = slice(g * POOL_CH, (g + 1) * POOL_CH)
        cur = u[:, cs]
        run = jnp.concatenate([halo_ref[:, cs], cur], axis=0)
        span = 1
        while span < win:
            n = run.shape[0] - span
            run = run[span:span + n] + run[0:n]
            span *= 2
        off = POOL_HALO - (win - 1)
        wsum = run[off:off + tm]
        count = jnp.minimum(tpos + 1, win).astype(F32)
        pooled = wsum / count - cur
        mixed = jnp.dot(pooled.astype(BF16), pw_ref[g], preferred_element_type=F32)
        outs.append(mixed * ps_ref[:, cs])
    halo_ref[...] = u[tm - POOL_HALO:tm, :]
    d_ref[0] = jnp.concatenate(outs, axis=1).astype(d_ref.dtype)


def _inproj1(x, w_bf, pool_w_bf, pool_scale, *, sb_w, tm):
    bsz, n_pos, d = x.shape
    n_in = w_bf.shape[1]
    pool_w = n_in - 3 * sb_w
    n_g = len(POOL_WINDOWS)
    return pl.pallas_call(
        functools.partial(_inproj1_kernel, sb_w=sb_w),
        grid=(bsz, n_pos // tm),
        in_specs=[
            pl.BlockSpec((1, tm, d), lambda b, i: (b, i, 0)),
            pl.BlockSpec((d, n_in), lambda b, i: (0, 0)),
            pl.BlockSpec((n_g, POOL_CH, POOL_CH), lambda b, i: (0, 0, 0)),
            pl.BlockSpec((1, pool_w), lambda b, i: (0, 0)),
        ],
        out_specs=[
            pl.BlockSpec((1, sb_w, tm), lambda b, i: (b, 0, i)),
            pl.BlockSpec((1, tm, sb_w), lambda b, i: (b, i, 0)),
            pl.BlockSpec((1, sb_w, tm), lambda b, i: (b, 0, i)),
            pl.BlockSpec((1, tm, pool_w), lambda b, i: (b, i, 0)),
        ],
        out_shape=[
            jax.ShapeDtypeStruct((bsz, sb_w, n_pos), BF16),
            jax.ShapeDtypeStruct((bsz, n_pos, sb_w), BF16),
            jax.ShapeDtypeStruct((bsz, sb_w, n_pos), BF16),
            jax.ShapeDtypeStruct((bsz, n_pos, pool_w), BF16),
        ],
        scratch_shapes=[pltpu.VMEM((POOL_HALO, pool_w), F32)],
        compiler_params=_params(("parallel", "arbitrary")),
        name="inproj1",
    )(x, w_bf, pool_w_bf, pool_scale.astype(F32).reshape(1, pool_w))


def _sb_attn_kernel(qt_ref, k_ref, vt_ref, o_ref, za_ref, zb_ref, *, tq, tk):
    i = pl.program_id(2)
    qt = qt_ref[0]
    zeros = jnp.zeros((HEAD_DIM, tq), BF16)
    qpads = (jnp.concatenate([qt[:HEAD_DIM], zeros], axis=0),
             jnp.concatenate([zeros, qt[HEAD_DIM:]], axis=0))
    cb = min(SB_SCAN_ROWS, tk)
    upper = (lax.broadcasted_iota(jnp.int32, (cb, cb), 1)
             > lax.broadcasted_iota(jnp.int32, (cb, cb), 0)).astype(BF16)

    heads = range(2)

    def scores(j, buf):
        kj = k_ref[0, pl.ds(pl.multiple_of(j * tk, tk), tk), :]
        for hh in heads:
            buf[hh] = jnp.dot(kj, qpads[hh], preferred_element_type=F32)

    def consume(j, buf, carry, masked):
        k0 = pl.multiple_of(j * tk, tk)
        if masked:
            key = k0 + lax.broadcasted_iota(jnp.int32, (tk, tq), 0)
            qpos = i * tq + lax.broadcasted_iota(jnp.int32, (tk, tq), 1)
            keep = key < qpos
        zs = [buf[hh] for hh in heads]
        sps = [jnp.maximum(z, 0.0) + jnp.log(1.0 + jnp.exp2(jnp.abs(z) * (-LOG2E))) for z in zs]
        if masked:
            sps = [jnp.where(keep, sp, 0.0) for sp in sps]
        his = [sp.astype(BF16) for sp in sps]
        laters = [carry[hh][1] for hh in heads]
        pieces = [[None] * (tk // cb) for _ in heads]
        for c in reversed(range(tk // cb)):
            rs = slice(c * cb, (c + 1) * cb)
            sufs = [jnp.dot(upper, his[hh][rs], preferred_element_type=F32)
                    for hh in heads]
            for hh in heads:
                spc = sps[hh][rs]
                pieces[hh][c] = jnp.exp2((zs[hh][rs] - (spc + sufs[hh] + laters[hh])) * LOG2E)
                laters[hh] = laters[hh] + sufs[hh][0:1] + spc[0:1]
        out = []
        for hh in heads:
            a = jnp.concatenate(pieces[hh], axis=0)
            if masked:
                a = jnp.where(keep, a, 0.0)
            vj = vt_ref[0, hh * HEAD_DIM:(hh + 1) * HEAD_DIM, pl.ds(k0, tk)]
            acc = carry[hh][0] + jnp.dot(vj, a.astype(BF16), preferred_element_type=F32)
            out.append((acc, laters[hh]))
        return tuple(out)

    init = (jnp.zeros((HEAD_DIM, tq), F32), jnp.zeros((1, tq), F32))
    carry = (init, init)
    n_diag = tq // tk
    last = (i + 1) * n_diag - 1
    n_old = i * n_diag
    bufs = (za_ref, zb_ref)
    scores(last, bufs[0])
    for d in range(n_diag):
        nxt = last - d - 1 if d + 1 < n_diag else jnp.maximum(n_old - 1, 0)
        scores(nxt, bufs[(d + 1) % 2])
        carry = consume(last - d, bufs[d % 2], carry, True)
    first_old = bufs[n_diag % 2]
    spare = bufs[(n_diag + 1) % 2]

    def reach(c):
        return jnp.min(jnp.minimum(c[0][1], c[1][1]))

    def live(state):
        t, r, _ = state
        return jnp.logical_and(t < n_old, r < SB_UNDERFLOW)

    def older(state):
        t, _, c = state
        scores(n_old - 1 - t, spare)
        c = consume(n_old - 1 - t, spare, c, False)
        return t + 1, reach(c), c

    r0 = reach(carry)
    carry = lax.cond(jnp.logical_and(n_old > 0, r0 < SB_UNDERFLOW),
                     lambda c: consume(n_old - 1, first_old, c, False), lambda c: c, carry)
    _, _, carry = lax.while_loop(live, older, (jnp.int32(1), reach(carry), carry))
    o_ref[0] = jnp.concatenate([carry[0][0], carry[1][0]], axis=0).T.astype(o_ref.dtype)


def _sb_attention(qt, k, vt, *, tq, tk):
    bsz, sb_w, n_pos = qt.shape
    return pl.pallas_call(
        functools.partial(_sb_attn_kernel, tq=tq, tk=tk),
        grid=(bsz, sb_w // LANES, n_pos // tq),
        in_specs=[
            pl.BlockSpec((1, LANES, tq), lambda b, h, i: (b, h, i)),
            pl.BlockSpec((1, n_pos, LANES), lambda b, h, i: (b, 0, h)),
            pl.BlockSpec((1, LANES, n_pos), lambda b, h, i: (b, h, 0)),
        ],
        out_specs=pl.BlockSpec((1, tq, LANES), lambda b, h, i: (b, i, h)),
        out_shape=jax.ShapeDtypeStruct((bsz, n_pos, sb_w), BF16),
        scratch_shapes=[pltpu.VMEM((2, tk, tq), F32), pltpu.VMEM((2, tk, tq), F32)],
        compiler_params=_params(("parallel", "parallel", "arbitrary")),
        name="sb_attention",
    )(qt, k, vt)


def _rope_tables(n_pos):
    inv = ROPE_THETA ** (-jnp.arange(0, HEAD_DIM, 2, dtype=F32) / HEAD_DIM)
    ang = jnp.arange(n_pos, dtype=F32)[:, None] * inv[None, :]
    ang = jnp.concatenate([ang, ang], axis=-1)
    return jnp.cos(ang), jnp.sin(ang)


def _lambda_init(layer_idx):
    return 0.8 - 0.6 * math.exp(-0.3 * layer_idx)


def kernel(x, router_w, router_bias, l0_in_proj, l0_conv_w, l0_conv_b, l0_conv_ln_g, l0_conv_ln_b, l0_lambda_q1, l0_lambda_k1, l0_lambda_q2, l0_lambda_k2, l0_subln_g, l0_out_proj, l0_ln_mix_g, l0_ln_mix_b, l0_w_gate, l0_w_up, l0_w_down, l0_ln_ffn_g, l0_ln_ffn_b, l1_in_proj, l1_pool_w, l1_pool_scale, l1_out_proj, l1_ln_mix_g, l1_ln_mix_b, l1_w_gate, l1_w_up, l1_w_down, l1_ln_ffn_g, l1_ln_ffn_b):
    bsz, n_pos, d = x.shape
    n_tok = bsz * n_pos
    x = x.astype(F32)

    conv_ch = l0_conv_w.shape[1]
    dv = l0_subln_g.shape[0]
    n_diff = (l0_out_proj.shape[0] - conv_ch) // dv
    qk_w = n_diff * 2 * HEAD_DIM
    v_w = n_diff * dv

    tm = min(512, n_pos)
    tq_diff = min(512, n_pos)
    tq_sb = min(512, n_pos)
    tk_sb = min(256, n_pos)
    ts = min(256, n_pos)
    tmc = min(512, n_pos)
    tmd = min(2048, n_tok)
    tm_out = min(1024, n_tok)

    cos, sin = _rope_tables(n_pos)
    sign = jnp.where(jnp.arange(HEAD_DIM) < HEAD_DIM // 2, -1.0, 1.0).astype(F32)
    cos128 = jnp.concatenate([cos, cos], axis=-1)
    sins128 = jnp.concatenate([sin * sign, sin * sign], axis=-1)
    cost = cos.T
    sint = sin.T

    glu, qt, k, vt = _inproj0(x, l0_in_proj.astype(BF16), cos128, sins128, cost, sint,
                              conv_ch=conv_ch, qk_w=qk_w, v_w=v_w, tm=tm)
    a = _conv_module(glu, l0_conv_w.astype(F32), l0_conv_b.astype(F32),
                     l0_conv_ln_g.astype(F32), l0_conv_ln_b.astype(F32), ts=ts)
    lam_init = _lambda_init(0)
    lam = (jnp.exp(jnp.sum(l0_lambda_q1.astype(F32) * l0_lambda_k1.astype(F32)))
           - jnp.exp(jnp.sum(l0_lambda_q2.astype(F32) * l0_lambda_k2.astype(F32))) + lam_init)
    o = _diff_attention(qt, k, vt, lam, l0_subln_g, n_heads=n_diff, lam_init=lam_init,
                        tq=tq_diff, tk=tq_diff)
    xf = x.reshape(n_tok, d)
    y, route, gates, counts = _outproj_ln_route(
        a.reshape(n_tok, conv_ch), o.reshape(n_tok, v_w), l0_out_proj, xf,
        l0_ln_mix_g.astype(F32), l0_ln_mix_b.astype(F32), router_w, router_bias, tm=tm_out)
    xf = _moe_sparse(y, route, gates, counts, l0_w_gate, l0_w_up, l0_w_down,
                     l0_ln_ffn_g.astype(F32), l0_ln_ffn_b.astype(F32), tmd=tmd, tmc=tmc)

    sb_w = l1_in_proj.shape[1] - len(POOL_WINDOWS) * POOL_CH
    sb_w = sb_w // 3
    qt, k, vt, dpool = _inproj1(xf.reshape(bsz, n_pos, d), l1_in_proj.astype(BF16),
                                l1_pool_w.astype(BF16), l1_pool_scale, sb_w=sb_w, tm=tm)
    o = _sb_attention(qt, k, vt, tq=tq_sb, tk=tk_sb)
    y, route, gates, counts = _outproj_ln_route(
        o.reshape(n_tok, sb_w), dpool.reshape(n_tok, -1), l1_out_proj, xf,
        l1_ln_mix_g.astype(F32), l1_ln_mix_b.astype(F32), router_w, router_bias, tm=tm_out)
    xf = _moe_sparse(y, route, gates, counts, l1_w_gate, l1_w_up, l1_w_down,
                     l1_ln_ffn_g.astype(F32), l1_ln_ffn_b.astype(F32), tmd=tmd, tmc=tmc)
    return xf.reshape(bsz, n_pos, d)
```

```python
import functools
import math

import jax
import jax.numpy as jnp
from jax import lax
from jax.experimental import pallas as pl
from jax.experimental.pallas import tpu as pltpu

F32 = jnp.float32
BF16 = jnp.bfloat16

DEPTH = 2
HEAD_DIM = 64
ROPE_THETA = 10000.0
LN_EPS = 1e-5
RMS_EPS = 1e-5
DEEPNORM_ALPHA = (2 * DEPTH) ** 0.25
CONV_WIDTH = 31
POOL_WINDOWS = (2, 4, 8, 16)
POOL_CH = 128
N_EXPERTS = 16
N_GROUPS = 4
EXPERTS_PER_GROUP = N_EXPERTS // N_GROUPS
ATTN_SCALE = HEAD_DIM ** -0.5
LOG2E = math.log2(math.e)

LANES = 128
SUBLANES = 8
MIB = 1024 * 1024
VMEM_LIMIT = 52 * MIB

NEG_BIG = -1e30
SB_SCAN_ROWS = 256
SB_UNDERFLOW = 110.0


def _params(semantics, vmem=VMEM_LIMIT):
    return pltpu.CompilerParams(dimension_semantics=semantics, vmem_limit_bytes=vmem)


def _layer_norm_rows(h, g, b):
    mu = jnp.mean(h, axis=-1, keepdims=True)
    d = h - mu
    var = jnp.mean(d * d, axis=-1, keepdims=True)
    return d * lax.rsqrt(var + LN_EPS) * g + b


def _load_token_tiles(ref, row0, n, d):
    k = d // LANES
    return jnp.concatenate([ref[pl.ds(row0 + c, n, stride=k), :] for c in range(k)], axis=1)


def _store_token_tiles(ref, row0, val):
    n, d = val.shape
    k = d // LANES
    for c in range(k):
        ref[pl.ds(row0 + c, n, stride=k), :] = val[:, c * LANES:(c + 1) * LANES]


def _inproj0_kernel(x_ref, w_ref, cos_ref, sins_ref, cost_ref, sint_ref,
                    glu_ref, qt_ref, k_ref, vt_ref, *, conv_ch, qk_w):
    x = x_ref[0].astype(BF16)
    c0, c1, c2, c3 = conv_ch, 2 * conv_ch, 2 * conv_ch + qk_w, 2 * conv_ch + 2 * qk_w

    a_val = jnp.dot(x, w_ref[:, 0:c0], preferred_element_type=F32)
    a_gate = jnp.dot(x, w_ref[:, c0:c1], preferred_element_type=F32)
    glu_ref[0] = a_val * jax.nn.sigmoid(a_gate)

    dq = jnp.dot(x, w_ref[:, c1:c2], preferred_element_type=F32)
    dqt = dq.T
    cost = cost_ref[...]
    sint = sint_ref[...]
    half = HEAD_DIM // 2
    pieces = []
    for blk in range(qk_w // HEAD_DIM):
        u = dqt[blk * HEAD_DIM:(blk + 1) * HEAD_DIM]
        rot = jnp.concatenate([-u[half:], u[:half]], axis=0)
        pieces.append((u * cost + rot * sint) * (ATTN_SCALE * LOG2E))
    qt_ref[0] = jnp.concatenate(pieces, axis=0).astype(BF16)

    dk = jnp.dot(x, w_ref[:, c2:c3], preferred_element_type=F32)
    cos = cos_ref[...]
    sins = sins_ref[...]
    lane = lax.broadcasted_iota(jnp.int32, cos.shape, 1)
    first = (lane & (HEAD_DIM - 1)) < half
    kparts = []
    for h in range(qk_w // LANES):
        u = dk[:, h * LANES:(h + 1) * LANES]
        rot = jnp.where(first, pltpu.roll(u, LANES - half, 1), pltpu.roll(u, half, 1))
        kparts.append(u * cos + rot * sins)
    k_ref[0] = jnp.concatenate(kparts, axis=1).astype(BF16)

    dv = jnp.dot(x, w_ref[:, c3:], preferred_element_type=F32)
    vt_ref[0] = dv.T.astype(BF16)


def _inproj0(x, w_bf, cos128, sins128, cost, sint, *, conv_ch, qk_w, v_w, tm):
    bsz, n_pos, d = x.shape
    n_in = w_bf.shape[1]
    kern = functools.partial(_inproj0_kernel, conv_ch=conv_ch, qk_w=qk_w)
    return pl.pallas_call(
        kern,
        grid=(bsz, n_pos // tm),
        in_specs=[
            pl.BlockSpec((1, tm, d), lambda b, i: (b, i, 0)),
            pl.BlockSpec((d, n_in), lambda b, i: (0, 0)),
            pl.BlockSpec((tm, LANES), lambda b, i: (i, 0)),
            pl.BlockSpec((tm, LANES), lambda b, i: (i, 0)),
            pl.BlockSpec((HEAD_DIM, tm), lambda b, i: (0, i)),
            pl.BlockSpec((HEAD_DIM, tm), lambda b, i: (0, i)),
        ],
        out_specs=[
            pl.BlockSpec((1, tm, conv_ch), lambda b, i: (b, i, 0)),
            pl.BlockSpec((1, qk_w, tm), lambda b, i: (b, 0, i)),
            pl.BlockSpec((1, tm, qk_w), lambda b, i: (b, i, 0)),
            pl.BlockSpec((1, v_w, tm), lambda b, i: (b, 0, i)),
        ],
        out_shape=[
            jax.ShapeDtypeStruct((bsz, n_pos, conv_ch), F32),
            jax.ShapeDtypeStruct((bsz, qk_w, n_pos), BF16),
            jax.ShapeDtypeStruct((bsz, n_pos, qk_w), BF16),
            jax.ShapeDtypeStruct((bsz, v_w, n_pos), BF16),
        ],
        compiler_params=_params(("parallel", "arbitrary")),
        name="inproj0",
    )(x, w_bf, cos128, sins128, cost, sint)


CONV_HALO = 32
CONV_ROW_BLOCK = 64


def _conv_kernel(a_ref, cw_ref, cb_ref, g_ref, b_ref, o_ref, win_ref, acc_ref, *, ts):
    i = pl.program_id(1)
    t0 = pl.multiple_of(i * ts, ts)
    h0 = pl.multiple_of(jnp.maximum(t0 - CONV_HALO, 0), CONV_HALO)
    n_ch = a_ref.shape[2]
    n_shift = ts + CONV_HALO - SUBLANES
    for c in range(n_ch // LANES):
        cs = slice(c * LANES, (c + 1) * LANES)
        cur = a_ref[0, pl.ds(t0, ts), cs]
        halo = a_ref[0, pl.ds(h0, CONV_HALO), cs]
        halo = jnp.where(i > 0, halo, 0.0)
        xw = jnp.concatenate([halo, cur], axis=0)
        win_ref[c, 0] = xw
        for r in range(1, SUBLANES):
            win_ref[c, r, 0:n_shift] = xw[r:r + n_shift]
        def row_block(rb, carry, c=c, cs=cs, full=xw.shape[0]):
            base = pl.multiple_of(rb * CONV_ROW_BLOCK, CONV_ROW_BLOCK)
            acc = jnp.zeros((CONV_ROW_BLOCK, LANES), F32) + cb_ref[:, cs]
            for r in range(SUBLANES):
                rows = full if r == 0 else n_shift
                for a in range(CONV_HALO // SUBLANES + 1):
                    w = SUBLANES * a + r - (CONV_HALO - CONV_WIDTH + 1)
                    if 0 <= w < CONV_WIDTH and SUBLANES * a + ts <= rows:
                        tap = win_ref[c, r, pl.ds(base + SUBLANES * a, CONV_ROW_BLOCK)]
                        acc = acc + tap * cw_ref[w:w + 1, cs]
            acc_ref[pl.ds(base, CONV_ROW_BLOCK), cs] = acc
            return carry
        lax.fori_loop(0, ts // CONV_ROW_BLOCK, row_block, 0)
    y = _layer_norm_rows(acc_ref[...], g_ref[...], b_ref[...])
    o_ref[0] = (y * jax.nn.sigmoid(y)).astype(o_ref.dtype)


def _conv_module(a, conv_w, conv_b, ln_g, ln_b, *, ts):
    bsz, n_pos, n_ch = a.shape
    return pl.pallas_call(
        functools.partial(_conv_kernel, ts=ts),
        grid=(bsz, n_pos // ts),
        in_specs=[
            pl.BlockSpec((1, n_pos, n_ch), lambda b, i: (b, 0, 0)),
            pl.BlockSpec((CONV_WIDTH, n_ch), lambda b, i: (0, 0)),
            pl.BlockSpec((1, n_ch), lambda b, i: (0, 0)),
            pl.BlockSpec((1, n_ch), lambda b, i: (0, 0)),
            pl.BlockSpec((1, n_ch), lambda b, i: (0, 0)),
        ],
        out_specs=pl.BlockSpec((1, ts, n_ch), lambda b, i: (b, i, 0)),
        out_shape=jax.ShapeDtypeStruct((bsz, n_pos, n_ch), BF16),
        scratch_shapes=[pltpu.VMEM((n_ch // LANES, SUBLANES, ts + CONV_HALO, LANES), F32),
                        pltpu.VMEM((ts, n_ch), F32)],
        compiler_params=_params(("parallel", "arbitrary")),
        name="conv_module",
    )(a, conv_w, conv_b.reshape(1, n_ch), ln_g.reshape(1, n_ch), ln_b.reshape(1, n_ch))


def _diff_attn_kernel(lam_ref, qt_ref, k_ref, vt_ref, g_ref, o_ref, sa_ref, sb_ref,
                      *, tq, tk, out_scale):
    i = pl.program_id(2)
    lam = lam_ref[0, 0]
    qt = qt_ref[0]
    zeros = jnp.zeros((HEAD_DIM, tq), BF16)
    qa = jnp.concatenate([qt[:HEAD_DIM], zeros], axis=0)
    qb = jnp.concatenate([zeros, qt[HEAD_DIM:]], axis=0)
    dv = vt_ref.shape[1]

    def online(s, vj, m, l, acc):
        m_new = jnp.maximum(m, jnp.max(s, axis=0, keepdims=True))
        alpha = jnp.exp2(m - m_new)
        p = jnp.exp2(s - m_new)
        l_new = alpha * l + jnp.sum(p, axis=0, keepdims=True)
        acc_new = alpha * acc + jnp.dot(vj, p.astype(BF16), preferred_element_type=F32)
        return m_new, l_new, acc_new

    def scores(j, buf):
        kj = k_ref[0, pl.ds(pl.multiple_of(j * tk, tk), tk), :]
        buf[0] = jnp.dot(kj, qa, preferred_element_type=F32)
        buf[1] = jnp.dot(kj, qb, preferred_element_type=F32)

    def consume(j, buf, carry, masked):
        m1, l1, acc1, m2, l2, acc2 = carry
        k0 = pl.multiple_of(j * tk, tk)
        vj = vt_ref[0, :, pl.ds(k0, tk)]
        s1, s2 = buf[0], buf[1]
        if masked:
            key = k0 + lax.broadcasted_iota(jnp.int32, (tk, tq), 0)
            qpos = i * tq + lax.broadcasted_iota(jnp.int32, (tk, tq), 1)
            keep = key <= qpos
            s1 = jnp.where(keep, s1, NEG_BIG)
            s2 = jnp.where(keep, s2, NEG_BIG)
        m1, l1, acc1 = online(s1, vj, m1, l1, acc1)
        m2, l2, acc2 = online(s2, vj, m2, l2, acc2)
        return m1, l1, acc1, m2, l2, acc2

    row = jnp.full((1, tq), NEG_BIG, F32)
    zrow = jnp.zeros((1, tq), F32)
    zacc = jnp.zeros((dv, tq), F32)
    carry = (row, zrow, zacc, row, zrow, zacc)

    scores(0, sa_ref)

    def pair(u, c):
        scores(2 * u + 1, sb_ref)
        c = consume(2 * u, sa_ref, c, False)
        scores(2 * u + 2, sa_ref)
        return consume(2 * u + 1, sb_ref, c, False)

    carry = lax.fori_loop(0, lax.shift_right_logical(i, 1), pair, carry)

    def odd_tail(c):
        scores(i, sb_ref)
        c = consume(i - 1, sa_ref, c, False)
        return consume(i, sb_ref, c, True)

    def even_tail(c):
        return consume(i, sa_ref, c, True)

    m1, l1, acc1, m2, l2, acc2 = lax.cond((i & 1) == 1, odd_tail, even_tail, carry)

    o = acc1 / l1 - lam * (acc2 / l2)
    o = o * lax.rsqrt(jnp.mean(o * o, axis=0, keepdims=True) + RMS_EPS)
    o = o * g_ref[...] * out_scale
    o_ref[0] = o.T.astype(o_ref.dtype)


def _diff_attention(qt, k, vt, lam, subln_g, *, n_heads, lam_init, tq, tk):
    bsz, _, n_pos = qt.shape
    dv = vt.shape[1] // n_heads
    kern = functools.partial(_diff_attn_kernel, tq=tq, tk=tk, out_scale=1.0 - lam_init)
    return pl.pallas_call(
        kern,
        grid=(bsz, n_heads, n_pos // tq),
        in_specs=[
            pl.BlockSpec(memory_space=pltpu.SMEM),
            pl.BlockSpec((1, 2 * HEAD_DIM, tq), lambda b, h, i: (b, h, i)),
            pl.BlockSpec((1, n_pos, 2 * HEAD_DIM), lambda b, h, i: (b, 0, h)),
            pl.BlockSpec((1, dv, n_pos), lambda b, h, i: (b, h, 0)),
            pl.BlockSpec((dv, 1), lambda b, h, i: (0, 0)),
        ],
        out_specs=pl.BlockSpec((1, tq, dv), lambda b, h, i: (b, i, h)),
        out_shape=jax.ShapeDtypeStruct((bsz, n_pos, n_heads * dv), BF16),
        scratch_shapes=[pltpu.VMEM((2, tk, tq), F32), pltpu.VMEM((2, tk, tq), F32)],
        compiler_params=_params(("parallel", "parallel", "arbitrary")),
        name="diff_attention",
    )(lam.reshape(1, 1), qt, k, vt, subln_g.astype(F32).reshape(dv, 1))


def _masked_top2(vals, member, row):
    big = N_EXPERTS
    v = jnp.where(member, vals, -jnp.inf)
    v1 = jnp.max(v, axis=0, keepdims=True)
    i1 = jnp.min(jnp.where(v == v1, row, big), axis=0, keepdims=True)
    v = jnp.where(row == i1, -jnp.inf, v)
    v2 = jnp.max(v, axis=0, keepdims=True)
    i2 = jnp.min(jnp.where(v == v2, row, big), axis=0, keepdims=True)
    return v1, i1, v2, i2


def _top2_route(aff, sel):
    row = lax.broadcasted_iota(jnp.int32, sel.shape, 0)
    group = lax.shift_right_logical(row, EXPERTS_PER_GROUP.bit_length() - 1)
    gbest = None
    for g in range(N_GROUPS):
        v1, _, v2, _ = _masked_top2(sel, group == g, row)
        score = v1 + v2
        if gbest is None:
            gbest, gval = jnp.zeros_like(score, dtype=jnp.int32), score
        else:
            better = score > gval
            gbest = jnp.where(better, g, gbest)
            gval = jnp.where(better, score, gval)
    _, e1, _, e2 = _masked_top2(sel, group == gbest, row)
    a1 = jnp.sum(jnp.where(row == e1, aff, 0.0), axis=0, keepdims=True)
    a2 = jnp.sum(jnp.where(row == e2, aff, 0.0), axis=0, keepdims=True)
    denom = a1 + a2
    return e1, e2, a1 / denom, a2 / denom


ROUTE_ROWS = SUBLANES


def _outproj_kernel(pa_ref, pb_ref, wa_ref, wb_ref, x_ref, g_ref, b_ref, rw_ref, rb_ref,
                    y_ref, route_ref, gate_ref, count_ref, seen_ref):
    i = pl.program_id(0)

    @pl.when(i == 0)
    def _():
        seen_ref[...] = jnp.zeros_like(seen_ref)

    mix = jnp.dot(pa_ref[...], wa_ref[...], preferred_element_type=F32)
    mix = mix + jnp.dot(pb_ref[...], wb_ref[...], preferred_element_type=F32)
    y = _layer_norm_rows(DEEPNORM_ALPHA * x_ref[...] + mix, g_ref[...], b_ref[...])
    _store_token_tiles(y_ref, 0, y)
    tm = y.shape[0]
    yh = y.astype(BF16)
    yl = (y - yh.astype(F32)).astype(BF16)
    nt = (((1,), (1,)), ((), ()))
    both = lax.dot_general(rw_ref[...], yh, nt, preferred_element_type=F32)
    logits = (both[:N_EXPERTS] + both[N_EXPERTS:]
              + lax.dot_general(rw_ref[0:N_EXPERTS], yl, nt, preferred_element_type=F32))
    aff = jax.nn.sigmoid(logits)
    e1, e2, g1, g2 = _top2_route(aff, aff + rb_ref[...])

    erow = lax.broadcasted_iota(jnp.int32, (N_EXPERTS, tm), 0)
    hit = jnp.where((erow == e1) | (erow == e2), 1.0, 0.0)
    upto = (lax.broadcasted_iota(jnp.int32, (tm, tm), 0)
            <= lax.broadcasted_iota(jnp.int32, (tm, tm), 1)).astype(BF16)
    incl = jnp.dot(hit.astype(BF16), upto, preferred_element_type=F32)
    before = seen_ref[:, 0:1] + incl - hit
    r1 = jnp.sum(jnp.where(erow == e1, before, 0.0), axis=0, keepdims=True).astype(jnp.int32)
    r2 = jnp.sum(jnp.where(erow == e2, before, 0.0), axis=0, keepdims=True).astype(jnp.int32)
    seen_ref[...] = seen_ref[...] + jnp.sum(hit, axis=1, keepdims=True)
    count_ref[...] = seen_ref[...]

    rrow = lax.broadcasted_iota(jnp.int32, (ROUTE_ROWS, tm), 0)
    route_ref[0] = jnp.where(rrow == 0, e1, jnp.where(rrow == 1, e2,
                             jnp.where(rrow == 2, r1, jnp.where(rrow == 3, r2, 0))))
    grow = lax.broadcasted_iota(jnp.int32, (LANES, tm), 0)
    gate_ref[...] = jnp.where(grow == 0, g1, jnp.where(grow == 1, g2, 0.0)).T


def _outproj_ln_route(pa, pb, w_out, x, ln_g, ln_b, router_w, router_bias, *, tm):
    n_tok, d = x.shape
    ka, kb = pa.shape[1], pb.shape[1]
    wa = w_out[:ka].astype(BF16)
    wb = w_out[ka:].astype(BF16)
    rw = router_w.astype(F32).T
    rw_hi = rw.astype(BF16)
    rw_lo = (rw - rw_hi.astype(F32)).astype(BF16)
    rows = lambda i: (i, 0)
    fixed = lambda i: (0, 0)
    return pl.pallas_call(
        _outproj_kernel,
        grid=(n_tok // tm,),
        in_specs=[
            pl.BlockSpec((tm, ka), rows),
            pl.BlockSpec((tm, kb), rows),
            pl.BlockSpec((ka, d), fixed),
            pl.BlockSpec((kb, d), fixed),
            pl.BlockSpec((tm, d), rows),
            pl.BlockSpec((1, d), fixed),
            pl.BlockSpec((1, d), fixed),
            pl.BlockSpec((2 * N_EXPERTS, d), fixed),
            pl.BlockSpec((N_EXPERTS, 1), fixed),
        ],
        out_specs=[
            pl.BlockSpec((tm * d // LANES, LANES), rows),
            pl.BlockSpec((1, ROUTE_ROWS, tm), lambda i: (i, 0, 0)),
            pl.BlockSpec((tm, LANES), rows),
            pl.BlockSpec((N_EXPERTS, LANES), fixed),
        ],
        out_shape=[
            jax.ShapeDtypeStruct((n_tok * d // LANES, LANES), F32),
            jax.ShapeDtypeStruct((n_tok // tm, ROUTE_ROWS, tm), jnp.int32),
            jax.ShapeDtypeStruct((n_tok, LANES), F32),
            jax.ShapeDtypeStruct((N_EXPERTS, LANES), F32),
        ],
        scratch_shapes=[pltpu.VMEM((N_EXPERTS, LANES), F32)],
        compiler_params=_params(("arbitrary",)),
        name="outproj_ln_route",
    )(pa, pb, wa, wb, x, ln_g.reshape(1, d), ln_b.reshape(1, d),
      jnp.concatenate([rw_hi, rw_lo], axis=0), router_bias.astype(F32).reshape(N_EXPERTS, 1))


MOE_ROW_TILE = 512
DMA_UNROLL = 8


def _row_copy_wait(src_hbm, dst, sem, n_rows):
    pltpu.make_async_copy(src_hbm.at[pl.ds(0, n_rows)], dst, sem).wait()


def _dispatch_kernel(offend_ref, pos_ref, y_ref, xs_hbm, zero_ref, sem, zsem, *, tmd, tr):
    i = pl.program_id(0)
    tile_rows = MOE_ROW_TILE * tr

    @pl.when(i == 0)
    def _():
        zero_ref[...] = jnp.zeros_like(zero_ref)
        def tail_copy(e):
            start = pl.multiple_of((offend_ref[e] - MOE_ROW_TILE) * tr, tile_rows)
            return pltpu.make_async_copy(zero_ref, xs_hbm.at[pl.ds(start, tile_rows)], zsem)
        def nonempty(e):
            return offend_ref[e] > (offend_ref[e - 1] if e > 0 else 0)
        for e in range(N_EXPERTS):
            @pl.when(nonempty(e))
            def _():
                tail_copy(e).start()
        for e in range(N_EXPERTS):
            @pl.when(nonempty(e))
            def _():
                tail_copy(e).wait()
        def unused_copy(j):
            return pltpu.make_async_copy(
                zero_ref, xs_hbm.at[pl.ds(pl.multiple_of(j * tile_rows, tile_rows), tile_rows)],
                zsem)
        first_unused = lax.shift_right_logical(offend_ref[N_EXPERTS - 1],
                                               MOE_ROW_TILE.bit_length() - 1)
        n_tiles = xs_hbm.shape[0] // tile_rows
        lax.fori_loop(first_unused, n_tiles, lambda j, c: (unused_copy(j).start(), c)[1], 0)
        lax.fori_loop(first_unused, n_tiles, lambda j, c: (unused_copy(j).wait(), c)[1], 0)

    def issue(g, c):
        r0 = g * DMA_UNROLL
        ps = [[pos_ref[0, 0, k * tmd + r0 + j] for k in range(2)] for j in range(DMA_UNROLL)]
        for j in range(DMA_UNROLL):
            src = y_ref.at[pl.ds(pl.multiple_of((r0 + j) * tr, tr), tr)]
            for k in range(2):
                dst = xs_hbm.at[pl.ds(pl.multiple_of(ps[j][k], tr), tr)]
                pltpu.make_async_copy(src, dst, sem).start(priority=k)
        return c

    lax.fori_loop(0, tmd // DMA_UNROLL, issue, 0)
    for k in range(2):
        pltpu.make_async_copy(y_ref, xs_hbm.at[pl.ds(0, tmd * tr)], sem).wait()


def _dispatch(y_tok, pos, off_end, *, n_sorted, tmd, tr):
    n_tok = y_tok.shape[0] // tr
    return pl.pallas_call(
        functools.partial(_dispatch_kernel, tmd=tmd, tr=tr),
        grid_spec=pltpu.PrefetchScalarGridSpec(
            num_scalar_prefetch=1,
            grid=(n_tok // tmd,),
            in_specs=[
                pl.BlockSpec((1, 1, 2 * tmd), lambda i, off: (i, 0, 0), memory_space=pltpu.SMEM),
                pl.BlockSpec((tmd * tr, LANES), lambda i, off: (i, 0)),
            ],
            out_specs=pl.BlockSpec(memory_space=pl.ANY),
            scratch_shapes=[pltpu.VMEM((MOE_ROW_TILE * tr, LANES), F32),
                            pltpu.SemaphoreType.DMA, pltpu.SemaphoreType.DMA],
        ),
        out_shape=jax.ShapeDtypeStruct((n_sorted * tr, LANES), F32),
        compiler_params=_params(("arbitrary",)),
        name="moe_dispatch",
    )(off_end, pos, y_tok)


def _experts_kernel(texp_ref, nused_ref, xs_ref, wg_ref, wu_ref, wd_ref, ys_ref,
                    wg_bf, wu_bf, wd_bf):
    i = pl.program_id(0)
    used = i < nused_ref[0]

    @pl.when(jnp.logical_not(used))
    def _():
        ys_ref[...] = jnp.zeros_like(ys_ref)

    new_expert = jnp.logical_or(i == 0, texp_ref[i] != texp_ref[jnp.maximum(i - 1, 0)])

    @pl.when(jnp.logical_and(used, new_expert))
    def _():
        wg_bf[...] = wg_ref[0].astype(BF16)
        wu_bf[...] = wu_ref[0].astype(BF16)
        wd_bf[...] = wd_ref[0].astype(BF16)

    @pl.when(used)
    def _():
        d = wg_ref.shape[1]
        x = _load_token_tiles(xs_ref, 0, MOE_ROW_TILE, d).astype(BF16)
        hg = jnp.dot(x, wg_bf[...], preferred_element_type=F32)
        hu = jnp.dot(x, wu_bf[...], preferred_element_type=F32)
        h = (hg * jax.nn.sigmoid(hg)) * hu
        _store_token_tiles(ys_ref, 0, jnp.dot(h.astype(BF16), wd_bf[...],
                                              preferred_element_type=F32))


def _experts(xs, tile_expert, n_used, w_gate, w_up, w_down):
    _, d, dff = w_gate.shape
    tile_rows = MOE_ROW_TILE * d // LANES
    tile = lambda i, te, nu: (jnp.minimum(i, nu[0] - 1), 0)
    every = lambda i, te, nu: (i, 0)
    wsel = lambda i, te, nu: (te[jnp.minimum(i, nu[0] - 1)], 0, 0)
    return pl.pallas_call(
        _experts_kernel,
        grid_spec=pltpu.PrefetchScalarGridSpec(
            num_scalar_prefetch=2,
            grid=(xs.shape[0] // tile_rows,),
            in_specs=[
                pl.BlockSpec((tile_rows, LANES), tile),
                pl.BlockSpec((1, d, dff), wsel),
                pl.BlockSpec((1, d, dff), wsel),
                pl.BlockSpec((1, dff, d), wsel),
            ],
            out_specs=pl.BlockSpec((tile_rows, LANES), every),
            scratch_shapes=[pltpu.VMEM((d, dff), BF16), pltpu.VMEM((d, dff), BF16),
                            pltpu.VMEM((dff, d), BF16)],
        ),
        out_shape=jax.ShapeDtypeStruct(xs.shape, F32),
        compiler_params=_params(("arbitrary",)),
        name="moe_experts",
    )(tile_expert, n_used, xs, w_gate, w_up, w_down)


def _combine_kernel(pos_ref, posn_ref, y_ref, gate_ref, g_ref, b_ref, ys_hbm, o_ref,
                    buf, sem, *, tmc, tr):
    i = pl.program_id(0)
    n = pl.num_programs(0)
    d = o_ref.shape[1]
    half = tmc * tr
    slot_rows = 2 * half

    def gather(p_ref, slot):
        base = slot * slot_rows
        def issue(g, c):
            r0 = g * DMA_UNROLL
            ps = [[p_ref[0, 0, k * tmc + r0 + j] for k in range(2)] for j in range(DMA_UNROLL)]
            for j in range(DMA_UNROLL):
                for k in range(2):
                    src = ys_hbm.at[pl.ds(pl.multiple_of(ps[j][k], tr), tr)]
                    dst = buf.at[pl.ds(pl.multiple_of(base + k * half + (r0 + j) * tr, tr), tr)]
                    pltpu.make_async_copy(src, dst, sem.at[slot]).start(priority=k)
            return c
        lax.fori_loop(0, tmc // DMA_UNROLL, issue, 0)

    @pl.when(i == 0)
    def _():
        gather(pos_ref, 0)

    @pl.when(i + 1 < n)
    def _():
        gather(posn_ref, (i + 1) % 2)

    slot = i % 2
    base = pl.multiple_of(slot * slot_rows, slot_rows)
    _row_copy_wait(ys_hbm, buf.at[pl.ds(base, slot_rows)], sem.at[slot], slot_rows)
    gate = gate_ref[...]
    ffn = (gate[:, 0:1] * _load_token_tiles(buf, base, tmc, d)
           + gate[:, 1:2] * _load_token_tiles(buf, base + half, tmc, d))
    y = _load_token_tiles(y_ref, 0, tmc, d)
    o_ref[...] = _layer_norm_rows(DEEPNORM_ALPHA * y + ffn, g_ref[...], b_ref[...])


def _combine(ys, pos, y_tok, gates, ln_g, ln_b, *, tmc, tr):
    d = tr * LANES
    n_tok = y_tok.shape[0] // tr
    n = n_tok // tmc
    rows = lambda i: (i, 0)
    fixed = lambda i: (0, 0)
    return pl.pallas_call(
        functools.partial(_combine_kernel, tmc=tmc, tr=tr),
        grid=(n,),
        in_specs=[
            pl.BlockSpec((1, 1, 2 * tmc), lambda i: (i, 0, 0), memory_space=pltpu.SMEM),
            pl.BlockSpec((1, 1, 2 * tmc), lambda i: (jnp.minimum(i + 1, n - 1), 0, 0),
                         memory_space=pltpu.SMEM),
            pl.BlockSpec((tmc * tr, LANES), rows),
            pl.BlockSpec((tmc, LANES), rows),
            pl.BlockSpec((1, d), fixed),
            pl.BlockSpec((1, d), fixed),
            pl.BlockSpec(memory_space=pl.ANY),
        ],
        out_specs=pl.BlockSpec((tmc, d), rows),
        out_shape=jax.ShapeDtypeStruct((n_tok, d), F32),
        scratch_shapes=[pltpu.VMEM((2 * 2 * tmc * tr, LANES), F32),
                        pltpu.SemaphoreType.DMA((2,))],
        compiler_params=_params(("arbitrary",)),
        name="moe_combine",
    )(pos, pos, y_tok, gates, ln_g.reshape(1, d), ln_b.reshape(1, d), ys)


def _tile_positions(route, off, t, tr):
    e = route[:, 0:2, :]
    hit = e[..., None] == jnp.arange(N_EXPERTS, dtype=jnp.int32)
    pos = (jnp.sum(jnp.where(hit, off, 0), axis=-1) + route[:, 2:4, :]) * tr
    pos = pos.transpose(1, 0, 2).reshape(2, -1, t)
    return pos.transpose(1, 0, 2).reshape(-1, 1, 2 * t)


def _moe_sparse(y_tok, route, gates, counts, w_gate, w_up, w_down, ln_g, ln_b, *, tmd, tmc):
    d = w_gate.shape[1]
    tr = d // LANES
    n_tok = y_tok.shape[0] // tr
    cnt = counts[:, 0].astype(jnp.int32)
    padded = (cnt + MOE_ROW_TILE - 1) // MOE_ROW_TILE * MOE_ROW_TILE
    off_end = jnp.cumsum(padded)
    off = off_end - padded
    n_sorted = 2 * n_tok + N_EXPERTS * MOE_ROW_TILE
    n_tiles = n_sorted // MOE_ROW_TILE
    n_used = (off_end[-1:] // MOE_ROW_TILE).astype(jnp.int32)
    tile_start = jnp.arange(n_tiles, dtype=jnp.int32) * MOE_ROW_TILE
    tile_expert = jnp.minimum(jnp.sum(tile_start[:, None] >= off_end[None, :], axis=1),
                              N_EXPERTS - 1).astype(jnp.int32)
    xs = _dispatch(y_tok, _tile_positions(route, off, tmd, tr), off_end.astype(jnp.int32),
                   n_sorted=n_sorted, tmd=tmd, tr=tr)
    ys = _experts(xs, tile_expert, n_used, w_gate, w_up, w_down)
    return _combine(ys, _tile_positions(route, off, tmc, tr), y_tok, gates, ln_g, ln_b,
                    tmc=tmc, tr=tr)


POOL_HALO = 32


def _inproj1_kernel(x_ref, w_ref, pw_ref, ps_ref, qt_ref, k_ref, vt_ref, d_ref, halo_ref,
                    *, sb_w):
    i = pl.program_id(1)
    x = x_ref[0].astype(BF16)
    tm = x.shape[0]
    q = jnp.dot(x, w_ref[:, 0:sb_w], preferred_element_type=F32) * ATTN_SCALE
    qt_ref[0] = q.T.astype(BF16)
    k_ref[0] = jnp.dot(x, w_ref[:, sb_w:2 * sb_w], preferred_element_type=F32).astype(BF16)
    v = jnp.dot(x, w_ref[:, 2 * sb_w:3 * sb_w], preferred_element_type=F32)
    vt_ref[0] = v.T.astype(BF16)
    u = jnp.dot(x, w_ref[:, 3 * sb_w:], preferred_element_type=F32)

    @pl.when(i == 0)
    def _():
        halo_ref[...] = jnp.zeros_like(halo_ref)

    tpos = i * tm + lax.broadcasted_iota(jnp.int32, (tm, 1), 0)
    outs = []
    for g, win in enumerate(POOL_WINDOWS):
        cs = slice(g * POOL_CH, (g + 1) * POOL_CH)
        cur = u[:, cs]
        run = jnp.concatenate([halo_ref[:, cs], cur], axis=0)
        span = 1
        while span < win:
            n = run.shape[0] - span
            run = run[span:span + n] + run[0:n]
            span *= 2
        off = POOL_HALO - (win - 1)
        wsum = run[off:off + tm]
        count = jnp.minimum(tpos + 1, win).astype(F32)
        pooled = wsum / count - cur
        mixed = jnp.dot(pooled.astype(BF16), pw_ref[g], preferred_element_type=F32)
        outs.append(mixed * ps_ref[:, cs])
    halo_ref[...] = u[tm - POOL_HALO:tm, :]
    d_ref[0] = jnp.concatenate(outs, axis=1).astype(d_ref.dtype)


def _inproj1(x, w_bf, pool_w_bf, pool_scale, *, sb_w, tm):
    bsz, n_pos, d = x.shape
    n_in = w_bf.shape[1]
    pool_w = n_in - 3 * sb_w
    n_g = len(POOL_WINDOWS)
    return pl.pallas_call(
        functools.partial(_inproj1_kernel, sb_w=sb_w),
        grid=(bsz, n_pos // tm),
        in_specs=[
            pl.BlockSpec((1, tm, d), lambda b, i: (b, i, 0)),
            pl.BlockSpec((d, n_in), lambda b, i: (0, 0)),
            pl.BlockSpec((n_g, POOL_CH, POOL_CH), lambda b, i: (0, 0, 0)),
            pl.BlockSpec((1, pool_w), lambda b, i: (0, 0)),
        ],
        out_specs=[
            pl.BlockSpec((1, sb_w, tm), lambda b, i: (b, 0, i)),
            pl.BlockSpec((1, tm, sb_w), lambda b, i: (b, i, 0)),
            pl.BlockSpec((1, sb_w, tm), lambda b, i: (b, 0, i)),
            pl.BlockSpec((1, tm, pool_w), lambda b, i: (b, i, 0)),
        ],
        out_shape=[
            jax.ShapeDtypeStruct((bsz, sb_w, n_pos), BF16),
            jax.ShapeDtypeStruct((bsz, n_pos, sb_w), BF16),
            jax.ShapeDtypeStruct((bsz, sb_w, n_pos), BF16),
            jax.ShapeDtypeStruct((bsz, n_pos, pool_w), BF16),
        ],
        scratch_shapes=[pltpu.VMEM((POOL_HALO, pool_w), F32)],
        compiler_params=_params(("parallel", "arbitrary")),
        name="inproj1",
    )(x, w_bf, pool_w_bf, pool_scale.astype(F32).reshape(1, pool_w))


def _sb_attn_kernel(qt_ref, k_ref, vt_ref, o_ref, za_ref, zb_ref, *, tq, tk):
    i = pl.program_id(2)
    qt = qt_ref[0]
    zeros = jnp.zeros((HEAD_DIM, tq), BF16)
    qpads = (jnp.concatenate([qt[:HEAD_DIM], zeros], axis=0),
             jnp.concatenate([zeros, qt[HEAD_DIM:]], axis=0))
    cb = min(SB_SCAN_ROWS, tk)
    upper = (lax.broadcasted_iota(jnp.int32, (cb, cb), 1)
             > lax.broadcasted_iota(jnp.int32, (cb, cb), 0)).astype(BF16)

    heads = range(2)

    def scores(j, buf):
        kj = k_ref[0, pl.ds(pl.multiple_of(j * tk, tk), tk), :]
        for hh in heads:
            buf[hh] = jnp.dot(kj, qpads[hh], preferred_element_type=F32)

    def consume(j, buf, carry, masked, lane0=0):
        k0 = pl.multiple_of(j * tk, tk)
        nq = tq - lane0
        whole = carry
        carry = [(acc[:, lane0:], later[:, lane0:]) for acc, later in whole]
        if masked:
            key = k0 + lax.broadcasted_iota(jnp.int32, (tk, nq), 0)
            qpos = i * tq + lane0 + lax.broadcasted_iota(jnp.int32, (tk, nq), 1)
            keep = key < qpos
        zs = [buf[hh, :, lane0:tq] for hh in heads]
        sps = [jnp.maximum(z, 0.0) + jnp.log(1.0 + jnp.exp2(jnp.abs(z) * (-LOG2E))) for z in zs]
        if masked:
            sps = [jnp.where(keep, sp, 0.0) for sp in sps]
        his = [sp.astype(BF16) for sp in sps]
        laters = [carry[hh][1] for hh in heads]
        pieces = [[None] * (tk // cb) for _ in heads]
        for c in reversed(range(tk // cb)):
            rs = slice(c * cb, (c + 1) * cb)
            sufs = [jnp.dot(upper, his[hh][rs], preferred_element_type=F32)
                    for hh in heads]
            for hh in heads:
                spc = sps[hh][rs]
                pieces[hh][c] = jnp.exp2((zs[hh][rs] - (spc + sufs[hh] + laters[hh])) * LOG2E)
                laters[hh] = laters[hh] + sufs[hh][0:1] + spc[0:1]
        out = []
        for hh in heads:
            a = jnp.concatenate(pieces[hh], axis=0)
            if masked:
                a = jnp.where(keep, a, 0.0)
            vj = vt_ref[0, hh * HEAD_DIM:(hh + 1) * HEAD_DIM, pl.ds(k0, tk)]
            acc = carry[hh][0] + jnp.dot(vj, a.astype(BF16), preferred_element_type=F32)
            later = laters[hh]
            if lane0:
                acc = jnp.concatenate([whole[hh][0][:, :lane0], acc], axis=1)
                later = jnp.concatenate([whole[hh][1][:, :lane0], later], axis=1)
            out.append((acc, later))
        return tuple(out)

    init = (jnp.zeros((HEAD_DIM, tq), F32), jnp.zeros((1, tq), F32))
    carry = (init, init)
    n_diag = tq // tk
    last = (i + 1) * n_diag - 1
    n_old = i * n_diag
    bufs = (za_ref, zb_ref)
    scores(last, bufs[0])
    for d in range(n_diag):
        nxt = last - d - 1 if d + 1 < n_diag else jnp.maximum(n_old - 1, 0)
        scores(nxt, bufs[(d + 1) % 2])
        carry = consume(last - d, bufs[d % 2], carry, True, lane0=tq - (d + 1) * tk)
    first_old = bufs[n_diag % 2]
    spare = bufs[(n_diag + 1) % 2]

    def reach(c):
        return jnp.min(jnp.minimum(c[0][1], c[1][1]))

    def live(state):
        t, r, _ = state
        return jnp.logical_and(t < n_old, r < SB_UNDERFLOW)

    def older(state):
        t, _, c = state
        scores(n_old - 1 - t, spare)
        c = consume(n_old - 1 - t, spare, c, False)
        return t + 1, reach(c), c

    r0 = reach(carry)
    carry = lax.cond(jnp.logical_and(n_old > 0, r0 < SB_UNDERFLOW),
                     lambda c: consume(n_old - 1, first_old, c, False), lambda c: c, carry)
    _, _, carry = lax.while_loop(live, older, (jnp.int32(1), reach(carry), carry))
    o_ref[0] = jnp.concatenate([carry[0][0], carry[1][0]], axis=0).T.astype(o_ref.dtype)


def _sb_attention(qt, k, vt, *, tq, tk):
    bsz, sb_w, n_pos = qt.shape
    return pl.pallas_call(
        functools.partial(_sb_attn_kernel, tq=tq, tk=tk),
        grid=(bsz, sb_w // LANES, n_pos // tq),
        in_specs=[
            pl.BlockSpec((1, LANES, tq), lambda b, h, i: (b, h, i)),
            pl.BlockSpec((1, n_pos, LANES), lambda b, h, i: (b, 0, h)),
            pl.BlockSpec((1, LANES, n_pos), lambda b, h, i: (b, h, 0)),
        ],
        out_specs=pl.BlockSpec((1, tq, LANES), lambda b, h, i: (b, i, h)),
        out_shape=jax.ShapeDtypeStruct((bsz, n_pos, sb_w), BF16),
        scratch_shapes=[pltpu.VMEM((2, tk, tq), F32), pltpu.VMEM((2, tk, tq), F32)],
        compiler_params=_params(("parallel", "parallel", "arbitrary")),
        name="sb_attention",
    )(qt, k, vt)


def _rope_tables(n_pos):
    inv = ROPE_THETA ** (-jnp.arange(0, HEAD_DIM, 2, dtype=F32) / HEAD_DIM)
    ang = jnp.arange(n_pos, dtype=F32)[:, None] * inv[None, :]
    ang = jnp.concatenate([ang, ang], axis=-1)
    return jnp.cos(ang), jnp.sin(ang)


def _lambda_init(layer_idx):
    return 0.8 - 0.6 * math.exp(-0.3 * layer_idx)


def kernel(x, router_w, router_bias, l0_in_proj, l0_conv_w, l0_conv_b, l0_conv_ln_g, l0_conv_ln_b, l0_lambda_q1, l0_lambda_k1, l0_lambda_q2, l0_lambda_k2, l0_subln_g, l0_out_proj, l0_ln_mix_g, l0_ln_mix_b, l0_w_gate, l0_w_up, l0_w_down, l0_ln_ffn_g, l0_ln_ffn_b, l1_in_proj, l1_pool_w, l1_pool_scale, l1_out_proj, l1_ln_mix_g, l1_ln_mix_b, l1_w_gate, l1_w_up, l1_w_down, l1_ln_ffn_g, l1_ln_ffn_b):
    bsz, n_pos, d = x.shape
    n_tok = bsz * n_pos
    x = x.astype(F32)

    conv_ch = l0_conv_w.shape[1]
    dv = l0_subln_g.shape[0]
    n_diff = (l0_out_proj.shape[0] - conv_ch) // dv
    qk_w = n_diff * 2 * HEAD_DIM
    v_w = n_diff * dv

    tm = min(512, n_pos)
    tq_diff = min(512, n_pos)
    tq_sb = min(512, n_pos)
    tk_sb = min(256, n_pos)
    ts = min(256, n_pos)
    tmc = min(256, n_pos)
    tmd = min(2048, n_tok)
    tm_out = min(1024, n_tok)

    cos, sin = _rope_tables(n_pos)
    sign = jnp.where(jnp.arange(HEAD_DIM) < HEAD_DIM // 2, -1.0, 1.0).astype(F32)
    cos128 = jnp.concatenate([cos, cos], axis=-1)
    sins128 = jnp.concatenate([sin * sign, sin * sign], axis=-1)
    cost = cos.T
    sint = sin.T

    glu, qt, k, vt = _inproj0(x, l0_in_proj.astype(BF16), cos128, sins128, cost, sint,
                              conv_ch=conv_ch, qk_w=qk_w, v_w=v_w, tm=tm)
    a = _conv_module(glu, l0_conv_w.astype(F32), l0_conv_b.astype(F32),
                     l0_conv_ln_g.astype(F32), l0_conv_ln_b.astype(F32), ts=ts)
    lam_init = _lambda_init(0)
    lam = (jnp.exp(jnp.sum(l0_lambda_q1.astype(F32) * l0_lambda_k1.astype(F32)))
           - jnp.exp(jnp.sum(l0_lambda_q2.astype(F32) * l0_lambda_k2.astype(F32))) + lam_init)
    o = _diff_attention(qt, k, vt, lam, l0_subln_g, n_heads=n_diff, lam_init=lam_init,
                        tq=tq_diff, tk=tq_diff)
    xf = x.reshape(n_tok, d)
    y, route, gates, counts = _outproj_ln_route(
        a.reshape(n_tok, conv_ch), o.reshape(n_tok, v_w), l0_out_proj, xf,
        l0_ln_mix_g.astype(F32), l0_ln_mix_b.astype(F32), router_w, router_bias, tm=tm_out)
    xf = _moe_sparse(y, route, gates, counts, l0_w_gate, l0_w_up, l0_w_down,
                     l0_ln_ffn_g.astype(F32), l0_ln_ffn_b.astype(F32), tmd=tmd, tmc=tmc)

    sb_w = l1_in_proj.shape[1] - len(POOL_WINDOWS) * POOL_CH
    sb_w = sb_w // 3
    qt, k, vt, dpool = _inproj1(xf.reshape(bsz, n_pos, d), l1_in_proj.astype(BF16),
                                l1_pool_w.astype(BF16), l1_pool_scale, sb_w=sb_w, tm=tm)
    o = _sb_attention(qt, k, vt, tq=tq_sb, tk=tk_sb)
    y, route, gates, counts = _outproj_ln_route(
        o.reshape(n_tok, sb_w), dpool.reshape(n_tok, -1), l1_out_proj, xf,
        l1_ln_mix_g.astype(F32), l1_ln_mix_b.astype(F32), router_w, router_bias, tm=tm_out)
    xf = _moe_sparse(y, route, gates, counts, l1_w_gate, l1_w_up, l1_w_down,
                     l1_ln_ffn_g.astype(F32), l1_ln_ffn_b.astype(F32), tmd=tmd, tmc=tmc)
    return xf.reshape(bsz, n_pos, d)
```

```python
import functools
import math

import jax
import jax.numpy as jnp
from jax import lax
from jax.experimental import pallas as pl
from jax.experimental.pallas import tpu as pltpu

F32 = jnp.float32
BF16 = jnp.bfloat16

DEPTH = 2
HEAD_DIM = 64
ROPE_THETA = 10000.0
LN_EPS = 1e-5
RMS_EPS = 1e-5
DEEPNORM_ALPHA = (2 * DEPTH) ** 0.25
CONV_WIDTH = 31
POOL_WINDOWS = (2, 4, 8, 16)
POOL_CH = 128
N_EXPERTS = 16
N_GROUPS = 4
EXPERTS_PER_GROUP = N_EXPERTS // N_GROUPS
ATTN_SCALE = HEAD_DIM ** -0.5
LOG2E = math.log2(math.e)

LANES = 128
SUBLANES = 8
MIB = 1024 * 1024
VMEM_LIMIT = 52 * MIB

NEG_BIG = -1e30
SB_SCAN_ROWS = 256
SB_UNDERFLOW = 110.0


def _params(semantics, vmem=VMEM_LIMIT):
    return pltpu.CompilerParams(dimension_semantics=semantics, vmem_limit_bytes=vmem)


def _layer_norm_rows(h, g, b):
    mu = jnp.mean(h, axis=-1, keepdims=True)
    d = h - mu
    var = jnp.mean(d * d, axis=-1, keepdims=True)
    return d * lax.rsqrt(var + LN_EPS) * g + b


def _load_token_tiles(ref, row0, n, d):
    k = d // LANES
    return jnp.concatenate([ref[pl.ds(row0 + c, n, stride=k), :] for c in range(k)], axis=1)


def _store_token_tiles(ref, row0, val):
    n, d = val.shape
    k = d // LANES
    for c in range(k):
        ref[pl.ds(row0 + c, n, stride=k), :] = val[:, c * LANES:(c + 1) * LANES]


def _inproj0_kernel(x_ref, w_ref, cos_ref, sins_ref, cost_ref, sint_ref,
                    glu_ref, qt_ref, k_ref, vt_ref, *, conv_ch, qk_w):
    x = x_ref[0].astype(BF16)
    c0, c1, c2, c3 = conv_ch, 2 * conv_ch, 2 * conv_ch + qk_w, 2 * conv_ch + 2 * qk_w

    a_val = jnp.dot(x, w_ref[:, 0:c0], preferred_element_type=F32)
    a_gate = jnp.dot(x, w_ref[:, c0:c1], preferred_element_type=F32)
    glu_ref[0] = a_val * jax.nn.sigmoid(a_gate)

    dq = jnp.dot(x, w_ref[:, c1:c2], preferred_element_type=F32)
    dqt = dq.T
    cost = cost_ref[...]
    sint = sint_ref[...]
    half = HEAD_DIM // 2
    pieces = []
    for blk in range(qk_w // HEAD_DIM):
        u = dqt[blk * HEAD_DIM:(blk + 1) * HEAD_DIM]
        rot = jnp.concatenate([-u[half:], u[:half]], axis=0)
        pieces.append((u * cost + rot * sint) * (ATTN_SCALE * LOG2E))
    qt_ref[0] = jnp.concatenate(pieces, axis=0).astype(BF16)

    dk = jnp.dot(x, w_ref[:, c2:c3], preferred_element_type=F32)
    cos = cos_ref[...]
    sins = sins_ref[...]
    lane = lax.broadcasted_iota(jnp.int32, cos.shape, 1)
    first = (lane & (HEAD_DIM - 1)) < half
    kparts = []
    for h in range(qk_w // LANES):
        u = dk[:, h * LANES:(h + 1) * LANES]
        rot = jnp.where(first, pltpu.roll(u, LANES - half, 1), pltpu.roll(u, half, 1))
        kparts.append(u * cos + rot * sins)
    k_ref[0] = jnp.concatenate(kparts, axis=1).astype(BF16)

    dv = jnp.dot(x, w_ref[:, c3:], preferred_element_type=F32)
    vt_ref[0] = dv.T.astype(BF16)


def _inproj0(x, w_bf, cos128, sins128, cost, sint, *, conv_ch, qk_w, v_w, tm):
    bsz, n_pos, d = x.shape
    n_in = w_bf.shape[1]
    kern = functools.partial(_inproj0_kernel, conv_ch=conv_ch, qk_w=qk_w)
    return pl.pallas_call(
        kern,
        grid=(bsz, n_pos // tm),
        in_specs=[
            pl.BlockSpec((1, tm, d), lambda b, i: (b, i, 0)),
            pl.BlockSpec((d, n_in), lambda b, i: (0, 0)),
            pl.BlockSpec((tm, LANES), lambda b, i: (i, 0)),
            pl.BlockSpec((tm, LANES), lambda b, i: (i, 0)),
            pl.BlockSpec((HEAD_DIM, tm), lambda b, i: (0, i)),
            pl.BlockSpec((HEAD_DIM, tm), lambda b, i: (0, i)),
        ],
        out_specs=[
            pl.BlockSpec((1, tm, conv_ch), lambda b, i: (b, i, 0)),
            pl.BlockSpec((1, qk_w, tm), lambda b, i: (b, 0, i)),
            pl.BlockSpec((1, tm, qk_w), lambda b, i: (b, i, 0)),
            pl.BlockSpec((1, v_w, tm), lambda b, i: (b, 0, i)),
        ],
        out_shape=[
            jax.ShapeDtypeStruct((bsz, n_pos, conv_ch), F32),
            jax.ShapeDtypeStruct((bsz, qk_w, n_pos), BF16),
            jax.ShapeDtypeStruct((bsz, n_pos, qk_w), BF16),
            jax.ShapeDtypeStruct((bsz, v_w, n_pos), BF16),
        ],
        compiler_params=_params(("parallel", "arbitrary")),
        name="inproj0",
    )(x, w_bf, cos128, sins128, cost, sint)


CONV_HALO = 32
CONV_ROW_BLOCK = 64


def _conv_kernel(a_ref, cw_ref, cb_ref, g_ref, b_ref, o_ref, win_ref, acc_ref, *, ts):
    i = pl.program_id(1)
    t0 = pl.multiple_of(i * ts, ts)
    h0 = pl.multiple_of(jnp.maximum(t0 - CONV_HALO, 0), CONV_HALO)
    n_ch = a_ref.shape[2]
    n_shift = ts + CONV_HALO - SUBLANES
    for c in range(n_ch // LANES):
        cs = slice(c * LANES, (c + 1) * LANES)
        cur = a_ref[0, pl.ds(t0, ts), cs]
        halo = a_ref[0, pl.ds(h0, CONV_HALO), cs]
        halo = jnp.where(i > 0, halo, 0.0)
        xw = jnp.concatenate([halo, cur], axis=0)
        win_ref[c, 0] = xw
        for r in range(1, SUBLANES):
            win_ref[c, r, 0:n_shift] = xw[r:r + n_shift]
        def row_block(rb, carry, c=c, cs=cs, full=xw.shape[0]):
            base = pl.multiple_of(rb * CONV_ROW_BLOCK, CONV_ROW_BLOCK)
            acc = jnp.zeros((CONV_ROW_BLOCK, LANES), F32) + cb_ref[:, cs]
            for r in range(SUBLANES):
                rows = full if r == 0 else n_shift
                for a in range(CONV_HALO // SUBLANES + 1):
                    w = SUBLANES * a + r - (CONV_HALO - CONV_WIDTH + 1)
                    if 0 <= w < CONV_WIDTH and SUBLANES * a + ts <= rows:
                        tap = win_ref[c, r, pl.ds(base + SUBLANES * a, CONV_ROW_BLOCK)]
                        acc = acc + tap * cw_ref[w:w + 1, cs]
            acc_ref[pl.ds(base, CONV_ROW_BLOCK), cs] = acc
            return carry
        lax.fori_loop(0, ts // CONV_ROW_BLOCK, row_block, 0)
    y = _layer_norm_rows(acc_ref[...], g_ref[...], b_ref[...])
    o_ref[0] = (y * jax.nn.sigmoid(y)).astype(o_ref.dtype)


def _conv_module(a, conv_w, conv_b, ln_g, ln_b, *, ts):
    bsz, n_pos, n_ch = a.shape
    return pl.pallas_call(
        functools.partial(_conv_kernel, ts=ts),
        grid=(bsz, n_pos // ts),
        in_specs=[
            pl.BlockSpec((1, n_pos, n_ch), lambda b, i: (b, 0, 0)),
            pl.BlockSpec((CONV_WIDTH, n_ch), lambda b, i: (0, 0)),
            pl.BlockSpec((1, n_ch), lambda b, i: (0, 0)),
            pl.BlockSpec((1, n_ch), lambda b, i: (0, 0)),
            pl.BlockSpec((1, n_ch), lambda b, i: (0, 0)),
        ],
        out_specs=pl.BlockSpec((1, ts, n_ch), lambda b, i: (b, i, 0)),
        out_shape=jax.ShapeDtypeStruct((bsz, n_pos, n_ch), BF16),
        scratch_shapes=[pltpu.VMEM((n_ch // LANES, SUBLANES, ts + CONV_HALO, LANES), F32),
                        pltpu.VMEM((ts, n_ch), F32)],
        compiler_params=_params(("parallel", "arbitrary")),
        name="conv_module",
    )(a, conv_w, conv_b.reshape(1, n_ch), ln_g.reshape(1, n_ch), ln_b.reshape(1, n_ch))


def _diff_attn_kernel(lam_ref, qt_ref, k_ref, vt_ref, g_ref, o_ref, sa_ref, sb_ref,
                      *, tq, tk, out_scale):
    i = pl.program_id(2)
    lam = lam_ref[0, 0]
    qt = qt_ref[0]
    zeros = jnp.zeros((HEAD_DIM, tq), BF16)
    qa = jnp.concatenate([qt[:HEAD_DIM], zeros], axis=0)
    qb = jnp.concatenate([zeros, qt[HEAD_DIM:]], axis=0)
    dv = vt_ref.shape[1]

    def online(s, vj, m, l, acc):
        m_new = jnp.maximum(m, jnp.max(s, axis=0, keepdims=True))
        alpha = jnp.exp2(m - m_new)
        p = jnp.exp2(s - m_new)
        l_new = alpha * l + jnp.sum(p, axis=0, keepdims=True)
        acc_new = alpha * acc + jnp.dot(vj, p.astype(BF16), preferred_element_type=F32)
        return m_new, l_new, acc_new

    def scores(j, buf):
        kj = k_ref[0, pl.ds(pl.multiple_of(j * tk, tk), tk), :]
        buf[0] = jnp.dot(kj, qa, preferred_element_type=F32)
        buf[1] = jnp.dot(kj, qb, preferred_element_type=F32)

    def consume(j, buf, carry, masked):
        if masked:
            return consume_diagonal(j, buf, carry)
        m1, l1, acc1, m2, l2, acc2 = carry
        vj = vt_ref[0, :, pl.ds(pl.multiple_of(j * tk, tk), tk)]
        m1, l1, acc1 = online(buf[0], vj, m1, l1, acc1)
        m2, l2, acc2 = online(buf[1], vj, m2, l2, acc2)
        return m1, l1, acc1, m2, l2, acc2

    def consume_diagonal(j, buf, carry):
        hk = tk // 2
        for key_lo, lane0 in ((0, 0), (hk, tq - hk)):
            nq = tq - lane0
            k0 = pl.multiple_of(j * tk + key_lo, hk)
            vj = vt_ref[0, :, pl.ds(k0, hk)]
            key = k0 + lax.broadcasted_iota(jnp.int32, (hk, nq), 0)
            qpos = i * tq + lane0 + lax.broadcasted_iota(jnp.int32, (hk, nq), 1)
            keep = key <= qpos
            new = []
            for mp in range(2):
                m, l, acc = carry[3 * mp:3 * mp + 3]
                s = jnp.where(keep, buf[mp, key_lo:key_lo + hk, lane0:tq], NEG_BIG)
                part = online(s, vj, m[:, lane0:], l[:, lane0:], acc[:, lane0:])
                if lane0:
                    part = [jnp.concatenate([old[:, :lane0], upd], axis=1)
                            for old, upd in zip((m, l, acc), part)]
                new.extend(part)
            carry = tuple(new)
        return carry

    row = jnp.full((1, tq), NEG_BIG, F32)
    zrow = jnp.zeros((1, tq), F32)
    zacc = jnp.zeros((dv, tq), F32)
    carry = (row, zrow, zacc, row, zrow, zacc)

    scores(0, sa_ref)

    def pair(u, c):
        scores(2 * u + 1, sb_ref)
        c = consume(2 * u, sa_ref, c, False)
        scores(2 * u + 2, sa_ref)
        return consume(2 * u + 1, sb_ref, c, False)

    carry = lax.fori_loop(0, lax.shift_right_logical(i, 1), pair, carry)

    def odd_tail(c):
        scores(i, sb_ref)
        c = consume(i - 1, sa_ref, c, False)
        return consume(i, sb_ref, c, True)

    def even_tail(c):
        return consume(i, sa_ref, c, True)

    m1, l1, acc1, m2, l2, acc2 = lax.cond((i & 1) == 1, odd_tail, even_tail, carry)

    o = acc1 / l1 - lam * (acc2 / l2)
    o = o * lax.rsqrt(jnp.mean(o * o, axis=0, keepdims=True) + RMS_EPS)
    o = o * g_ref[...] * out_scale
    o_ref[0] = o.T.astype(o_ref.dtype)


def _diff_attention(qt, k, vt, lam, subln_g, *, n_heads, lam_init, tq, tk):
    bsz, _, n_pos = qt.shape
    dv = vt.shape[1] // n_heads
    kern = functools.partial(_diff_attn_kernel, tq=tq, tk=tk, out_scale=1.0 - lam_init)
    return pl.pallas_call(
        kern,
        grid=(bsz, n_heads, n_pos // tq),
        in_specs=[
            pl.BlockSpec(memory_space=pltpu.SMEM),
            pl.BlockSpec((1, 2 * HEAD_DIM, tq), lambda b, h, i: (b, h, i)),
            pl.BlockSpec((1, n_pos, 2 * HEAD_DIM), lambda b, h, i: (b, 0, h)),
            pl.BlockSpec((1, dv, n_pos), lambda b, h, i: (b, h, 0)),
            pl.BlockSpec((dv, 1), lambda b, h, i: (0, 0)),
        ],
        out_specs=pl.BlockSpec((1, tq, dv), lambda b, h, i: (b, i, h)),
        out_shape=jax.ShapeDtypeStruct((bsz, n_pos, n_heads * dv), BF16),
        scratch_shapes=[pltpu.VMEM((2, tk, tq), F32), pltpu.VMEM((2, tk, tq), F32)],
        compiler_params=_params(("parallel", "parallel", "arbitrary")),
        name="diff_attention",
    )(lam.reshape(1, 1), qt, k, vt, subln_g.astype(F32).reshape(dv, 1))


def _masked_top2(vals, member, row):
    big = N_EXPERTS
    v = jnp.where(member, vals, -jnp.inf)
    v1 = jnp.max(v, axis=0, keepdims=True)
    i1 = jnp.min(jnp.where(v == v1, row, big), axis=0, keepdims=True)
    v = jnp.where(row == i1, -jnp.inf, v)
    v2 = jnp.max(v, axis=0, keepdims=True)
    i2 = jnp.min(jnp.where(v == v2, row, big), axis=0, keepdims=True)
    return v1, i1, v2, i2


def _top2_route(aff, sel):
    row = lax.broadcasted_iota(jnp.int32, sel.shape, 0)
    group = lax.shift_right_logical(row, EXPERTS_PER_GROUP.bit_length() - 1)
    gbest = None
    for g in range(N_GROUPS):
        v1, _, v2, _ = _masked_top2(sel, group == g, row)
        score = v1 + v2
        if gbest is None:
            gbest, gval = jnp.zeros_like(score, dtype=jnp.int32), score
        else:
            better = score > gval
            gbest = jnp.where(better, g, gbest)
            gval = jnp.where(better, score, gval)
    _, e1, _, e2 = _masked_top2(sel, group == gbest, row)
    a1 = jnp.sum(jnp.where(row == e1, aff, 0.0), axis=0, keepdims=True)
    a2 = jnp.sum(jnp.where(row == e2, aff, 0.0), axis=0, keepdims=True)
    denom = a1 + a2
    return e1, e2, a1 / denom, a2 / denom


ROUTE_ROWS = SUBLANES


def _outproj_kernel(pa_ref, pb_ref, wa_ref, wb_ref, x_ref, g_ref, b_ref, rw_ref, rb_ref,
                    y_ref, route_ref, gate_ref, count_ref, seen_ref):
    i = pl.program_id(0)

    @pl.when(i == 0)
    def _():
        seen_ref[...] = jnp.zeros_like(seen_ref)

    mix = jnp.dot(pa_ref[...], wa_ref[...], preferred_element_type=F32)
    mix = mix + jnp.dot(pb_ref[...], wb_ref[...], preferred_element_type=F32)
    y = _layer_norm_rows(DEEPNORM_ALPHA * x_ref[...] + mix, g_ref[...], b_ref[...])
    _store_token_tiles(y_ref, 0, y)
    tm = y.shape[0]
    yh = y.astype(BF16)
    yl = (y - yh.astype(F32)).astype(BF16)
    nt = (((1,), (1,)), ((), ()))
    both = lax.dot_general(rw_ref[...], yh, nt, preferred_element_type=F32)
    logits = (both[:N_EXPERTS] + both[N_EXPERTS:]
              + lax.dot_general(rw_ref[0:N_EXPERTS], yl, nt, preferred_element_type=F32))
    aff = jax.nn.sigmoid(logits)
    e1, e2, g1, g2 = _top2_route(aff, aff + rb_ref[...])

    erow = lax.broadcasted_iota(jnp.int32, (N_EXPERTS, tm), 0)
    hit = jnp.where((erow == e1) | (erow == e2), 1.0, 0.0)
    upto = (lax.broadcasted_iota(jnp.int32, (tm, tm), 0)
            <= lax.broadcasted_iota(jnp.int32, (tm, tm), 1)).astype(BF16)
    incl = jnp.dot(hit.astype(BF16), upto, preferred_element_type=F32)
    before = seen_ref[:, 0:1] + incl - hit
    r1 = jnp.sum(jnp.where(erow == e1, before, 0.0), axis=0, keepdims=True).astype(jnp.int32)
    r2 = jnp.sum(jnp.where(erow == e2, before, 0.0), axis=0, keepdims=True).astype(jnp.int32)
    seen_ref[...] = seen_ref[...] + jnp.sum(hit, axis=1, keepdims=True)
    count_ref[...] = seen_ref[...]

    rrow = lax.broadcasted_iota(jnp.int32, (ROUTE_ROWS, tm), 0)
    route_ref[0] = jnp.where(rrow == 0, e1, jnp.where(rrow == 1, e2,
                             jnp.where(rrow == 2, r1, jnp.where(rrow == 3, r2, 0))))
    grow = lax.broadcasted_iota(jnp.int32, (LANES, tm), 0)
    gate_ref[...] = jnp.where(grow == 0, g1, jnp.where(grow == 1, g2, 0.0)).T


def _outproj_ln_route(pa, pb, w_out, x, ln_g, ln_b, router_w, router_bias, *, tm):
    n_tok, d = x.shape
    ka, kb = pa.shape[1], pb.shape[1]
    wa = w_out[:ka].astype(BF16)
    wb = w_out[ka:].astype(BF16)
    rw = router_w.astype(F32).T
    rw_hi = rw.astype(BF16)
    rw_lo = (rw - rw_hi.astype(F32)).astype(BF16)
    rows = lambda i: (i, 0)
    fixed = lambda i: (0, 0)
    return pl.pallas_call(
        _outproj_kernel,
        grid=(n_tok // tm,),
        in_specs=[
            pl.BlockSpec((tm, ka), rows),
            pl.BlockSpec((tm, kb), rows),
            pl.BlockSpec((ka, d), fixed),
            pl.BlockSpec((kb, d), fixed),
            pl.BlockSpec((tm, d), rows),
            pl.BlockSpec((1, d), fixed),
            pl.BlockSpec((1, d), fixed),
            pl.BlockSpec((2 * N_EXPERTS, d), fixed),
            pl.BlockSpec((N_EXPERTS, 1), fixed),
        ],
        out_specs=[
            pl.BlockSpec((tm * d // LANES, LANES), rows),
            pl.BlockSpec((1, ROUTE_ROWS, tm), lambda i: (i, 0, 0)),
            pl.BlockSpec((tm, LANES), rows),
            pl.BlockSpec((N_EXPERTS, LANES), fixed),
        ],
        out_shape=[
            jax.ShapeDtypeStruct((n_tok * d // LANES, LANES), F32),
            jax.ShapeDtypeStruct((n_tok // tm, ROUTE_ROWS, tm), jnp.int32),
            jax.ShapeDtypeStruct((n_tok, LANES), F32),
            jax.ShapeDtypeStruct((N_EXPERTS, LANES), F32),
        ],
        scratch_shapes=[pltpu.VMEM((N_EXPERTS, LANES), F32)],
        compiler_params=_params(("arbitrary",)),
        name="outproj_ln_route",
    )(pa, pb, wa, wb, x, ln_g.reshape(1, d), ln_b.reshape(1, d),
      jnp.concatenate([rw_hi, rw_lo], axis=0), router_bias.astype(F32).reshape(N_EXPERTS, 1))


MOE_ROW_TILE = 512
DMA_UNROLL = 8


def _row_copy_wait(src_hbm, dst, sem, n_rows):
    pltpu.make_async_copy(src_hbm.at[pl.ds(0, n_rows)], dst, sem).wait()


def _dispatch_kernel(offend_ref, pos_ref, y_ref, xs_hbm, zero_ref, sem, zsem, *, tmd, tr):
    i = pl.program_id(0)
    tile_rows = MOE_ROW_TILE * tr

    @pl.when(i == 0)
    def _():
        zero_ref[...] = jnp.zeros_like(zero_ref)
        def tail_copy(e):
            start = pl.multiple_of((offend_ref[e] - MOE_ROW_TILE) * tr, tile_rows)
            return pltpu.make_async_copy(zero_ref, xs_hbm.at[pl.ds(start, tile_rows)], zsem)
        def nonempty(e):
            return offend_ref[e] > (offend_ref[e - 1] if e > 0 else 0)
        for e in range(N_EXPERTS):
            @pl.when(nonempty(e))
            def _():
                tail_copy(e).start()
        for e in range(N_EXPERTS):
            @pl.when(nonempty(e))
            def _():
                tail_copy(e).wait()
        def unused_copy(j):
            return pltpu.make_async_copy(
                zero_ref, xs_hbm.at[pl.ds(pl.multiple_of(j * tile_rows, tile_rows), tile_rows)],
                zsem)
        first_unused = lax.shift_right_logical(offend_ref[N_EXPERTS - 1],
                                               MOE_ROW_TILE.bit_length() - 1)
        n_tiles = xs_hbm.shape[0] // tile_rows
        lax.fori_loop(first_unused, n_tiles, lambda j, c: (unused_copy(j).start(), c)[1], 0)
        lax.fori_loop(first_unused, n_tiles, lambda j, c: (unused_copy(j).wait(), c)[1], 0)

    def issue(g, c):
        r0 = g * DMA_UNROLL
        ps = [[pos_ref[0, 0, k * tmd + r0 + j] for k in range(2)] for j in range(DMA_UNROLL)]
        for j in range(DMA_UNROLL):
            src = y_ref.at[pl.ds(pl.multiple_of((r0 + j) * tr, tr), tr)]
            for k in range(2):
                dst = xs_hbm.at[pl.ds(pl.multiple_of(ps[j][k], tr), tr)]
                pltpu.make_async_copy(src, dst, sem).start(priority=k)
        return c

    lax.fori_loop(0, tmd // DMA_UNROLL, issue, 0)
    for k in range(2):
        pltpu.make_async_copy(y_ref, xs_hbm.at[pl.ds(0, tmd * tr)], sem).wait()


def _dispatch(y_tok, pos, off_end, *, n_sorted, tmd, tr):
    n_tok = y_tok.shape[0] // tr
    return pl.pallas_call(
        functools.partial(_dispatch_kernel, tmd=tmd, tr=tr),
        grid_spec=pltpu.PrefetchScalarGridSpec(
            num_scalar_prefetch=1,
            grid=(n_tok // tmd,),
            in_specs=[
                pl.BlockSpec((1, 1, 2 * tmd), lambda i, off: (i, 0, 0), memory_space=pltpu.SMEM),
                pl.BlockSpec((tmd * tr, LANES), lambda i, off: (i, 0)),
            ],
            out_specs=pl.BlockSpec(memory_space=pl.ANY),
            scratch_shapes=[pltpu.VMEM((MOE_ROW_TILE * tr, LANES), F32),
                            pltpu.SemaphoreType.DMA, pltpu.SemaphoreType.DMA],
        ),
        out_shape=jax.ShapeDtypeStruct((n_sorted * tr, LANES), F32),
        compiler_params=_params(("arbitrary",)),
        name="moe_dispatch",
    )(off_end, pos, y_tok)


def _experts_kernel(texp_ref, nused_ref, xs_ref, wg_ref, wu_ref, wd_ref, ys_ref,
                    wg_bf, wu_bf, wd_bf):
    i = pl.program_id(0)
    used = i < nused_ref[0]

    @pl.when(jnp.logical_not(used))
    def _():
        ys_ref[...] = jnp.zeros_like(ys_ref)

    new_expert = jnp.logical_or(i == 0, texp_ref[i] != texp_ref[jnp.maximum(i - 1, 0)])

    @pl.when(jnp.logical_and(used, new_expert))
    def _():
        wg_bf[...] = wg_ref[0].astype(BF16)
        wu_bf[...] = wu_ref[0].astype(BF16)
        wd_bf[...] = wd_ref[0].astype(BF16)

    @pl.when(used)
    def _():
        d = wg_ref.shape[1]
        x = _load_token_tiles(xs_ref, 0, MOE_ROW_TILE, d).astype(BF16)
        hg = jnp.dot(x, wg_bf[...], preferred_element_type=F32)
        hu = jnp.dot(x, wu_bf[...], preferred_element_type=F32)
        h = (hg * jax.nn.sigmoid(hg)) * hu
        _store_token_tiles(ys_ref, 0, jnp.dot(h.astype(BF16), wd_bf[...],
                                              preferred_element_type=F32))


def _experts(xs, tile_expert, n_used, w_gate, w_up, w_down):
    _, d, dff = w_gate.shape
    tile_rows = MOE_ROW_TILE * d // LANES
    tile = lambda i, te, nu: (jnp.minimum(i, nu[0] - 1), 0)
    every = lambda i, te, nu: (i, 0)
    wsel = lambda i, te, nu: (te[jnp.minimum(i, nu[0] - 1)], 0, 0)
    return pl.pallas_call(
        _experts_kernel,
        grid_spec=pltpu.PrefetchScalarGridSpec(
            num_scalar_prefetch=2,
            grid=(xs.shape[0] // tile_rows,),
            in_specs=[
                pl.BlockSpec((tile_rows, LANES), tile),
                pl.BlockSpec((1, d, dff), wsel),
                pl.BlockSpec((1, d, dff), wsel),
                pl.BlockSpec((1, dff, d), wsel),
            ],
            out_specs=pl.BlockSpec((tile_rows, LANES), every),
            scratch_shapes=[pltpu.VMEM((d, dff), BF16), pltpu.VMEM((d, dff), BF16),
                            pltpu.VMEM((dff, d), BF16)],
        ),
        out_shape=jax.ShapeDtypeStruct(xs.shape, F32),
        compiler_params=_params(("arbitrary",)),
        name="moe_experts",
    )(tile_expert, n_used, xs, w_gate, w_up, w_down)


def _combine_kernel(pos_ref, posn_ref, y_ref, gate_ref, g_ref, b_ref, ys_hbm, o_ref,
                    buf, sem, *, tmc, tr):
    i = pl.program_id(0)
    n = pl.num_programs(0)
    d = o_ref.shape[1]
    half = tmc * tr
    slot_rows = 2 * half

    def gather(p_ref, slot):
        base = slot * slot_rows
        def issue(g, c):
            r0 = g * DMA_UNROLL
            ps = [[p_ref[0, 0, k * tmc + r0 + j] for k in range(2)] for j in range(DMA_UNROLL)]
            for j in range(DMA_UNROLL):
                for k in range(2):
                    src = ys_hbm.at[pl.ds(pl.multiple_of(ps[j][k], tr), tr)]
                    dst = buf.at[pl.ds(pl.multiple_of(base + k * half + (r0 + j) * tr, tr), tr)]
                    pltpu.make_async_copy(src, dst, sem.at[slot]).start(priority=k)
            return c
        lax.fori_loop(0, tmc // DMA_UNROLL, issue, 0)

    @pl.when(i == 0)
    def _():
        gather(pos_ref, 0)

    @pl.when(i + 1 < n)
    def _():
        gather(posn_ref, (i + 1) % 2)

    slot = i % 2
    base = pl.multiple_of(slot * slot_rows, slot_rows)
    _row_copy_wait(ys_hbm, buf.at[pl.ds(base, slot_rows)], sem.at[slot], slot_rows)
    gate = gate_ref[...]
    ffn = (gate[:, 0:1] * _load_token_tiles(buf, base, tmc, d)
           + gate[:, 1:2] * _load_token_tiles(buf, base + half, tmc, d))
    y = _load_token_tiles(y_ref, 0, tmc, d)
    o_ref[...] = _layer_norm_rows(DEEPNORM_ALPHA * y + ffn, g_ref[...], b_ref[...])


def _combine(ys, pos, y_tok, gates, ln_g, ln_b, *, tmc, tr):
    d = tr * LANES
    n_tok = y_tok.shape[0] // tr
    n = n_tok // tmc
    rows = lambda i: (i, 0)
    fixed = lambda i: (0, 0)
    return pl.pallas_call(
        functools.partial(_combine_kernel, tmc=tmc, tr=tr),
        grid=(n,),
        in_specs=[
            pl.BlockSpec((1, 1, 2 * tmc), lambda i: (i, 0, 0), memory_space=pltpu.SMEM),
            pl.BlockSpec((1, 1, 2 * tmc), lambda i: (jnp.minimum(i + 1, n - 1), 0, 0),
                         memory_space=pltpu.SMEM),
            pl.BlockSpec((tmc * tr, LANES), rows),
            pl.BlockSpec((tmc, LANES), rows),
            pl.BlockSpec((1, d), fixed),
            pl.BlockSpec((1, d), fixed),
            pl.BlockSpec(memory_space=pl.ANY),
        ],
        out_specs=pl.BlockSpec((tmc, d), rows),
        out_shape=jax.ShapeDtypeStruct((n_tok, d), F32),
        scratch_shapes=[pltpu.VMEM((2 * 2 * tmc * tr, LANES), F32),
                        pltpu.SemaphoreType.DMA((2,))],
        compiler_params=_params(("arbitrary",)),
        name="moe_combine",
    )(pos, pos, y_tok, gates, ln_g.reshape(1, d), ln_b.reshape(1, d), ys)


def _tile_positions(route, off, t, tr):
    e = route[:, 0:2, :]
    hit = e[..., None] == jnp.arange(N_EXPERTS, dtype=jnp.int32)
    pos = (jnp.sum(jnp.where(hit, off, 0), axis=-1) + route[:, 2:4, :]) * tr
    pos = pos.transpose(1, 0, 2).reshape(2, -1, t)
    return pos.transpose(1, 0, 2).reshape(-1, 1, 2 * t)


def _moe_sparse(y_tok, route, gates, counts, w_gate, w_up, w_down, ln_g, ln_b, *, tmd, tmc):
    d = w_gate.shape[1]
    tr = d // LANES
    n_tok = y_tok.shape[0] // tr
    cnt = counts[:, 0].astype(jnp.int32)
    padded = (cnt + MOE_ROW_TILE - 1) // MOE_ROW_TILE * MOE_ROW_TILE
    off_end = jnp.cumsum(padded)
    off = off_end - padded
    n_sorted = 2 * n_tok + N_EXPERTS * MOE_ROW_TILE
    n_tiles = n_sorted // MOE_ROW_TILE
    n_used = (off_end[-1:] // MOE_ROW_TILE).astype(jnp.int32)
    tile_start = jnp.arange(n_tiles, dtype=jnp.int32) * MOE_ROW_TILE
    tile_expert = jnp.minimum(jnp.sum(tile_start[:, None] >= off_end[None, :], axis=1),
                              N_EXPERTS - 1).astype(jnp.int32)
    xs = _dispatch(y_tok, _tile_positions(route, off, tmd, tr), off_end.astype(jnp.int32),
                   n_sorted=n_sorted, tmd=tmd, tr=tr)
    ys = _experts(xs, tile_expert, n_used, w_gate, w_up, w_down)
    return _combine(ys, _tile_positions(route, off, tmc, tr), y_tok, gates, ln_g, ln_b,
                    tmc=tmc, tr=tr)


POOL_HALO = 32


def _inproj1_kernel(x_ref, w_ref, pw_ref, ps_ref, qt_ref, k_ref, vt_ref, d_ref, halo_ref,
                    *, sb_w):
    i = pl.program_id(1)
    x = x_ref[0].astype(BF16)
    tm = x.shape[0]
    q = jnp.dot(x, w_ref[:, 0:sb_w], preferred_element_type=F32) * ATTN_SCALE
    qt_ref[0] = q.T.astype(BF16)
    k_ref[0] = jnp.dot(x, w_ref[:, sb_w:2 * sb_w], preferred_element_type=F32).astype(BF16)
    v = jnp.dot(x, w_ref[:, 2 * sb_w:3 * sb_w], preferred_element_type=F32)
    vt_ref[0] = v.T.astype(BF16)
    u = jnp.dot(x, w_ref[:, 3 * sb_w:], preferred_element_type=F32)

    @pl.when(i == 0)
    def _():
        halo_ref[...] = jnp.zeros_like(halo_ref)

    tpos = i * tm + lax.broadcasted_iota(jnp.int32, (tm, 1), 0)
    outs = []
    for g, win in enumerate(POOL_WINDOWS):
        cs = slice(g * POOL_CH, (g + 1) * POOL_CH)
        cur = u[:, cs]
        run = jnp.concatenate([halo_ref[:, cs], cur], axis=0)
        span = 1
        while span < win:
            n = run.shape[0] - span
            run = run[span:span + n] + run[0:n]
            span *= 2
        off = POOL_HALO - (win - 1)
        wsum = run[off:off + tm]
        count = jnp.minimum(tpos + 1, win).astype(F32)
        pooled = wsum / count - cur
        mixed = jnp.dot(pooled.astype(BF16), pw_ref[g], preferred_element_type=F32)
        outs.append(mixed * ps_ref[:, cs])
    halo_ref[...] = u[tm - POOL_HALO:tm, :]
    d_ref[0] = jnp.concatenate(outs, axis=1).astype(d_ref.dtype)


def _inproj1(x, w_bf, pool_w_bf, pool_scale, *, sb_w, tm):
    bsz, n_pos, d = x.shape
    n_in = w_bf.shape[1]
    pool_w = n_in - 3 * sb_w
    n_g = len(POOL_WINDOWS)
    return pl.pallas_call(
        functools.partial(_inproj1_kernel, sb_w=sb_w),
        grid=(bsz, n_pos // tm),
        in_specs=[
            pl.BlockSpec((1, tm, d), lambda b, i: (b, i, 0)),
            pl.BlockSpec((d, n_in), lambda b, i: (0, 0)),
            pl.BlockSpec((n_g, POOL_CH, POOL_CH), lambda b, i: (0, 0, 0)),
            pl.BlockSpec((1, pool_w), lambda b, i: (0, 0)),
        ],
        out_specs=[
            pl.BlockSpec((1, sb_w, tm), lambda b, i: (b, 0, i)),
            pl.BlockSpec((1, tm, sb_w), lambda b, i: (b, i, 0)),
            pl.BlockSpec((1, sb_w, tm), lambda b, i: (b, 0, i)),
            pl.BlockSpec((1, tm, pool_w), lambda b, i: (b, i, 0)),
        ],
        out_shape=[
            jax.ShapeDtypeStruct((bsz, sb_w, n_pos), BF16),
            jax.ShapeDtypeStruct((bsz, n_pos, sb_w), BF16),
            jax.ShapeDtypeStruct((bsz, sb_w, n_pos), BF16),
            jax.ShapeDtypeStruct((bsz, n_pos, pool_w), BF16),
        ],
        scratch_shapes=[pltpu.VMEM((POOL_HALO, pool_w), F32)],
        compiler_params=_params(("parallel", "arbitrary")),
        name="inproj1",
    )(x, w_bf, pool_w_bf, pool_scale.astype(F32).reshape(1, pool_w))


def _sb_attn_kernel(qt_ref, k_ref, vt_ref, o_ref, za_ref, zb_ref, *, tq, tk):
    i = pl.program_id(2)
    qt = qt_ref[0]
    zeros = jnp.zeros((HEAD_DIM, tq), BF16)
    qpads = (jnp.concatenate([qt[:HEAD_DIM], zeros], axis=0),
             jnp.concatenate([zeros, qt[HEAD_DIM:]], axis=0))
    cb = min(SB_SCAN_ROWS, tk)
    upper = (lax.broadcasted_iota(jnp.int32, (cb, cb), 1)
             > lax.broadcasted_iota(jnp.int32, (cb, cb), 0)).astype(BF16)

    heads = range(2)

    def scores(j, buf):
        kj = k_ref[0, pl.ds(pl.multiple_of(j * tk, tk), tk), :]
        for hh in heads:
            buf[hh] = jnp.dot(kj, qpads[hh], preferred_element_type=F32)

    def consume(j, buf, carry, masked, lane0=0):
        k0 = pl.multiple_of(j * tk, tk)
        nq = tq - lane0
        whole = carry
        carry = [(acc[:, lane0:], later[:, lane0:]) for acc, later in whole]
        if masked:
            key = k0 + lax.broadcasted_iota(jnp.int32, (tk, nq), 0)
            qpos = i * tq + lane0 + lax.broadcasted_iota(jnp.int32, (tk, nq), 1)
            keep = key < qpos
        zs = [buf[hh, :, lane0:tq] for hh in heads]
        sps = [jnp.maximum(z, 0.0) + jnp.log(1.0 + jnp.exp2(jnp.abs(z) * (-LOG2E))) for z in zs]
        if masked:
            sps = [jnp.where(keep, sp, 0.0) for sp in sps]
        his = [sp.astype(BF16) for sp in sps]
        laters = [carry[hh][1] for hh in heads]
        pieces = [[None] * (tk // cb) for _ in heads]
        for c in reversed(range(tk // cb)):
            rs = slice(c * cb, (c + 1) * cb)
            sufs = [jnp.dot(upper, his[hh][rs], preferred_element_type=F32)
                    for hh in heads]
            for hh in heads:
                spc = sps[hh][rs]
                pieces[hh][c] = jnp.exp2((zs[hh][rs] - (spc + sufs[hh] + laters[hh])) * LOG2E)
                laters[hh] = laters[hh] + sufs[hh][0:1] + spc[0:1]
        out = []
        for hh in heads:
            a = jnp.concatenate(pieces[hh], axis=0)
            if masked:
                a = jnp.where(keep, a, 0.0)
            vj = vt_ref[0, hh * HEAD_DIM:(hh + 1) * HEAD_DIM, pl.ds(k0, tk)]
            acc = carry[hh][0] + jnp.dot(vj, a.astype(BF16), preferred_element_type=F32)
            later = laters[hh]
            if lane0:
                acc = jnp.concatenate([whole[hh][0][:, :lane0], acc], axis=1)
                later = jnp.concatenate([whole[hh][1][:, :lane0], later], axis=1)
            out.append((acc, later))
        return tuple(out)

    init = (jnp.zeros((HEAD_DIM, tq), F32), jnp.zeros((1, tq), F32))
    carry = (init, init)
    n_diag = tq // tk
    last = (i + 1) * n_diag - 1
    n_old = i * n_diag
    bufs = (za_ref, zb_ref)
    scores(last, bufs[0])
    for d in range(n_diag):
        nxt = last - d - 1 if d + 1 < n_diag else jnp.maximum(n_old - 1, 0)
        scores(nxt, bufs[(d + 1) % 2])
        carry = consume(last - d, bufs[d % 2], carry, True, lane0=tq - (d + 1) * tk)
    first_old = bufs[n_diag % 2]
    spare = bufs[(n_diag + 1) % 2]

    def reach(c):
        return jnp.min(jnp.minimum(c[0][1], c[1][1]))

    def live(state):
        t, r, _ = state
        return jnp.logical_and(t < n_old, r < SB_UNDERFLOW)

    def older(state):
        t, _, c = state
        scores(n_old - 1 - t, spare)
        c = consume(n_old - 1 - t, spare, c, False)
        return t + 1, reach(c), c

    r0 = reach(carry)
    carry = lax.cond(jnp.logical_and(n_old > 0, r0 < SB_UNDERFLOW),
                     lambda c: consume(n_old - 1, first_old, c, False), lambda c: c, carry)
    _, _, carry = lax.while_loop(live, older, (jnp.int32(1), reach(carry), carry))
    o_ref[0] = jnp.concatenate([carry[0][0], carry[1][0]], axis=0).T.astype(o_ref.dtype)


def _sb_attention(qt, k, vt, *, tq, tk):
    bsz, sb_w, n_pos = qt.shape
    return pl.pallas_call(
        functools.partial(_sb_attn_kernel, tq=tq, tk=tk),
        grid=(bsz, sb_w // LANES, n_pos // tq),
        in_specs=[
            pl.BlockSpec((1, LANES, tq), lambda b, h, i: (b, h, i)),
            pl.BlockSpec((1, n_pos, LANES), lambda b, h, i: (b, 0, h)),
            pl.BlockSpec((1, LANES, n_pos), lambda b, h, i: (b, h, 0)),
        ],
        out_specs=pl.BlockSpec((1, tq, LANES), lambda b, h, i: (b, i, h)),
        out_shape=jax.ShapeDtypeStruct((bsz, n_pos, sb_w), BF16),
        scratch_shapes=[pltpu.VMEM((2, tk, tq), F32), pltpu.VMEM((2, tk, tq), F32)],
        compiler_params=_params(("parallel", "parallel", "arbitrary")),
        name="sb_attention",
    )(qt, k, vt)


def _rope_tables(n_pos):
    inv = ROPE_THETA ** (-jnp.arange(0, HEAD_DIM, 2, dtype=F32) / HEAD_DIM)
    ang = jnp.arange(n_pos, dtype=F32)[:, None] * inv[None, :]
    ang = jnp.concatenate([ang, ang], axis=-1)
    return jnp.cos(ang), jnp.sin(ang)


def _lambda_init(layer_idx):
    return 0.8 - 0.6 * math.exp(-0.3 * layer_idx)


def kernel(x, router_w, router_bias, l0_in_proj, l0_conv_w, l0_conv_b, l0_conv_ln_g, l0_conv_ln_b, l0_lambda_q1, l0_lambda_k1, l0_lambda_q2, l0_lambda_k2, l0_subln_g, l0_out_proj, l0_ln_mix_g, l0_ln_mix_b, l0_w_gate, l0_w_up, l0_w_down, l0_ln_ffn_g, l0_ln_ffn_b, l1_in_proj, l1_pool_w, l1_pool_scale, l1_out_proj, l1_ln_mix_g, l1_ln_mix_b, l1_w_gate, l1_w_up, l1_w_down, l1_ln_ffn_g, l1_ln_ffn_b):
    bsz, n_pos, d = x.shape
    n_tok = bsz * n_pos
    x = x.astype(F32)

    conv_ch = l0_conv_w.shape[1]
    dv = l0_subln_g.shape[0]
    n_diff = (l0_out_proj.shape[0] - conv_ch) // dv
    qk_w = n_diff * 2 * HEAD_DIM
    v_w = n_diff * dv

    tm = min(512, n_pos)
    tq_diff = min(512, n_pos)
    tq_sb = min(512, n_pos)
    tk_sb = min(256, n_pos)
    ts = min(256, n_pos)
    tmc = min(256, n_pos)
    tmd = min(2048, n_tok)
    tm_out = min(1024, n_tok)

    cos, sin = _rope_tables(n_pos)
    sign = jnp.where(jnp.arange(HEAD_DIM) < HEAD_DIM // 2, -1.0, 1.0).astype(F32)
    cos128 = jnp.concatenate([cos, cos], axis=-1)
    sins128 = jnp.concatenate([sin * sign, sin * sign], axis=-1)
    cost = cos.T
    sint = sin.T

    glu, qt, k, vt = _inproj0(x, l0_in_proj.astype(BF16), cos128, sins128, cost, sint,
                              conv_ch=conv_ch, qk_w=qk_w, v_w=v_w, tm=tm)
    a = _conv_module(glu, l0_conv_w.astype(F32), l0_conv_b.astype(F32),
                     l0_conv_ln_g.astype(F32), l0_conv_ln_b.astype(F32), ts=ts)
    lam_init = _lambda_init(0)
    lam = (jnp.exp(jnp.sum(l0_lambda_q1.astype(F32) * l0_lambda_k1.astype(F32)))
           - jnp.exp(jnp.sum(l0_lambda_q2.astype(F32) * l0_lambda_k2.astype(F32))) + lam_init)
    o = _diff_attention(qt, k, vt, lam, l0_subln_g, n_heads=n_diff, lam_init=lam_init,
                        tq=tq_diff, tk=tq_diff)
    xf = x.reshape(n_tok, d)
    y, route, gates, counts = _outproj_ln_route(
        a.reshape(n_tok, conv_ch), o.reshape(n_tok, v_w), l0_out_proj, xf,
        l0_ln_mix_g.astype(F32), l0_ln_mix_b.astype(F32), router_w, router_bias, tm=tm_out)
    xf = _moe_sparse(y, route, gates, counts, l0_w_gate, l0_w_up, l0_w_down,
                     l0_ln_ffn_g.astype(F32), l0_ln_ffn_b.astype(F32), tmd=tmd, tmc=tmc)

    sb_w = l1_in_proj.shape[1] - len(POOL_WINDOWS) * POOL_CH
    sb_w = sb_w // 3
    qt, k, vt, dpool = _inproj1(xf.reshape(bsz, n_pos, d), l1_in_proj.astype(BF16),
                                l1_pool_w.astype(BF16), l1_pool_scale, sb_w=sb_w, tm=tm)
    o = _sb_attention(qt, k, vt, tq=tq_sb, tk=tk_sb)
    y, route, gates, counts = _outproj_ln_route(
        o.reshape(n_tok, sb_w), dpool.reshape(n_tok, -1), l1_out_proj, xf,
        l1_ln_mix_g.astype(F32), l1_ln_mix_b.astype(F32), router_w, router_bias, tm=tm_out)
    xf = _moe_sparse(y, route, gates, counts, l1_w_gate, l1_w_up, l1_w_down,
                     l1_ln_ffn_g.astype(F32), l1_ln_ffn_b.astype(F32), tmd=tmd, tmc=tmc)
    return xf.reshape(bsz, n_pos, d)
```

```python
import functools
import math

import jax
import jax.numpy as jnp
from jax import lax
from jax.experimental import pallas as pl
from jax.experimental.pallas import tpu as pltpu

F32 = jnp.float32
BF16 = jnp.bfloat16

DEPTH = 2
HEAD_DIM = 64
ROPE_THETA = 10000.0
LN_EPS = 1e-5
RMS_EPS = 1e-5
DEEPNORM_ALPHA = (2 * DEPTH) ** 0.25
CONV_WIDTH = 31
POOL_WINDOWS = (2, 4, 8, 16)
POOL_CH = 128
N_EXPERTS = 16
N_GROUPS = 4
EXPERTS_PER_GROUP = N_EXPERTS // N_GROUPS
ATTN_SCALE = HEAD_DIM ** -0.5
LOG2E = math.log2(math.e)

LANES = 128
SUBLANES = 8
MIB = 1024 * 1024
VMEM_LIMIT = 52 * MIB

PROJ_ROWS = 512
OUTPROJ_ROWS = 1024
DIFF_TILE = 512
SB_QUERY_TILE = 512
SB_KEY_TILE = 256
CONV_ROWS = 256
DISPATCH_ROWS = 2048
COMBINE_ROWS = 256

NEG_BIG = -1e30
SB_SCAN_ROWS = 256
SB_UNDERFLOW = 110.0


def _params(semantics, vmem=VMEM_LIMIT):
    return pltpu.CompilerParams(dimension_semantics=semantics, vmem_limit_bytes=vmem)


def _layer_norm_rows(h, g, b):
    mu = jnp.mean(h, axis=-1, keepdims=True)
    d = h - mu
    var = jnp.mean(d * d, axis=-1, keepdims=True)
    return d * lax.rsqrt(var + LN_EPS) * g + b


def _load_token_tiles(ref, row0, n, d):
    k = d // LANES
    return jnp.concatenate([ref[pl.ds(row0 + c, n, stride=k), :] for c in range(k)], axis=1)


def _store_token_tiles(ref, row0, val):
    n, d = val.shape
    k = d // LANES
    for c in range(k):
        ref[pl.ds(row0 + c, n, stride=k), :] = val[:, c * LANES:(c + 1) * LANES]


def _inproj0_kernel(x_ref, w_ref, cos_ref, sins_ref, cost_ref, sint_ref,
                    glu_ref, qt_ref, k_ref, vt_ref, *, conv_ch, qk_w):
    x = x_ref[0].astype(BF16)
    c0, c1, c2, c3 = conv_ch, 2 * conv_ch, 2 * conv_ch + qk_w, 2 * conv_ch + 2 * qk_w

    a_val = jnp.dot(x, w_ref[:, 0:c0], preferred_element_type=F32)
    a_gate = jnp.dot(x, w_ref[:, c0:c1], preferred_element_type=F32)
    glu_ref[0] = a_val * jax.nn.sigmoid(a_gate)

    dq = jnp.dot(x, w_ref[:, c1:c2], preferred_element_type=F32)
    dqt = dq.T
    cost = cost_ref[...]
    sint = sint_ref[...]
    half = HEAD_DIM // 2
    pieces = []
    for blk in range(qk_w // HEAD_DIM):
        u = dqt[blk * HEAD_DIM:(blk + 1) * HEAD_DIM]
        rot = jnp.concatenate([-u[half:], u[:half]], axis=0)
        pieces.append((u * cost + rot * sint) * (ATTN_SCALE * LOG2E))
    qt_ref[0] = jnp.concatenate(pieces, axis=0).astype(BF16)

    dk = jnp.dot(x, w_ref[:, c2:c3], preferred_element_type=F32)
    cos = cos_ref[...]
    sins = sins_ref[...]
    lane = lax.broadcasted_iota(jnp.int32, cos.shape, 1)
    first = (lane & (HEAD_DIM - 1)) < half
    kparts = []
    for h in range(qk_w // LANES):
        u = dk[:, h * LANES:(h + 1) * LANES]
        rot = jnp.where(first, pltpu.roll(u, LANES - half, 1), pltpu.roll(u, half, 1))
        kparts.append(u * cos + rot * sins)
    k_ref[0] = jnp.concatenate(kparts, axis=1).astype(BF16)

    dv = jnp.dot(x, w_ref[:, c3:], preferred_element_type=F32)
    vt_ref[0] = dv.T.astype(BF16)


def _inproj0(x, w_bf, cos128, sins128, cost, sint, *, conv_ch, qk_w, v_w, tm):
    bsz, n_pos, d = x.shape
    n_in = w_bf.shape[1]
    kern = functools.partial(_inproj0_kernel, conv_ch=conv_ch, qk_w=qk_w)
    return pl.pallas_call(
        kern,
        grid=(bsz, n_pos // tm),
        in_specs=[
            pl.BlockSpec((1, tm, d), lambda b, i: (b, i, 0)),
            pl.BlockSpec((d, n_in), lambda b, i: (0, 0)),
            pl.BlockSpec((tm, LANES), lambda b, i: (i, 0)),
            pl.BlockSpec((tm, LANES), lambda b, i: (i, 0)),
            pl.BlockSpec((HEAD_DIM, tm), lambda b, i: (0, i)),
            pl.BlockSpec((HEAD_DIM, tm), lambda b, i: (0, i)),
        ],
        out_specs=[
            pl.BlockSpec((1, tm, conv_ch), lambda b, i: (b, i, 0)),
            pl.BlockSpec((1, qk_w, tm), lambda b, i: (b, 0, i)),
            pl.BlockSpec((1, tm, qk_w), lambda b, i: (b, i, 0)),
            pl.BlockSpec((1, v_w, tm), lambda b, i: (b, 0, i)),
        ],
        out_shape=[
            jax.ShapeDtypeStruct((bsz, n_pos, conv_ch), F32),
            jax.ShapeDtypeStruct((bsz, qk_w, n_pos), BF16),
            jax.ShapeDtypeStruct((bsz, n_pos, qk_w), BF16),
            jax.ShapeDtypeStruct((bsz, v_w, n_pos), BF16),
        ],
        compiler_params=_params(("parallel", "arbitrary")),
        name="inproj0",
    )(x, w_bf, cos128, sins128, cost, sint)


CONV_HALO = 32
CONV_ROW_BLOCK = 64


def _conv_kernel(a_ref, cw_ref, cb_ref, g_ref, b_ref, o_ref, win_ref, acc_ref, *, ts):
    i = pl.program_id(1)
    t0 = pl.multiple_of(i * ts, ts)
    h0 = pl.multiple_of(jnp.maximum(t0 - CONV_HALO, 0), CONV_HALO)
    n_ch = a_ref.shape[2]
    n_shift = ts + CONV_HALO - SUBLANES
    for c in range(n_ch // LANES):
        cs = slice(c * LANES, (c + 1) * LANES)
        cur = a_ref[0, pl.ds(t0, ts), cs]
        halo = a_ref[0, pl.ds(h0, CONV_HALO), cs]
        halo = jnp.where(i > 0, halo, 0.0)
        xw = jnp.concatenate([halo, cur], axis=0)
        win_ref[c, 0] = xw
        for r in range(1, SUBLANES):
            win_ref[c, r, 0:n_shift] = xw[r:r + n_shift]
        def row_block(rb, carry, c=c, cs=cs, full=xw.shape[0]):
            base = pl.multiple_of(rb * CONV_ROW_BLOCK, CONV_ROW_BLOCK)
            acc = jnp.zeros((CONV_ROW_BLOCK, LANES), F32) + cb_ref[:, cs]
            for r in range(SUBLANES):
                rows = full if r == 0 else n_shift
                for a in range(CONV_HALO // SUBLANES + 1):
                    w = SUBLANES * a + r - (CONV_HALO - CONV_WIDTH + 1)
                    if 0 <= w < CONV_WIDTH and SUBLANES * a + ts <= rows:
                        tap = win_ref[c, r, pl.ds(base + SUBLANES * a, CONV_ROW_BLOCK)]
                        acc = acc + tap * cw_ref[w:w + 1, cs]
            acc_ref[pl.ds(base, CONV_ROW_BLOCK), cs] = acc
            return carry
        lax.fori_loop(0, ts // CONV_ROW_BLOCK, row_block, 0)
    y = _layer_norm_rows(acc_ref[...], g_ref[...], b_ref[...])
    o_ref[0] = (y * jax.nn.sigmoid(y)).astype(o_ref.dtype)


def _conv_module(a, conv_w, conv_b, ln_g, ln_b, *, ts):
    bsz, n_pos, n_ch = a.shape
    return pl.pallas_call(
        functools.partial(_conv_kernel, ts=ts),
        grid=(bsz, n_pos // ts),
        in_specs=[
            pl.BlockSpec((1, n_pos, n_ch), lambda b, i: (b, 0, 0)),
            pl.BlockSpec((CONV_WIDTH, n_ch), lambda b, i: (0, 0)),
            pl.BlockSpec((1, n_ch), lambda b, i: (0, 0)),
            pl.BlockSpec((1, n_ch), lambda b, i: (0, 0)),
            pl.BlockSpec((1, n_ch), lambda b, i: (0, 0)),
        ],
        out_specs=pl.BlockSpec((1, ts, n_ch), lambda b, i: (b, i, 0)),
        out_shape=jax.ShapeDtypeStruct((bsz, n_pos, n_ch), BF16),
        scratch_shapes=[pltpu.VMEM((n_ch // LANES, SUBLANES, ts + CONV_HALO, LANES), F32),
                        pltpu.VMEM((ts, n_ch), F32)],
        compiler_params=_params(("parallel", "arbitrary")),
        name="conv_module",
    )(a, conv_w, conv_b.reshape(1, n_ch), ln_g.reshape(1, n_ch), ln_b.reshape(1, n_ch))


def _diff_attn_kernel(lam_ref, qt_ref, k_ref, vt_ref, g_ref, o_ref, sa_ref, sb_ref,
                      *, tq, tk, out_scale):
    i = pl.program_id(2)
    lam = lam_ref[0, 0]
    qt = qt_ref[0]
    zeros = jnp.zeros((HEAD_DIM, tq), BF16)
    qa = jnp.concatenate([qt[:HEAD_DIM], zeros], axis=0)
    qb = jnp.concatenate([zeros, qt[HEAD_DIM:]], axis=0)
    dv = vt_ref.shape[1]

    def online(s, vj, m, l, acc):
        m_new = jnp.maximum(m, jnp.max(s, axis=0, keepdims=True))
        alpha = jnp.exp2(m - m_new)
        p = jnp.exp2(s - m_new)
        l_new = alpha * l + jnp.sum(p, axis=0, keepdims=True)
        acc_new = alpha * acc + jnp.dot(vj, p.astype(BF16), preferred_element_type=F32)
        return m_new, l_new, acc_new

    def scores(j, buf):
        kj = k_ref[0, pl.ds(pl.multiple_of(j * tk, tk), tk), :]
        buf[0] = jnp.dot(kj, qa, preferred_element_type=F32)
        buf[1] = jnp.dot(kj, qb, preferred_element_type=F32)

    def consume(j, buf, carry, masked):
        if masked:
            return consume_diagonal(j, buf, carry)
        m1, l1, acc1, m2, l2, acc2 = carry
        vj = vt_ref[0, :, pl.ds(pl.multiple_of(j * tk, tk), tk)]
        m1, l1, acc1 = online(buf[0], vj, m1, l1, acc1)
        m2, l2, acc2 = online(buf[1], vj, m2, l2, acc2)
        return m1, l1, acc1, m2, l2, acc2

    def consume_diagonal(j, buf, carry):
        hk = tk // 2
        for key_lo, lane0 in ((0, 0), (hk, tq - hk)):
            nq = tq - lane0
            k0 = pl.multiple_of(j * tk + key_lo, hk)
            vj = vt_ref[0, :, pl.ds(k0, hk)]
            key = k0 + lax.broadcasted_iota(jnp.int32, (hk, nq), 0)
            qpos = i * tq + lane0 + lax.broadcasted_iota(jnp.int32, (hk, nq), 1)
            keep = key <= qpos
            new = []
            for mp in range(2):
                m, l, acc = carry[3 * mp:3 * mp + 3]
                s = jnp.where(keep, buf[mp, key_lo:key_lo + hk, lane0:tq], NEG_BIG)
                part = online(s, vj, m[:, lane0:], l[:, lane0:], acc[:, lane0:])
                if lane0:
                    part = [jnp.concatenate([old[:, :lane0], upd], axis=1)
                            for old, upd in zip((m, l, acc), part)]
                new.extend(part)
            carry = tuple(new)
        return carry

    row = jnp.full((1, tq), NEG_BIG, F32)
    zrow = jnp.zeros((1, tq), F32)
    zacc = jnp.zeros((dv, tq), F32)
    carry = (row, zrow, zacc, row, zrow, zacc)

    scores(0, sa_ref)

    def pair(u, c):
        scores(2 * u + 1, sb_ref)
        c = consume(2 * u, sa_ref, c, False)
        scores(2 * u + 2, sa_ref)
        return consume(2 * u + 1, sb_ref, c, False)

    carry = lax.fori_loop(0, lax.shift_right_logical(i, 1), pair, carry)

    def odd_tail(c):
        scores(i, sb_ref)
        c = consume(i - 1, sa_ref, c, False)
        return consume(i, sb_ref, c, True)

    def even_tail(c):
        return consume(i, sa_ref, c, True)

    m1, l1, acc1, m2, l2, acc2 = lax.cond((i & 1) == 1, odd_tail, even_tail, carry)

    o = acc1 / l1 - lam * (acc2 / l2)
    o = o * lax.rsqrt(jnp.mean(o * o, axis=0, keepdims=True) + RMS_EPS)
    o = o * g_ref[...] * out_scale
    o_ref[0] = o.T.astype(o_ref.dtype)


def _diff_attention(qt, k, vt, lam, subln_g, *, n_heads, lam_init, tq, tk):
    bsz, _, n_pos = qt.shape
    dv = vt.shape[1] // n_heads
    kern = functools.partial(_diff_attn_kernel, tq=tq, tk=tk, out_scale=1.0 - lam_init)
    return pl.pallas_call(
        kern,
        grid=(bsz, n_heads, n_pos // tq),
        in_specs=[
            pl.BlockSpec(memory_space=pltpu.SMEM),
            pl.BlockSpec((1, 2 * HEAD_DIM, tq), lambda b, h, i: (b, h, i)),
            pl.BlockSpec((1, n_pos, 2 * HEAD_DIM), lambda b, h, i: (b, 0, h)),
            pl.BlockSpec((1, dv, n_pos), lambda b, h, i: (b, h, 0)),
            pl.BlockSpec((dv, 1), lambda b, h, i: (0, 0)),
        ],
        out_specs=pl.BlockSpec((1, tq, dv), lambda b, h, i: (b, i, h)),
        out_shape=jax.ShapeDtypeStruct((bsz, n_pos, n_heads * dv), BF16),
        scratch_shapes=[pltpu.VMEM((2, tk, tq), F32), pltpu.VMEM((2, tk, tq), F32)],
        compiler_params=_params(("parallel", "parallel", "arbitrary")),
        name="diff_attention",
    )(lam.reshape(1, 1), qt, k, vt, subln_g.astype(F32).reshape(dv, 1))


def _masked_top2(vals, member, row):
    big = N_EXPERTS
    v = jnp.where(member, vals, -jnp.inf)
    v1 = jnp.max(v, axis=0, keepdims=True)
    i1 = jnp.min(jnp.where(v == v1, row, big), axis=0, keepdims=True)
    v = jnp.where(row == i1, -jnp.inf, v)
    v2 = jnp.max(v, axis=0, keepdims=True)
    i2 = jnp.min(jnp.where(v == v2, row, big), axis=0, keepdims=True)
    return v1, i1, v2, i2


def _top2_route(aff, sel):
    row = lax.broadcasted_iota(jnp.int32, sel.shape, 0)
    group = lax.shift_right_logical(row, EXPERTS_PER_GROUP.bit_length() - 1)
    gbest = None
    for g in range(N_GROUPS):
        v1, _, v2, _ = _masked_top2(sel, group == g, row)
        score = v1 + v2
        if gbest is None:
            gbest, gval = jnp.zeros_like(score, dtype=jnp.int32), score
        else:
            better = score > gval
            gbest = jnp.where(better, g, gbest)
            gval = jnp.where(better, score, gval)
    _, e1, _, e2 = _masked_top2(sel, group == gbest, row)
    a1 = jnp.sum(jnp.where(row == e1, aff, 0.0), axis=0, keepdims=True)
    a2 = jnp.sum(jnp.where(row == e2, aff, 0.0), axis=0, keepdims=True)
    denom = a1 + a2
    return e1, e2, a1 / denom, a2 / denom


ROUTE_ROWS = SUBLANES


def _outproj_kernel(pa_ref, pb_ref, wa_ref, wb_ref, x_ref, g_ref, b_ref, rw_ref, rb_ref,
                    y_ref, route_ref, gate_ref, count_ref, seen_ref):
    i = pl.program_id(0)

    @pl.when(i == 0)
    def _():
        seen_ref[...] = jnp.zeros_like(seen_ref)

    mix = jnp.dot(pa_ref[...], wa_ref[...], preferred_element_type=F32)
    mix = mix + jnp.dot(pb_ref[...], wb_ref[...], preferred_element_type=F32)
    y = _layer_norm_rows(DEEPNORM_ALPHA * x_ref[...] + mix, g_ref[...], b_ref[...])
    _store_token_tiles(y_ref, 0, y)
    tm = y.shape[0]
    yh = y.astype(BF16)
    yl = (y - yh.astype(F32)).astype(BF16)
    nt = (((1,), (1,)), ((), ()))
    both = lax.dot_general(rw_ref[...], yh, nt, preferred_element_type=F32)
    logits = (both[:N_EXPERTS] + both[N_EXPERTS:]
              + lax.dot_general(rw_ref[0:N_EXPERTS], yl, nt, preferred_element_type=F32))
    aff = jax.nn.sigmoid(logits)
    e1, e2, g1, g2 = _top2_route(aff, aff + rb_ref[...])

    erow = lax.broadcasted_iota(jnp.int32, (N_EXPERTS, tm), 0)
    hit = jnp.where((erow == e1) | (erow == e2), 1.0, 0.0)
    upto = (lax.broadcasted_iota(jnp.int32, (tm, tm), 0)
            <= lax.broadcasted_iota(jnp.int32, (tm, tm), 1)).astype(BF16)
    incl = jnp.dot(hit.astype(BF16), upto, preferred_element_type=F32)
    before = seen_ref[:, 0:1] + incl - hit
    r1 = jnp.sum(jnp.where(erow == e1, before, 0.0), axis=0, keepdims=True).astype(jnp.int32)
    r2 = jnp.sum(jnp.where(erow == e2, before, 0.0), axis=0, keepdims=True).astype(jnp.int32)
    seen_ref[...] = seen_ref[...] + jnp.sum(hit, axis=1, keepdims=True)
    count_ref[...] = seen_ref[...]

    rrow = lax.broadcasted_iota(jnp.int32, (ROUTE_ROWS, tm), 0)
    route_ref[0] = jnp.where(rrow == 0, e1, jnp.where(rrow == 1, e2,
                             jnp.where(rrow == 2, r1, jnp.where(rrow == 3, r2, 0))))
    grow = lax.broadcasted_iota(jnp.int32, (LANES, tm), 0)
    gate_ref[...] = jnp.where(grow == 0, g1, jnp.where(grow == 1, g2, 0.0)).T


def _outproj_ln_route(pa, pb, w_out, x, ln_g, ln_b, router_w, router_bias, *, tm):
    n_tok, d = x.shape
    ka, kb = pa.shape[1], pb.shape[1]
    wa = w_out[:ka].astype(BF16)
    wb = w_out[ka:].astype(BF16)
    rw = router_w.astype(F32).T
    rw_hi = rw.astype(BF16)
    rw_lo = (rw - rw_hi.astype(F32)).astype(BF16)
    rows = lambda i: (i, 0)
    fixed = lambda i: (0, 0)
    return pl.pallas_call(
        _outproj_kernel,
        grid=(n_tok // tm,),
        in_specs=[
            pl.BlockSpec((tm, ka), rows),
            pl.BlockSpec((tm, kb), rows),
            pl.BlockSpec((ka, d), fixed),
            pl.BlockSpec((kb, d), fixed),
            pl.BlockSpec((tm, d), rows),
            pl.BlockSpec((1, d), fixed),
            pl.BlockSpec((1, d), fixed),
            pl.BlockSpec((2 * N_EXPERTS, d), fixed),
            pl.BlockSpec((N_EXPERTS, 1), fixed),
        ],
        out_specs=[
            pl.BlockSpec((tm * d // LANES, LANES), rows),
            pl.BlockSpec((1, ROUTE_ROWS, tm), lambda i: (i, 0, 0)),
            pl.BlockSpec((tm, LANES), rows),
            pl.BlockSpec((N_EXPERTS, LANES), fixed),
        ],
        out_shape=[
            jax.ShapeDtypeStruct((n_tok * d // LANES, LANES), F32),
            jax.ShapeDtypeStruct((n_tok // tm, ROUTE_ROWS, tm), jnp.int32),
            jax.ShapeDtypeStruct((n_tok, LANES), F32),
            jax.ShapeDtypeStruct((N_EXPERTS, LANES), F32),
        ],
        scratch_shapes=[pltpu.VMEM((N_EXPERTS, LANES), F32)],
        compiler_params=_params(("arbitrary",)),
        name="outproj_ln_route",
    )(pa, pb, wa, wb, x, ln_g.reshape(1, d), ln_b.reshape(1, d),
      jnp.concatenate([rw_hi, rw_lo], axis=0), router_bias.astype(F32).reshape(N_EXPERTS, 1))


MOE_ROW_TILE = 512
DMA_UNROLL = 8


def _row_copy_wait(src_hbm, dst, sem, n_rows):
    pltpu.make_async_copy(src_hbm.at[pl.ds(0, n_rows)], dst, sem).wait()


def _dispatch_kernel(offend_ref, pos_ref, y_ref, xs_hbm, zero_ref, sem, zsem, *, tmd, tr):
    i = pl.program_id(0)
    tile_rows = MOE_ROW_TILE * tr

    @pl.when(i == 0)
    def _():
        zero_ref[...] = jnp.zeros_like(zero_ref)
        def tail_copy(e):
            start = pl.multiple_of((offend_ref[e] - MOE_ROW_TILE) * tr, tile_rows)
            return pltpu.make_async_copy(zero_ref, xs_hbm.at[pl.ds(start, tile_rows)], zsem)
        def nonempty(e):
            return offend_ref[e] > (offend_ref[e - 1] if e > 0 else 0)
        for e in range(N_EXPERTS):
            @pl.when(nonempty(e))
            def _():
                tail_copy(e).start()
        for e in range(N_EXPERTS):
            @pl.when(nonempty(e))
            def _():
                tail_copy(e).wait()
        def unused_copy(j):
            return pltpu.make_async_copy(
                zero_ref, xs_hbm.at[pl.ds(pl.multiple_of(j * tile_rows, tile_rows), tile_rows)],
                zsem)
        first_unused = lax.shift_right_logical(offend_ref[N_EXPERTS - 1],
                                               MOE_ROW_TILE.bit_length() - 1)
        n_tiles = xs_hbm.shape[0] // tile_rows
        lax.fori_loop(first_unused, n_tiles, lambda j, c: (unused_copy(j).start(), c)[1], 0)
        lax.fori_loop(first_unused, n_tiles, lambda j, c: (unused_copy(j).wait(), c)[1], 0)

    def issue(g, c):
        r0 = g * DMA_UNROLL
        ps = [[pos_ref[0, 0, k * tmd + r0 + j] for k in range(2)] for j in range(DMA_UNROLL)]
        for j in range(DMA_UNROLL):
            src = y_ref.at[pl.ds(pl.multiple_of((r0 + j) * tr, tr), tr)]
            for k in range(2):
                dst = xs_hbm.at[pl.ds(pl.multiple_of(ps[j][k], tr), tr)]
                pltpu.make_async_copy(src, dst, sem).start(priority=k)
        return c

    lax.fori_loop(0, tmd // DMA_UNROLL, issue, 0)
    for k in range(2):
        pltpu.make_async_copy(y_ref, xs_hbm.at[pl.ds(0, tmd * tr)], sem).wait()


def _dispatch(y_tok, pos, off_end, *, n_sorted, tmd, tr):
    n_tok = y_tok.shape[0] // tr
    return pl.pallas_call(
        functools.partial(_dispatch_kernel, tmd=tmd, tr=tr),
        grid_spec=pltpu.PrefetchScalarGridSpec(
            num_scalar_prefetch=1,
            grid=(n_tok // tmd,),
            in_specs=[
                pl.BlockSpec((1, 1, 2 * tmd), lambda i, off: (i, 0, 0), memory_space=pltpu.SMEM),
                pl.BlockSpec((tmd * tr, LANES), lambda i, off: (i, 0)),
            ],
            out_specs=pl.BlockSpec(memory_space=pl.ANY),
            scratch_shapes=[pltpu.VMEM((MOE_ROW_TILE * tr, LANES), F32),
                            pltpu.SemaphoreType.DMA, pltpu.SemaphoreType.DMA],
        ),
        out_shape=jax.ShapeDtypeStruct((n_sorted * tr, LANES), F32),
        compiler_params=_params(("arbitrary",)),
        name="moe_dispatch",
    )(off_end, pos, y_tok)


def _experts_kernel(texp_ref, nused_ref, xs_ref, wg_ref, wu_ref, wd_ref, ys_ref,
                    wg_bf, wu_bf, wd_bf):
    i = pl.program_id(0)
    used = i < nused_ref[0]

    @pl.when(jnp.logical_not(used))
    def _():
        ys_ref[...] = jnp.zeros_like(ys_ref)

    new_expert = jnp.logical_or(i == 0, texp_ref[i] != texp_ref[jnp.maximum(i - 1, 0)])

    @pl.when(jnp.logical_and(used, new_expert))
    def _():
        wg_bf[...] = wg_ref[0].astype(BF16)
        wu_bf[...] = wu_ref[0].astype(BF16)
        wd_bf[...] = wd_ref[0].astype(BF16)

    @pl.when(used)
    def _():
        d = wg_ref.shape[1]
        x = _load_token_tiles(xs_ref, 0, MOE_ROW_TILE, d).astype(BF16)
        hg = jnp.dot(x, wg_bf[...], preferred_element_type=F32)
        hu = jnp.dot(x, wu_bf[...], preferred_element_type=F32)
        h = (hg * jax.nn.sigmoid(hg)) * hu
        _store_token_tiles(ys_ref, 0, jnp.dot(h.astype(BF16), wd_bf[...],
                                              preferred_element_type=F32))


def _experts(xs, tile_expert, n_used, w_gate, w_up, w_down):
    _, d, dff = w_gate.shape
    tile_rows = MOE_ROW_TILE * d // LANES
    tile = lambda i, te, nu: (jnp.minimum(i, nu[0] - 1), 0)
    every = lambda i, te, nu: (i, 0)
    wsel = lambda i, te, nu: (te[jnp.minimum(i, nu[0] - 1)], 0, 0)
    return pl.pallas_call(
        _experts_kernel,
        grid_spec=pltpu.PrefetchScalarGridSpec(
            num_scalar_prefetch=2,
            grid=(xs.shape[0] // tile_rows,),
            in_specs=[
                pl.BlockSpec((tile_rows, LANES), tile),
                pl.BlockSpec((1, d, dff), wsel),
                pl.BlockSpec((1, d, dff), wsel),
                pl.BlockSpec((1, dff, d), wsel),
            ],
            out_specs=pl.BlockSpec((tile_rows, LANES), every),
            scratch_shapes=[pltpu.VMEM((d, dff), BF16), pltpu.VMEM((d, dff), BF16),
                            pltpu.VMEM((dff, d), BF16)],
        ),
        out_shape=jax.ShapeDtypeStruct(xs.shape, F32),
        compiler_params=_params(("arbitrary",)),
        name="moe_experts",
    )(tile_expert, n_used, xs, w_gate, w_up, w_down)


def _combine_kernel(pos_ref, posn_ref, y_ref, gate_ref, g_ref, b_ref, ys_hbm, o_ref,
                    buf, sem, *, tmc, tr):
    i = pl.program_id(0)
    n = pl.num_programs(0)
    d = o_ref.shape[1]
    half = tmc * tr
    slot_rows = 2 * half

    def gather(p_ref, slot):
        base = slot * slot_rows
        def issue(g, c):
            r0 = g * DMA_UNROLL
            ps = [[p_ref[0, 0, k * tmc + r0 + j] for k in range(2)] for j in range(DMA_UNROLL)]
            for j in range(DMA_UNROLL):
                for k in range(2):
                    src = ys_hbm.at[pl.ds(pl.multiple_of(ps[j][k], tr), tr)]
                    dst = buf.at[pl.ds(pl.multiple_of(base + k * half + (r0 + j) * tr, tr), tr)]
                    pltpu.make_async_copy(src, dst, sem.at[slot]).start(priority=k)
            return c
        lax.fori_loop(0, tmc // DMA_UNROLL, issue, 0)

    @pl.when(i == 0)
    def _():
        gather(pos_ref, 0)

    @pl.when(i + 1 < n)
    def _():
        gather(posn_ref, (i + 1) % 2)

    slot = i % 2
    base = pl.multiple_of(slot * slot_rows, slot_rows)
    _row_copy_wait(ys_hbm, buf.at[pl.ds(base, slot_rows)], sem.at[slot], slot_rows)
    gate = gate_ref[...]
    ffn = (gate[:, 0:1] * _load_token_tiles(buf, base, tmc, d)
           + gate[:, 1:2] * _load_token_tiles(buf, base + half, tmc, d))
    y = _load_token_tiles(y_ref, 0, tmc, d)
    o_ref[...] = _layer_norm_rows(DEEPNORM_ALPHA * y + ffn, g_ref[...], b_ref[...])


def _combine(ys, pos, y_tok, gates, ln_g, ln_b, *, tmc, tr):
    d = tr * LANES
    n_tok = y_tok.shape[0] // tr
    n = n_tok // tmc
    rows = lambda i: (i, 0)
    fixed = lambda i: (0, 0)
    return pl.pallas_call(
        functools.partial(_combine_kernel, tmc=tmc, tr=tr),
        grid=(n,),
        in_specs=[
            pl.BlockSpec((1, 1, 2 * tmc), lambda i: (i, 0, 0), memory_space=pltpu.SMEM),
            pl.BlockSpec((1, 1, 2 * tmc), lambda i: (jnp.minimum(i + 1, n - 1), 0, 0),
                         memory_space=pltpu.SMEM),
            pl.BlockSpec((tmc * tr, LANES), rows),
            pl.BlockSpec((tmc, LANES), rows),
            pl.BlockSpec((1, d), fixed),
            pl.BlockSpec((1, d), fixed),
            pl.BlockSpec(memory_space=pl.ANY),
        ],
        out_specs=pl.BlockSpec((tmc, d), rows),
        out_shape=jax.ShapeDtypeStruct((n_tok, d), F32),
        scratch_shapes=[pltpu.VMEM((2 * 2 * tmc * tr, LANES), F32),
                        pltpu.SemaphoreType.DMA((2,))],
        compiler_params=_params(("arbitrary",)),
        name="moe_combine",
    )(pos, pos, y_tok, gates, ln_g.reshape(1, d), ln_b.reshape(1, d), ys)


def _tile_positions(route, off, t, tr):
    e = route[:, 0:2, :]
    hit = e[..., None] == jnp.arange(N_EXPERTS, dtype=jnp.int32)
    pos = (jnp.sum(jnp.where(hit, off, 0), axis=-1) + route[:, 2:4, :]) * tr
    pos = pos.transpose(1, 0, 2).reshape(2, -1, t)
    return pos.transpose(1, 0, 2).reshape(-1, 1, 2 * t)


def _moe_sparse(y_tok, route, gates, counts, w_gate, w_up, w_down, ln_g, ln_b, *, tmd, tmc):
    d = w_gate.shape[1]
    tr = d // LANES
    n_tok = y_tok.shape[0] // tr
    cnt = counts[:, 0].astype(jnp.int32)
    padded = (cnt + MOE_ROW_TILE - 1) // MOE_ROW_TILE * MOE_ROW_TILE
    off_end = jnp.cumsum(padded)
    off = off_end - padded
    n_sorted = 2 * n_tok + N_EXPERTS * MOE_ROW_TILE
    n_tiles = n_sorted // MOE_ROW_TILE
    n_used = (off_end[-1:] // MOE_ROW_TILE).astype(jnp.int32)
    tile_start = jnp.arange(n_tiles, dtype=jnp.int32) * MOE_ROW_TILE
    tile_expert = jnp.minimum(jnp.sum(tile_start[:, None] >= off_end[None, :], axis=1),
                              N_EXPERTS - 1).astype(jnp.int32)
    xs = _dispatch(y_tok, _tile_positions(route, off, tmd, tr), off_end.astype(jnp.int32),
                   n_sorted=n_sorted, tmd=tmd, tr=tr)
    ys = _experts(xs, tile_expert, n_used, w_gate, w_up, w_down)
    return _combine(ys, _tile_positions(route, off, tmc, tr), y_tok, gates, ln_g, ln_b,
                    tmc=tmc, tr=tr)


POOL_HALO = 32


def _inproj1_kernel(x_ref, w_ref, pw_ref, ps_ref, qt_ref, k_ref, vt_ref, d_ref, halo_ref,
                    *, sb_w):
    i = pl.program_id(1)
    x = x_ref[0].astype(BF16)
    tm = x.shape[0]
    q = jnp.dot(x, w_ref[:, 0:sb_w], preferred_element_type=F32) * ATTN_SCALE
    qt_ref[0] = q.T.astype(BF16)
    k_ref[0] = jnp.dot(x, w_ref[:, sb_w:2 * sb_w], preferred_element_type=F32).astype(BF16)
    v = jnp.dot(x, w_ref[:, 2 * sb_w:3 * sb_w], preferred_element_type=F32)
    vt_ref[0] = v.T.astype(BF16)
    u = jnp.dot(x, w_ref[:, 3 * sb_w:], preferred_element_type=F32)

    @pl.when(i == 0)
    def _():
        halo_ref[...] = jnp.zeros_like(halo_ref)

    tpos = i * tm + lax.broadcasted_iota(jnp.int32, (tm, 1), 0)
    outs = []
    for g, win in enumerate(POOL_WINDOWS):
        cs = slice(g * POOL_CH, (g + 1) * POOL_CH)
        cur = u[:, cs]
        run = jnp.concatenate([halo_ref[:, cs], cur], axis=0)
        span = 1
        while span < win:
            n = run.shape[0] - span
            run = run[span:span + n] + run[0:n]
            span *= 2
        off = POOL_HALO - (win - 1)
        wsum = run[off:off + tm]
        count = jnp.minimum(tpos + 1, win).astype(F32)
        pooled = wsum / count - cur
        mixed = jnp.dot(pooled.astype(BF16), pw_ref[g], preferred_element_type=F32)
        outs.append(mixed * ps_ref[:, cs])
    halo_ref[...] = u[tm - POOL_HALO:tm, :]
    d_ref[0] = jnp.concatenate(outs, axis=1).astype(d_ref.dtype)


def _inproj1(x, w_bf, pool_w_bf, pool_scale, *, sb_w, tm):
    bsz, n_pos, d = x.shape
    n_in = w_bf.shape[1]
    pool_w = n_in - 3 * sb_w
    n_g = len(POOL_WINDOWS)
    return pl.pallas_call(
        functools.partial(_inproj1_kernel, sb_w=sb_w),
        grid=(bsz, n_pos // tm),
        in_specs=[
            pl.BlockSpec((1, tm, d), lambda b, i: (b, i, 0)),
            pl.BlockSpec((d, n_in), lambda b, i: (0, 0)),
            pl.BlockSpec((n_g, POOL_CH, POOL_CH), lambda b, i: (0, 0, 0)),
            pl.BlockSpec((1, pool_w), lambda b, i: (0, 0)),
        ],
        out_specs=[
            pl.BlockSpec((1, sb_w, tm), lambda b, i: (b, 0, i)),
            pl.BlockSpec((1, tm, sb_w), lambda b, i: (b, i, 0)),
            pl.BlockSpec((1, sb_w, tm), lambda b, i: (b, 0, i)),
            pl.BlockSpec((1, tm, pool_w), lambda b, i: (b, i, 0)),
        ],
        out_shape=[
            jax.ShapeDtypeStruct((bsz, sb_w, n_pos), BF16),
            jax.ShapeDtypeStruct((bsz, n_pos, sb_w), BF16),
            jax.ShapeDtypeStruct((bsz, sb_w, n_pos), BF16),
            jax.ShapeDtypeStruct((bsz, n_pos, pool_w), BF16),
        ],
        scratch_shapes=[pltpu.VMEM((POOL_HALO, pool_w), F32)],
        compiler_params=_params(("parallel", "arbitrary")),
        name="inproj1",
    )(x, w_bf, pool_w_bf, pool_scale.astype(F32).reshape(1, pool_w))


def _sb_attn_kernel(qt_ref, k_ref, vt_ref, o_ref, za_ref, zb_ref, *, tq, tk):
    i = pl.program_id(2)
    qt = qt_ref[0]
    zeros = jnp.zeros((HEAD_DIM, tq), BF16)
    qpads = (jnp.concatenate([qt[:HEAD_DIM], zeros], axis=0),
             jnp.concatenate([zeros, qt[HEAD_DIM:]], axis=0))
    cb = min(SB_SCAN_ROWS, tk)
    upper = (lax.broadcasted_iota(jnp.int32, (cb, cb), 1)
             > lax.broadcasted_iota(jnp.int32, (cb, cb), 0)).astype(BF16)

    heads = range(2)

    def scores(j, buf):
        kj = k_ref[0, pl.ds(pl.multiple_of(j * tk, tk), tk), :]
        for hh in heads:
            buf[hh] = jnp.dot(kj, qpads[hh], preferred_element_type=F32)

    def consume(j, buf, carry, masked, lane0=0):
        k0 = pl.multiple_of(j * tk, tk)
        nq = tq - lane0
        whole = carry
        carry = [(acc[:, lane0:], later[:, lane0:]) for acc, later in whole]
        if masked:
            key = k0 + lax.broadcasted_iota(jnp.int32, (tk, nq), 0)
            qpos = i * tq + lane0 + lax.broadcasted_iota(jnp.int32, (tk, nq), 1)
            keep = key < qpos
        zs = [buf[hh, :, lane0:tq] for hh in heads]
        sps = [jnp.maximum(z, 0.0) + jnp.log(1.0 + jnp.exp2(jnp.abs(z) * (-LOG2E))) for z in zs]
        if masked:
            sps = [jnp.where(keep, sp, 0.0) for sp in sps]
        his = [sp.astype(BF16) for sp in sps]
        laters = [carry[hh][1] for hh in heads]
        pieces = [[None] * (tk // cb) for _ in heads]
        for c in reversed(range(tk // cb)):
            rs = slice(c * cb, (c + 1) * cb)
            sufs = [jnp.dot(upper, his[hh][rs], preferred_element_type=F32)
                    for hh in heads]
            for hh in heads:
                spc = sps[hh][rs]
                pieces[hh][c] = jnp.exp2((zs[hh][rs] - (spc + sufs[hh] + laters[hh])) * LOG2E)
                laters[hh] = laters[hh] + sufs[hh][0:1] + spc[0:1]
        out = []
        for hh in heads:
            a = jnp.concatenate(pieces[hh], axis=0)
            if masked:
                a = jnp.where(keep, a, 0.0)
            vj = vt_ref[0, hh * HEAD_DIM:(hh + 1) * HEAD_DIM, pl.ds(k0, tk)]
            acc = carry[hh][0] + jnp.dot(vj, a.astype(BF16), preferred_element_type=F32)
            later = laters[hh]
            if lane0:
                acc = jnp.concatenate([whole[hh][0][:, :lane0], acc], axis=1)
                later = jnp.concatenate([whole[hh][1][:, :lane0], later], axis=1)
            out.append((acc, later))
        return tuple(out)

    init = (jnp.zeros((HEAD_DIM, tq), F32), jnp.zeros((1, tq), F32))
    carry = (init, init)
    n_diag = tq // tk
    last = (i + 1) * n_diag - 1
    n_old = i * n_diag
    bufs = (za_ref, zb_ref)
    scores(last, bufs[0])
    for d in range(n_diag):
        nxt = last - d - 1 if d + 1 < n_diag else jnp.maximum(n_old - 1, 0)
        scores(nxt, bufs[(d + 1) % 2])
        carry = consume(last - d, bufs[d % 2], carry, True, lane0=tq - (d + 1) * tk)
    first_old = bufs[n_diag % 2]
    spare = bufs[(n_diag + 1) % 2]

    def reach(c):
        return jnp.min(jnp.minimum(c[0][1], c[1][1]))

    def live(state):
        t, r, _ = state
        return jnp.logical_and(t < n_old, r < SB_UNDERFLOW)

    def older(state):
        t, _, c = state
        scores(n_old - 1 - t, spare)
        c = consume(n_old - 1 - t, spare, c, False)
        return t + 1, reach(c), c

    r0 = reach(carry)
    carry = lax.cond(jnp.logical_and(n_old > 0, r0 < SB_UNDERFLOW),
                     lambda c: consume(n_old - 1, first_old, c, False), lambda c: c, carry)
    _, _, carry = lax.while_loop(live, older, (jnp.int32(1), reach(carry), carry))
    o_ref[0] = jnp.concatenate([carry[0][0], carry[1][0]], axis=0).T.astype(o_ref.dtype)


def _sb_attention(qt, k, vt, *, tq, tk):
    bsz, sb_w, n_pos = qt.shape
    return pl.pallas_call(
        functools.partial(_sb_attn_kernel, tq=tq, tk=tk),
        grid=(bsz, sb_w // LANES, n_pos // tq),
        in_specs=[
            pl.BlockSpec((1, LANES, tq), lambda b, h, i: (b, h, i)),
            pl.BlockSpec((1, n_pos, LANES), lambda b, h, i: (b, 0, h)),
            pl.BlockSpec((1, LANES, n_pos), lambda b, h, i: (b, h, 0)),
        ],
        out_specs=pl.BlockSpec((1, tq, LANES), lambda b, h, i: (b, i, h)),
        out_shape=jax.ShapeDtypeStruct((bsz, n_pos, sb_w), BF16),
        scratch_shapes=[pltpu.VMEM((2, tk, tq), F32), pltpu.VMEM((2, tk, tq), F32)],
        compiler_params=_params(("parallel", "parallel", "arbitrary")),
        name="sb_attention",
    )(qt, k, vt)


def _rope_tables(n_pos):
    inv = ROPE_THETA ** (-jnp.arange(0, HEAD_DIM, 2, dtype=F32) / HEAD_DIM)
    ang = jnp.arange(n_pos, dtype=F32)[:, None] * inv[None, :]
    ang = jnp.concatenate([ang, ang], axis=-1)
    return jnp.cos(ang), jnp.sin(ang)


def _lambda_init(layer_idx):
    return 0.8 - 0.6 * math.exp(-0.3 * layer_idx)


def kernel(x, router_w, router_bias, l0_in_proj, l0_conv_w, l0_conv_b, l0_conv_ln_g, l0_conv_ln_b, l0_lambda_q1, l0_lambda_k1, l0_lambda_q2, l0_lambda_k2, l0_subln_g, l0_out_proj, l0_ln_mix_g, l0_ln_mix_b, l0_w_gate, l0_w_up, l0_w_down, l0_ln_ffn_g, l0_ln_ffn_b, l1_in_proj, l1_pool_w, l1_pool_scale, l1_out_proj, l1_ln_mix_g, l1_ln_mix_b, l1_w_gate, l1_w_up, l1_w_down, l1_ln_ffn_g, l1_ln_ffn_b):
    bsz, n_pos, d = x.shape
    n_tok = bsz * n_pos
    x = x.astype(F32)

    conv_ch = l0_conv_w.shape[1]
    dv = l0_subln_g.shape[0]
    n_diff = (l0_out_proj.shape[0] - conv_ch) // dv
    qk_w = n_diff * 2 * HEAD_DIM
    v_w = n_diff * dv

    tm = min(PROJ_ROWS, n_pos)
    tq_diff = min(DIFF_TILE, n_pos)
    tq_sb = min(SB_QUERY_TILE, n_pos)
    tk_sb = min(SB_KEY_TILE, n_pos)
    ts = min(CONV_ROWS, n_pos)
    tmc = min(COMBINE_ROWS, n_pos)
    tmd = min(DISPATCH_ROWS, n_tok)
    tm_out = min(OUTPROJ_ROWS, n_tok)
    assert d == SUBLANES * LANES, "token-tile layout needs one (8, 128) tile per token"
    assert n_pos % max(tm, tq_diff, tq_sb, ts, tmc) == 0 and tq_sb % tk_sb == 0
    assert n_tok % max(tmd, tm_out) == 0 and (2 * n_tok) % MOE_ROW_TILE == 0

    cos, sin = _rope_tables(n_pos)
    sign = jnp.where(jnp.arange(HEAD_DIM) < HEAD_DIM // 2, -1.0, 1.0).astype(F32)
    cos128 = jnp.concatenate([cos, cos], axis=-1)
    sins128 = jnp.concatenate([sin * sign, sin * sign], axis=-1)
    cost = cos.T
    sint = sin.T

    glu, qt, k, vt = _inproj0(x, l0_in_proj.astype(BF16), cos128, sins128, cost, sint,
                              conv_ch=conv_ch, qk_w=qk_w, v_w=v_w, tm=tm)
    a = _conv_module(glu, l0_conv_w.astype(F32), l0_conv_b.astype(F32),
                     l0_conv_ln_g.astype(F32), l0_conv_ln_b.astype(F32), ts=ts)
    lam_init = _lambda_init(0)
    lam = (jnp.exp(jnp.sum(l0_lambda_q1.astype(F32) * l0_lambda_k1.astype(F32)))
           - jnp.exp(jnp.sum(l0_lambda_q2.astype(F32) * l0_lambda_k2.astype(F32))) + lam_init)
    o = _diff_attention(qt, k, vt, lam, l0_subln_g, n_heads=n_diff, lam_init=lam_init,
                        tq=tq_diff, tk=tq_diff)
    xf = x.reshape(n_tok, d)
    y, route, gates, counts = _outproj_ln_route(
        a.reshape(n_tok, conv_ch), o.reshape(n_tok, v_w), l0_out_proj, xf,
        l0_ln_mix_g.astype(F32), l0_ln_mix_b.astype(F32), router_w, router_bias, tm=tm_out)
    xf = _moe_sparse(y, route, gates, counts, l0_w_gate, l0_w_up, l0_w_down,
                     l0_ln_ffn_g.astype(F32), l0_ln_ffn_b.astype(F32), tmd=tmd, tmc=tmc)

    sb_w = l1_in_proj.shape[1] - len(POOL_WINDOWS) * POOL_CH
    sb_w = sb_w // 3
    qt, k, vt, dpool = _inproj1(xf.reshape(bsz, n_pos, d), l1_in_proj.astype(BF16),
                                l1_pool_w.astype(BF16), l1_pool_scale, sb_w=sb_w, tm=tm)
    o = _sb_attention(qt, k, vt, tq=tq_sb, tk=tk_sb)
    y, route, gates, counts = _outproj_ln_route(
        o.reshape(n_tok, sb_w), dpool.reshape(n_tok, -1), l1_out_proj, xf,
        l1_ln_mix_g.astype(F32), l1_ln_mix_b.astype(F32), router_w, router_bias, tm=tm_out)
    xf = _moe_sparse(y, route, gates, counts, l1_w_gate, l1_w_up, l1_w_down,
                     l1_ln_ffn_g.astype(F32), l1_ln_ffn_b.astype(F32), tmd=tmd, tmc=tmc)
    return xf.reshape(bsz, n_pos, d)
```

```python
import functools
import math

import jax
import jax.numpy as jnp
from jax import lax
from jax.experimental import pallas as pl
from jax.experimental.pallas import tpu as pltpu

F32 = jnp.float32
BF16 = jnp.bfloat16

DEPTH = 2
HEAD_DIM = 64
ROPE_THETA = 10000.0
LN_EPS = 1e-5
RMS_EPS = 1e-5
DEEPNORM_ALPHA = (2 * DEPTH) ** 0.25
CONV_WIDTH = 31
POOL_WINDOWS = (2, 4, 8, 16)
POOL_CH = 128
N_EXPERTS = 16
N_GROUPS = 4
EXPERTS_PER_GROUP = N_EXPERTS // N_GROUPS
ATTN_SCALE = HEAD_DIM ** -0.5
LOG2E = math.log2(math.e)

LANES = 128
SUBLANES = 8
MIB = 1024 * 1024
VMEM_LIMIT = 52 * MIB

PROJ_ROWS = 512
OUTPROJ_ROWS = 1024
DIFF_TILE = 512
SB_QUERY_TILE = 512
SB_KEY_TILE = 256
CONV_ROWS = 256
DISPATCH_ROWS = 2048
COMBINE_ROWS = 256

NEG_BIG = -1e30
SB_SCAN_ROWS = 256
SB_UNDERFLOW = 110.0


def _params(semantics, vmem=VMEM_LIMIT):
    return pltpu.CompilerParams(dimension_semantics=semantics, vmem_limit_bytes=vmem)


def _layer_norm_rows(h, g, b):
    mu = jnp.mean(h, axis=-1, keepdims=True)
    d = h - mu
    var = jnp.mean(d * d, axis=-1, keepdims=True)
    return d * lax.rsqrt(var + LN_EPS) * g + b


def _load_token_tiles(ref, row0, n, d):
    k = d // LANES
    return jnp.concatenate([ref[pl.ds(row0 + c, n, stride=k), :] for c in range(k)], axis=1)


def _store_token_tiles(ref, row0, val):
    n, d = val.shape
    k = d // LANES
    for c in range(k):
        ref[pl.ds(row0 + c, n, stride=k), :] = val[:, c * LANES:(c + 1) * LANES]


def _inproj0_kernel(x_ref, w_ref, cos_ref, sins_ref, cost_ref, sint_ref,
                    glu_ref, qt_ref, k_ref, vt_ref, *, conv_ch, qk_w):
    x = x_ref[0].astype(BF16)
    c0, c1, c2, c3 = conv_ch, 2 * conv_ch, 2 * conv_ch + qk_w, 2 * conv_ch + 2 * qk_w

    a_val = jnp.dot(x, w_ref[:, 0:c0], preferred_element_type=F32)
    a_gate = jnp.dot(x, w_ref[:, c0:c1], preferred_element_type=F32)
    glu_ref[0] = a_val * jax.nn.sigmoid(a_gate)

    dq = jnp.dot(x, w_ref[:, c1:c2], preferred_element_type=F32)
    dqt = dq.T
    cost = cost_ref[...]
    sint = sint_ref[...]
    half = HEAD_DIM // 2
    pieces = []
    for blk in range(qk_w // HEAD_DIM):
        u = dqt[blk * HEAD_DIM:(blk + 1) * HEAD_DIM]
        rot = jnp.concatenate([-u[half:], u[:half]], axis=0)
        pieces.append((u * cost + rot * sint) * (ATTN_SCALE * LOG2E))
    qt_ref[0] = jnp.concatenate(pieces, axis=0).astype(BF16)

    dk = jnp.dot(x, w_ref[:, c2:c3], preferred_element_type=F32)
    cos = cos_ref[...]
    sins = sins_ref[...]
    lane = lax.broadcasted_iota(jnp.int32, cos.shape, 1)
    first = (lane & (HEAD_DIM - 1)) < half
    kparts = []
    for h in range(qk_w // LANES):
        u = dk[:, h * LANES:(h + 1) * LANES]
        rot = jnp.where(first, pltpu.roll(u, LANES - half, 1), pltpu.roll(u, half, 1))
        kparts.append(u * cos + rot * sins)
    k_ref[0] = jnp.concatenate(kparts, axis=1).astype(BF16)

    dv = jnp.dot(x, w_ref[:, c3:], preferred_element_type=F32)
    vt_ref[0] = dv.T.astype(BF16)


def _inproj0(x, w_bf, cos128, sins128, cost, sint, *, conv_ch, qk_w, v_w, tm):
    bsz, n_pos, d = x.shape
    n_in = w_bf.shape[1]
    kern = functools.partial(_inproj0_kernel, conv_ch=conv_ch, qk_w=qk_w)
    return pl.pallas_call(
        kern,
        grid=(bsz, n_pos // tm),
        in_specs=[
            pl.BlockSpec((1, tm, d), lambda b, i: (b, i, 0)),
            pl.BlockSpec((d, n_in), lambda b, i: (0, 0)),
            pl.BlockSpec((tm, LANES), lambda b, i: (i, 0)),
            pl.BlockSpec((tm, LANES), lambda b, i: (i, 0)),
            pl.BlockSpec((HEAD_DIM, tm), lambda b, i: (0, i)),
            pl.BlockSpec((HEAD_DIM, tm), lambda b, i: (0, i)),
        ],
        out_specs=[
            pl.BlockSpec((1, tm, conv_ch), lambda b, i: (b, i, 0)),
            pl.BlockSpec((1, qk_w, tm), lambda b, i: (b, 0, i)),
            pl.BlockSpec((1, tm, qk_w), lambda b, i: (b, i, 0)),
            pl.BlockSpec((1, v_w, tm), lambda b, i: (b, 0, i)),
        ],
        out_shape=[
            jax.ShapeDtypeStruct((bsz, n_pos, conv_ch), F32),
            jax.ShapeDtypeStruct((bsz, qk_w, n_pos), BF16),
            jax.ShapeDtypeStruct((bsz, n_pos, qk_w), BF16),
            jax.ShapeDtypeStruct((bsz, v_w, n_pos), BF16),
        ],
        compiler_params=_params(("parallel", "arbitrary")),
        name="inproj0",
    )(x, w_bf, cos128, sins128, cost, sint)


CONV_HALO = 32
CONV_ROW_BLOCK = 128


def _conv_kernel(a_ref, cw_ref, cb_ref, g_ref, b_ref, o_ref, win_ref, acc_ref, *, ts):
    i = pl.program_id(1)
    t0 = pl.multiple_of(i * ts, ts)
    h0 = pl.multiple_of(jnp.maximum(t0 - CONV_HALO, 0), CONV_HALO)
    n_ch = a_ref.shape[2]
    n_shift = ts + CONV_HALO - SUBLANES
    for c in range(n_ch // LANES):
        cs = slice(c * LANES, (c + 1) * LANES)
        cur = a_ref[0, pl.ds(t0, ts), cs]
        halo = a_ref[0, pl.ds(h0, CONV_HALO), cs]
        halo = jnp.where(i > 0, halo, 0.0)
        xw = jnp.concatenate([halo, cur], axis=0)
        win_ref[c, 0] = xw
        for r in range(1, SUBLANES):
            win_ref[c, r, 0:n_shift] = xw[r:r + n_shift]
        def row_block(rb, carry, c=c, cs=cs, full=xw.shape[0]):
            base = pl.multiple_of(rb * CONV_ROW_BLOCK, CONV_ROW_BLOCK)
            acc = jnp.zeros((CONV_ROW_BLOCK, LANES), F32) + cb_ref[:, cs]
            for r in range(SUBLANES):
                rows = full if r == 0 else n_shift
                for a in range(CONV_HALO // SUBLANES + 1):
                    w = SUBLANES * a + r - (CONV_HALO - CONV_WIDTH + 1)
                    if 0 <= w < CONV_WIDTH and SUBLANES * a + ts <= rows:
                        tap = win_ref[c, r, pl.ds(base + SUBLANES * a, CONV_ROW_BLOCK)]
                        acc = acc + tap * cw_ref[w:w + 1, cs]
            acc_ref[pl.ds(base, CONV_ROW_BLOCK), cs] = acc
            return carry
        lax.fori_loop(0, ts // CONV_ROW_BLOCK, row_block, 0)
    y = _layer_norm_rows(acc_ref[...], g_ref[...], b_ref[...])
    o_ref[0] = (y * jax.nn.sigmoid(y)).astype(o_ref.dtype)


def _conv_module(a, conv_w, conv_b, ln_g, ln_b, *, ts):
    bsz, n_pos, n_ch = a.shape
    return pl.pallas_call(
        functools.partial(_conv_kernel, ts=ts),
        grid=(bsz, n_pos // ts),
        in_specs=[
            pl.BlockSpec((1, n_pos, n_ch), lambda b, i: (b, 0, 0)),
            pl.BlockSpec((CONV_WIDTH, n_ch), lambda b, i: (0, 0)),
            pl.BlockSpec((1, n_ch), lambda b, i: (0, 0)),
            pl.BlockSpec((1, n_ch), lambda b, i: (0, 0)),
            pl.BlockSpec((1, n_ch), lambda b, i: (0, 0)),
        ],
        out_specs=pl.BlockSpec((1, ts, n_ch), lambda b, i: (b, i, 0)),
        out_shape=jax.ShapeDtypeStruct((bsz, n_pos, n_ch), BF16),
        scratch_shapes=[pltpu.VMEM((n_ch // LANES, SUBLANES, ts + CONV_HALO, LANES), F32),
                        pltpu.VMEM((ts, n_ch), F32)],
        compiler_params=_params(("parallel", "arbitrary")),
        name="conv_module",
    )(a, conv_w, conv_b.reshape(1, n_ch), ln_g.reshape(1, n_ch), ln_b.reshape(1, n_ch))


def _diff_attn_kernel(lam_ref, qt_ref, k_ref, vt_ref, g_ref, o_ref, sa_ref, sb_ref,
                      *, tq, tk, out_scale):
    i = pl.program_id(2)
    lam = lam_ref[0, 0]
    qt = qt_ref[0]
    zeros = jnp.zeros((HEAD_DIM, tq), BF16)
    qa = jnp.concatenate([qt[:HEAD_DIM], zeros], axis=0)
    qb = jnp.concatenate([zeros, qt[HEAD_DIM:]], axis=0)
    dv = vt_ref.shape[1]

    def online(s, vj, m, l, acc):
        m_new = jnp.maximum(m, jnp.max(s, axis=0, keepdims=True))
        alpha = jnp.exp2(m - m_new)
        p = jnp.exp2(s - m_new)
        l_new = alpha * l + jnp.sum(p, axis=0, keepdims=True)
        acc_new = alpha * acc + jnp.dot(vj, p.astype(BF16), preferred_element_type=F32)
        return m_new, l_new, acc_new

    def scores(j, buf):
        kj = k_ref[0, pl.ds(pl.multiple_of(j * tk, tk), tk), :]
        buf[0] = jnp.dot(kj, qa, preferred_element_type=F32)
        buf[1] = jnp.dot(kj, qb, preferred_element_type=F32)

    def consume(j, buf, carry, masked):
        if masked:
            return consume_diagonal(j, buf, carry)
        m1, l1, acc1, m2, l2, acc2 = carry
        vj = vt_ref[0, :, pl.ds(pl.multiple_of(j * tk, tk), tk)]
        m1, l1, acc1 = online(buf[0], vj, m1, l1, acc1)
        m2, l2, acc2 = online(buf[1], vj, m2, l2, acc2)
        return m1, l1, acc1, m2, l2, acc2

    def consume_diagonal(j, buf, carry):
        hk = tk // 2
        for key_lo, lane0 in ((0, 0), (hk, tq - hk)):
            nq = tq - lane0
            k0 = pl.multiple_of(j * tk + key_lo, hk)
            vj = vt_ref[0, :, pl.ds(k0, hk)]
            key = k0 + lax.broadcasted_iota(jnp.int32, (hk, nq), 0)
            qpos = i * tq + lane0 + lax.broadcasted_iota(jnp.int32, (hk, nq), 1)
            keep = key <= qpos
            new = []
            for mp in range(2):
                m, l, acc = carry[3 * mp:3 * mp + 3]
                s = jnp.where(keep, buf[mp, key_lo:key_lo + hk, lane0:tq], NEG_BIG)
                part = online(s, vj, m[:, lane0:], l[:, lane0:], acc[:, lane0:])
                if lane0:
                    part = [jnp.concatenate([old[:, :lane0], upd], axis=1)
                            for old, upd in zip((m, l, acc), part)]
                new.extend(part)
            carry = tuple(new)
        return carry

    row = jnp.full((1, tq), NEG_BIG, F32)
    zrow = jnp.zeros((1, tq), F32)
    zacc = jnp.zeros((dv, tq), F32)
    carry = (row, zrow, zacc, row, zrow, zacc)

    scores(0, sa_ref)

    def pair(u, c):
        scores(2 * u + 1, sb_ref)
        c = consume(2 * u, sa_ref, c, False)
        scores(2 * u + 2, sa_ref)
        return consume(2 * u + 1, sb_ref, c, False)

    carry = lax.fori_loop(0, lax.shift_right_logical(i, 1), pair, carry)

    def odd_tail(c):
        scores(i, sb_ref)
        c = consume(i - 1, sa_ref, c, False)
        return consume(i, sb_ref, c, True)

    def even_tail(c):
        return consume(i, sa_ref, c, True)

    m1, l1, acc1, m2, l2, acc2 = lax.cond((i & 1) == 1, odd_tail, even_tail, carry)

    o = acc1 / l1 - lam * (acc2 / l2)
    o = o * lax.rsqrt(jnp.mean(o * o, axis=0, keepdims=True) + RMS_EPS)
    o = o * g_ref[...] * out_scale
    o_ref[0] = o.T.astype(o_ref.dtype)


def _diff_attention(qt, k, vt, lam, subln_g, *, n_heads, lam_init, tq, tk):
    bsz, _, n_pos = qt.shape
    dv = vt.shape[1] // n_heads
    kern = functools.partial(_diff_attn_kernel, tq=tq, tk=tk, out_scale=1.0 - lam_init)
    return pl.pallas_call(
        kern,
        grid=(bsz, n_heads, n_pos // tq),
        in_specs=[
            pl.BlockSpec(memory_space=pltpu.SMEM),
            pl.BlockSpec((1, 2 * HEAD_DIM, tq), lambda b, h, i: (b, h, i)),
            pl.BlockSpec((1, n_pos, 2 * HEAD_DIM), lambda b, h, i: (b, 0, h)),
            pl.BlockSpec((1, dv, n_pos), lambda b, h, i: (b, h, 0)),
            pl.BlockSpec((dv, 1), lambda b, h, i: (0, 0)),
        ],
        out_specs=pl.BlockSpec((1, tq, dv), lambda b, h, i: (b, i, h)),
        out_shape=jax.ShapeDtypeStruct((bsz, n_pos, n_heads * dv), BF16),
        scratch_shapes=[pltpu.VMEM((2, tk, tq), F32), pltpu.VMEM((2, tk, tq), F32)],
        compiler_params=_params(("parallel", "parallel", "arbitrary")),
        name="diff_attention",
    )(lam.reshape(1, 1), qt, k, vt, subln_g.astype(F32).reshape(dv, 1))


def _masked_top2(vals, member, row):
    big = N_EXPERTS
    v = jnp.where(member, vals, -jnp.inf)
    v1 = jnp.max(v, axis=0, keepdims=True)
    i1 = jnp.min(jnp.where(v == v1, row, big), axis=0, keepdims=True)
    v = jnp.where(row == i1, -jnp.inf, v)
    v2 = jnp.max(v, axis=0, keepdims=True)
    i2 = jnp.min(jnp.where(v == v2, row, big), axis=0, keepdims=True)
    return v1, i1, v2, i2


def _top2_route(aff, sel):
    row = lax.broadcasted_iota(jnp.int32, sel.shape, 0)
    group = lax.shift_right_logical(row, EXPERTS_PER_GROUP.bit_length() - 1)
    gbest = None
    for g in range(N_GROUPS):
        v1, _, v2, _ = _masked_top2(sel, group == g, row)
        score = v1 + v2
        if gbest is None:
            gbest, gval = jnp.zeros_like(score, dtype=jnp.int32), score
        else:
            better = score > gval
            gbest = jnp.where(better, g, gbest)
            gval = jnp.where(better, score, gval)
    _, e1, _, e2 = _masked_top2(sel, group == gbest, row)
    a1 = jnp.sum(jnp.where(row == e1, aff, 0.0), axis=0, keepdims=True)
    a2 = jnp.sum(jnp.where(row == e2, aff, 0.0), axis=0, keepdims=True)
    denom = a1 + a2
    return e1, e2, a1 / denom, a2 / denom


ROUTE_ROWS = SUBLANES


def _outproj_kernel(pa_ref, pb_ref, wa_ref, wb_ref, x_ref, g_ref, b_ref, rw_ref, rb_ref,
                    y_ref, route_ref, gate_ref, count_ref, seen_ref):
    i = pl.program_id(0)

    @pl.when(i == 0)
    def _():
        seen_ref[...] = jnp.zeros_like(seen_ref)

    mix = jnp.dot(pa_ref[...], wa_ref[...], preferred_element_type=F32)
    mix = mix + jnp.dot(pb_ref[...], wb_ref[...], preferred_element_type=F32)
    y = _layer_norm_rows(DEEPNORM_ALPHA * x_ref[...] + mix, g_ref[...], b_ref[...])
    _store_token_tiles(y_ref, 0, y)
    tm = y.shape[0]
    yh = y.astype(BF16)
    yl = (y - yh.astype(F32)).astype(BF16)
    nt = (((1,), (1,)), ((), ()))
    both = lax.dot_general(rw_ref[...], yh, nt, preferred_element_type=F32)
    logits = (both[:N_EXPERTS] + both[N_EXPERTS:]
              + lax.dot_general(rw_ref[0:N_EXPERTS], yl, nt, preferred_element_type=F32))
    aff = jax.nn.sigmoid(logits)
    e1, e2, g1, g2 = _top2_route(aff, aff + rb_ref[...])

    erow = lax.broadcasted_iota(jnp.int32, (N_EXPERTS, tm), 0)
    hit = jnp.where((erow == e1) | (erow == e2), 1.0, 0.0)
    upto = (lax.broadcasted_iota(jnp.int32, (tm, tm), 0)
            <= lax.broadcasted_iota(jnp.int32, (tm, tm), 1)).astype(BF16)
    incl = jnp.dot(hit.astype(BF16), upto, preferred_element_type=F32)
    before = seen_ref[:, 0:1] + incl - hit
    r1 = jnp.sum(jnp.where(erow == e1, before, 0.0), axis=0, keepdims=True).astype(jnp.int32)
    r2 = jnp.sum(jnp.where(erow == e2, before, 0.0), axis=0, keepdims=True).astype(jnp.int32)
    seen_ref[...] = seen_ref[...] + jnp.sum(hit, axis=1, keepdims=True)
    count_ref[...] = seen_ref[...]

    rrow = lax.broadcasted_iota(jnp.int32, (ROUTE_ROWS, tm), 0)
    route_ref[0] = jnp.where(rrow == 0, e1, jnp.where(rrow == 1, e2,
                             jnp.where(rrow == 2, r1, jnp.where(rrow == 3, r2, 0))))
    grow = lax.broadcasted_iota(jnp.int32, (LANES, tm), 0)
    gate_ref[...] = jnp.where(grow == 0, g1, jnp.where(grow == 1, g2, 0.0)).T


def _outproj_ln_route(pa, pb, w_out, x, ln_g, ln_b, router_w, router_bias, *, tm):
    n_tok, d = x.shape
    ka, kb = pa.shape[1], pb.shape[1]
    wa = w_out[:ka].astype(BF16)
    wb = w_out[ka:].astype(BF16)
    rw = router_w.astype(F32).T
    rw_hi = rw.astype(BF16)
    rw_lo = (rw - rw_hi.astype(F32)).astype(BF16)
    rows = lambda i: (i, 0)
    fixed = lambda i: (0, 0)
    return pl.pallas_call(
        _outproj_kernel,
        grid=(n_tok // tm,),
        in_specs=[
            pl.BlockSpec((tm, ka), rows),
            pl.BlockSpec((tm, kb), rows),
            pl.BlockSpec((ka, d), fixed),
            pl.BlockSpec((kb, d), fixed),
            pl.BlockSpec((tm, d), rows),
            pl.BlockSpec((1, d), fixed),
            pl.BlockSpec((1, d), fixed),
            pl.BlockSpec((2 * N_EXPERTS, d), fixed),
            pl.BlockSpec((N_EXPERTS, 1), fixed),
        ],
        out_specs=[
            pl.BlockSpec((tm * d // LANES, LANES), rows),
            pl.BlockSpec((1, ROUTE_ROWS, tm), lambda i: (i, 0, 0)),
            pl.BlockSpec((tm, LANES), rows),
            pl.BlockSpec((N_EXPERTS, LANES), fixed),
        ],
        out_shape=[
            jax.ShapeDtypeStruct((n_tok * d // LANES, LANES), F32),
            jax.ShapeDtypeStruct((n_tok // tm, ROUTE_ROWS, tm), jnp.int32),
            jax.ShapeDtypeStruct((n_tok, LANES), F32),
            jax.ShapeDtypeStruct((N_EXPERTS, LANES), F32),
        ],
        scratch_shapes=[pltpu.VMEM((N_EXPERTS, LANES), F32)],
        compiler_params=_params(("arbitrary",)),
        name="outproj_ln_route",
    )(pa, pb, wa, wb, x, ln_g.reshape(1, d), ln_b.reshape(1, d),
      jnp.concatenate([rw_hi, rw_lo], axis=0), router_bias.astype(F32).reshape(N_EXPERTS, 1))


MOE_ROW_TILE = 512
DMA_UNROLL = 8


def _row_copy_wait(src_hbm, dst, sem, n_rows):
    pltpu.make_async_copy(src_hbm.at[pl.ds(0, n_rows)], dst, sem).wait()


def _dispatch_kernel(offend_ref, pos_ref, y_ref, xs_hbm, zero_ref, sem, zsem, *, tmd, tr):
    i = pl.program_id(0)
    tile_rows = MOE_ROW_TILE * tr

    @pl.when(i == 0)
    def _():
        zero_ref[...] = jnp.zeros_like(zero_ref)
        def tail_copy(e):
            start = pl.multiple_of((offend_ref[e] - MOE_ROW_TILE) * tr, tile_rows)
            return pltpu.make_async_copy(zero_ref, xs_hbm.at[pl.ds(start, tile_rows)], zsem)
        def nonempty(e):
            return offend_ref[e] > (offend_ref[e - 1] if e > 0 else 0)
        for e in range(N_EXPERTS):
            @pl.when(nonempty(e))
            def _():
                tail_copy(e).start()
        for e in range(N_EXPERTS):
            @pl.when(nonempty(e))
            def _():
                tail_copy(e).wait()
        def unused_copy(j):
            return pltpu.make_async_copy(
                zero_ref, xs_hbm.at[pl.ds(pl.multiple_of(j * tile_rows, tile_rows), tile_rows)],
                zsem)
        first_unused = lax.shift_right_logical(offend_ref[N_EXPERTS - 1],
                                               MOE_ROW_TILE.bit_length() - 1)
        n_tiles = xs_hbm.shape[0] // tile_rows
        lax.fori_loop(first_unused, n_tiles, lambda j, c: (unused_copy(j).start(), c)[1], 0)
        lax.fori_loop(first_unused, n_tiles, lambda j, c: (unused_copy(j).wait(), c)[1], 0)

    def issue(g, c):
        r0 = g * DMA_UNROLL
        ps = [[pos_ref[0, 0, k * tmd + r0 + j] for k in range(2)] for j in range(DMA_UNROLL)]
        for j in range(DMA_UNROLL):
            src = y_ref.at[pl.ds(pl.multiple_of((r0 + j) * tr, tr), tr)]
            for k in range(2):
                dst = xs_hbm.at[pl.ds(pl.multiple_of(ps[j][k], tr), tr)]
                pltpu.make_async_copy(src, dst, sem).start(priority=k)
        return c

    lax.fori_loop(0, tmd // DMA_UNROLL, issue, 0)
    for k in range(2):
        pltpu.make_async_copy(y_ref, xs_hbm.at[pl.ds(0, tmd * tr)], sem).wait()


def _dispatch(y_tok, pos, off_end, *, n_sorted, tmd, tr):
    n_tok = y_tok.shape[0] // tr
    return pl.pallas_call(
        functools.partial(_dispatch_kernel, tmd=tmd, tr=tr),
        grid_spec=pltpu.PrefetchScalarGridSpec(
            num_scalar_prefetch=1,
            grid=(n_tok // tmd,),
            in_specs=[
                pl.BlockSpec((1, 1, 2 * tmd), lambda i, off: (i, 0, 0), memory_space=pltpu.SMEM),
                pl.BlockSpec((tmd * tr, LANES), lambda i, off: (i, 0)),
            ],
            out_specs=pl.BlockSpec(memory_space=pl.ANY),
            scratch_shapes=[pltpu.VMEM((MOE_ROW_TILE * tr, LANES), F32),
                            pltpu.SemaphoreType.DMA, pltpu.SemaphoreType.DMA],
        ),
        out_shape=jax.ShapeDtypeStruct((n_sorted * tr, LANES), F32),
        compiler_params=_params(("arbitrary",)),
        name="moe_dispatch",
    )(off_end, pos, y_tok)


def _experts_kernel(texp_ref, nused_ref, xs_ref, wg_ref, wu_ref, wd_ref, ys_ref,
                    wg_bf, wu_bf, wd_bf):
    i = pl.program_id(0)
    used = i < nused_ref[0]

    @pl.when(jnp.logical_not(used))
    def _():
        ys_ref[...] = jnp.zeros_like(ys_ref)

    new_expert = jnp.logical_or(i == 0, texp_ref[i] != texp_ref[jnp.maximum(i - 1, 0)])

    @pl.when(jnp.logical_and(used, new_expert))
    def _():
        wg_bf[...] = wg_ref[0].astype(BF16)
        wu_bf[...] = wu_ref[0].astype(BF16)
        wd_bf[...] = wd_ref[0].astype(BF16)

    @pl.when(used)
    def _():
        d = wg_ref.shape[1]
        x = _load_token_tiles(xs_ref, 0, MOE_ROW_TILE, d).astype(BF16)
        hg = jnp.dot(x, wg_bf[...], preferred_element_type=F32)
        hu = jnp.dot(x, wu_bf[...], preferred_element_type=F32)
        h = (hg * jax.nn.sigmoid(hg)) * hu
        _store_token_tiles(ys_ref, 0, jnp.dot(h.astype(BF16), wd_bf[...],
                                              preferred_element_type=F32))


def _experts(xs, tile_expert, n_used, w_gate, w_up, w_down):
    _, d, dff = w_gate.shape
    tile_rows = MOE_ROW_TILE * d // LANES
    tile = lambda i, te, nu: (jnp.minimum(i, nu[0] - 1), 0)
    every = lambda i, te, nu: (i, 0)
    wsel = lambda i, te, nu: (te[jnp.minimum(i, nu[0] - 1)], 0, 0)
    return pl.pallas_call(
        _experts_kernel,
        grid_spec=pltpu.PrefetchScalarGridSpec(
            num_scalar_prefetch=2,
            grid=(xs.shape[0] // tile_rows,),
            in_specs=[
                pl.BlockSpec((tile_rows, LANES), tile),
                pl.BlockSpec((1, d, dff), wsel),
                pl.BlockSpec((1, d, dff), wsel),
                pl.BlockSpec((1, dff, d), wsel),
            ],
            out_specs=pl.BlockSpec((tile_rows, LANES), every),
            scratch_shapes=[pltpu.VMEM((d, dff), BF16), pltpu.VMEM((d, dff), BF16),
                            pltpu.VMEM((dff, d), BF16)],
        ),
        out_shape=jax.ShapeDtypeStruct(xs.shape, F32),
        compiler_params=_params(("arbitrary",)),
        name="moe_experts",
    )(tile_expert, n_used, xs, w_gate, w_up, w_down)


def _combine_kernel(pos_ref, posn_ref, y_ref, gate_ref, g_ref, b_ref, ys_hbm, o_ref,
                    buf, sem, *, tmc, tr):
    i = pl.program_id(0)
    n = pl.num_programs(0)
    d = o_ref.shape[1]
    half = tmc * tr
    slot_rows = 2 * half

    def gather(p_ref, slot):
        base = slot * slot_rows
        def issue(g, c):
            r0 = g * DMA_UNROLL
            ps = [[p_ref[0, 0, k * tmc + r0 + j] for k in range(2)] for j in range(DMA_UNROLL)]
            for j in range(DMA_UNROLL):
                for k in range(2):
                    src = ys_hbm.at[pl.ds(pl.multiple_of(ps[j][k], tr), tr)]
                    dst = buf.at[pl.ds(pl.multiple_of(base + k * half + (r0 + j) * tr, tr), tr)]
                    pltpu.make_async_copy(src, dst, sem.at[slot]).start(priority=k)
            return c
        lax.fori_loop(0, tmc // DMA_UNROLL, issue, 0)

    @pl.when(i == 0)
    def _():
        gather(pos_ref, 0)

    @pl.when(i + 1 < n)
    def _():
        gather(posn_ref, (i + 1) % 2)

    slot = i % 2
    base = pl.multiple_of(slot * slot_rows, slot_rows)
    _row_copy_wait(ys_hbm, buf.at[pl.ds(base, slot_rows)], sem.at[slot], slot_rows)
    gate = gate_ref[...]
    ffn = (gate[:, 0:1] * _load_token_tiles(buf, base, tmc, d)
           + gate[:, 1:2] * _load_token_tiles(buf, base + half, tmc, d))
    y = _load_token_tiles(y_ref, 0, tmc, d)
    o_ref[...] = _layer_norm_rows(DEEPNORM_ALPHA * y + ffn, g_ref[...], b_ref[...])


def _combine(ys, pos, y_tok, gates, ln_g, ln_b, *, tmc, tr):
    d = tr * LANES
    n_tok = y_tok.shape[0] // tr
    n = n_tok // tmc
    rows = lambda i: (i, 0)
    fixed = lambda i: (0, 0)
    return pl.pallas_call(
        functools.partial(_combine_kernel, tmc=tmc, tr=tr),
        grid=(n,),
        in_specs=[
            pl.BlockSpec((1, 1, 2 * tmc), lambda i: (i, 0, 0), memory_space=pltpu.SMEM),
            pl.BlockSpec((1, 1, 2 * tmc), lambda i: (jnp.minimum(i + 1, n - 1), 0, 0),
                         memory_space=pltpu.SMEM),
            pl.BlockSpec((tmc * tr, LANES), rows),
            pl.BlockSpec((tmc, LANES), rows),
            pl.BlockSpec((1, d), fixed),
            pl.BlockSpec((1, d), fixed),
            pl.BlockSpec(memory_space=pl.ANY),
        ],
        out_specs=pl.BlockSpec((tmc, d), rows),
        out_shape=jax.ShapeDtypeStruct((n_tok, d), F32),
        scratch_shapes=[pltpu.VMEM((2 * 2 * tmc * tr, LANES), F32),
                        pltpu.SemaphoreType.DMA((2,))],
        compiler_params=_params(("arbitrary",)),
        name="moe_combine",
    )(pos, pos, y_tok, gates, ln_g.reshape(1, d), ln_b.reshape(1, d), ys)


def _tile_positions(route, off, t, tr):
    e = route[:, 0:2, :]
    hit = e[..., None] == jnp.arange(N_EXPERTS, dtype=jnp.int32)
    pos = (jnp.sum(jnp.where(hit, off, 0), axis=-1) + route[:, 2:4, :]) * tr
    pos = pos.transpose(1, 0, 2).reshape(2, -1, t)
    return pos.transpose(1, 0, 2).reshape(-1, 1, 2 * t)


def _moe_sparse(y_tok, route, gates, counts, w_gate, w_up, w_down, ln_g, ln_b, *, tmd, tmc):
    d = w_gate.shape[1]
    tr = d // LANES
    n_tok = y_tok.shape[0] // tr
    cnt = counts[:, 0].astype(jnp.int32)
    padded = (cnt + MOE_ROW_TILE - 1) // MOE_ROW_TILE * MOE_ROW_TILE
    off_end = jnp.cumsum(padded)
    off = off_end - padded
    n_sorted = 2 * n_tok + N_EXPERTS * MOE_ROW_TILE
    n_tiles = n_sorted // MOE_ROW_TILE
    n_used = (off_end[-1:] // MOE_ROW_TILE).astype(jnp.int32)
    tile_start = jnp.arange(n_tiles, dtype=jnp.int32) * MOE_ROW_TILE
    tile_expert = jnp.minimum(jnp.sum(tile_start[:, None] >= off_end[None, :], axis=1),
                              N_EXPERTS - 1).astype(jnp.int32)
    xs = _dispatch(y_tok, _tile_positions(route, off, tmd, tr), off_end.astype(jnp.int32),
                   n_sorted=n_sorted, tmd=tmd, tr=tr)
    ys = _experts(xs, tile_expert, n_used, w_gate, w_up, w_down)
    return _combine(ys, _tile_positions(route, off, tmc, tr), y_tok, gates, ln_g, ln_b,
                    tmc=tmc, tr=tr)


POOL_HALO = 32


def _inproj1_kernel(x_ref, w_ref, pw_ref, ps_ref, qt_ref, k_ref, vt_ref, d_ref, halo_ref,
                    *, sb_w):
    i = pl.program_id(1)
    x = x_ref[0].astype(BF16)
    tm = x.shape[0]
    q = jnp.dot(x, w_ref[:, 0:sb_w], preferred_element_type=F32) * ATTN_SCALE
    qt_ref[0] = q.T.astype(BF16)
    k_ref[0] = jnp.dot(x, w_ref[:, sb_w:2 * sb_w], preferred_element_type=F32).astype(BF16)
    v = jnp.dot(x, w_ref[:, 2 * sb_w:3 * sb_w], preferred_element_type=F32)
    vt_ref[0] = v.T.astype(BF16)
    u = jnp.dot(x, w_ref[:, 3 * sb_w:], preferred_element_type=F32)

    @pl.when(i == 0)
    def _():
        halo_ref[...] = jnp.zeros_like(halo_ref)

    tpos = i * tm + lax.broadcasted_iota(jnp.int32, (tm, 1), 0)
    outs = []
    for g, win in enumerate(POOL_WINDOWS):
        cs = slice(g * POOL_CH, (g + 1) * POOL_CH)
        cur = u[:, cs]
        run = jnp.concatenate([halo_ref[:, cs], cur], axis=0)
        span = 1
        while span < win:
            n = run.shape[0] - span
            run = run[span:span + n] + run[0:n]
            span *= 2
        off = POOL_HALO - (win - 1)
        wsum = run[off:off + tm]
        count = jnp.minimum(tpos + 1, win).astype(F32)
        pooled = wsum / count - cur
        mixed = jnp.dot(pooled.astype(BF16), pw_ref[g], preferred_element_type=F32)
        outs.append(mixed * ps_ref[:, cs])
    halo_ref[...] = u[tm - POOL_HALO:tm, :]
    d_ref[0] = jnp.concatenate(outs, axis=1).astype(d_ref.dtype)


def _inproj1(x, w_bf, pool_w_bf, pool_scale, *, sb_w, tm):
    bsz, n_pos, d = x.shape
    n_in = w_bf.shape[1]
    pool_w = n_in - 3 * sb_w
    n_g = len(POOL_WINDOWS)
    return pl.pallas_call(
        functools.partial(_inproj1_kernel, sb_w=sb_w),
        grid=(bsz, n_pos // tm),
        in_specs=[
            pl.BlockSpec((1, tm, d), lambda b, i: (b, i, 0)),
            pl.BlockSpec((d, n_in), lambda b, i: (0, 0)),
            pl.BlockSpec((n_g, POOL_CH, POOL_CH), lambda b, i: (0, 0, 0)),
            pl.BlockSpec((1, pool_w), lambda b, i: (0, 0)),
        ],
        out_specs=[
            pl.BlockSpec((1, sb_w, tm), lambda b, i: (b, 0, i)),
            pl.BlockSpec((1, tm, sb_w), lambda b, i: (b, i, 0)),
            pl.BlockSpec((1, sb_w, tm), lambda b, i: (b, 0, i)),
            pl.BlockSpec((1, tm, pool_w), lambda b, i: (b, i, 0)),
        ],
        out_shape=[
            jax.ShapeDtypeStruct((bsz, sb_w, n_pos), BF16),
            jax.ShapeDtypeStruct((bsz, n_pos, sb_w), BF16),
            jax.ShapeDtypeStruct((bsz, sb_w, n_pos), BF16),
            jax.ShapeDtypeStruct((bsz, n_pos, pool_w), BF16),
        ],
        scratch_shapes=[pltpu.VMEM((POOL_HALO, pool_w), F32)],
        compiler_params=_params(("parallel", "arbitrary")),
        name="inproj1",
    )(x, w_bf, pool_w_bf, pool_scale.astype(F32).reshape(1, pool_w))


def _sb_attn_kernel(qt_ref, k_ref, vt_ref, o_ref, za_ref, zb_ref, *, tq, tk):
    i = pl.program_id(2)
    qt = qt_ref[0]
    zeros = jnp.zeros((HEAD_DIM, tq), BF16)
    qpads = (jnp.concatenate([qt[:HEAD_DIM], zeros], axis=0),
             jnp.concatenate([zeros, qt[HEAD_DIM:]], axis=0))
    cb = min(SB_SCAN_ROWS, tk)
    upper = (lax.broadcasted_iota(jnp.int32, (cb, cb), 1)
             > lax.broadcasted_iota(jnp.int32, (cb, cb), 0)).astype(BF16)

    heads = range(2)

    def scores(j, buf):
        kj = k_ref[0, pl.ds(pl.multiple_of(j * tk, tk), tk), :]
        for hh in heads:
            buf[hh] = jnp.dot(kj, qpads[hh], preferred_element_type=F32)

    def consume(j, buf, carry, masked, lane0=0):
        k0 = pl.multiple_of(j * tk, tk)
        nq = tq - lane0
        whole = carry
        carry = [(acc[:, lane0:], later[:, lane0:]) for acc, later in whole]
        if masked:
            key = k0 + lax.broadcasted_iota(jnp.int32, (tk, nq), 0)
            qpos = i * tq + lane0 + lax.broadcasted_iota(jnp.int32, (tk, nq), 1)
            keep = key < qpos
        zs = [buf[hh, :, lane0:tq] for hh in heads]
        sps = [jnp.maximum(z, 0.0) + jnp.log(1.0 + jnp.exp2(jnp.abs(z) * (-LOG2E))) for z in zs]
        if masked:
            sps = [jnp.where(keep, sp, 0.0) for sp in sps]
        his = [sp.astype(BF16) for sp in sps]
        laters = [carry[hh][1] for hh in heads]
        pieces = [[None] * (tk // cb) for _ in heads]
        for c in reversed(range(tk // cb)):
            rs = slice(c * cb, (c + 1) * cb)
            sufs = [jnp.dot(upper, his[hh][rs], preferred_element_type=F32)
                    for hh in heads]
            for hh in heads:
                spc = sps[hh][rs]
                pieces[hh][c] = jnp.exp2((zs[hh][rs] - (spc + sufs[hh] + laters[hh])) * LOG2E)
                laters[hh] = laters[hh] + sufs[hh][0:1] + spc[0:1]
        out = []
        for hh in heads:
            a = jnp.concatenate(pieces[hh], axis=0)
            if masked:
                a = jnp.where(keep, a, 0.0)
            vj = vt_ref[0, hh * HEAD_DIM:(hh + 1) * HEAD_DIM, pl.ds(k0, tk)]
            acc = carry[hh][0] + jnp.dot(vj, a.astype(BF16), preferred_element_type=F32)
            later = laters[hh]
            if lane0:
                acc = jnp.concatenate([whole[hh][0][:, :lane0], acc], axis=1)
                later = jnp.concatenate([whole[hh][1][:, :lane0], later], axis=1)
            out.append((acc, later))
        return tuple(out)

    init = (jnp.zeros((HEAD_DIM, tq), F32), jnp.zeros((1, tq), F32))
    carry = (init, init)
    n_diag = tq // tk
    last = (i + 1) * n_diag - 1
    n_old = i * n_diag
    bufs = (za_ref, zb_ref)
    scores(last, bufs[0])
    for d in range(n_diag):
        nxt = last - d - 1 if d + 1 < n_diag else jnp.maximum(n_old - 1, 0)
        scores(nxt, bufs[(d + 1) % 2])
        carry = consume(last - d, bufs[d % 2], carry, True, lane0=tq - (d + 1) * tk)
    first_old = bufs[n_diag % 2]
    spare = bufs[(n_diag + 1) % 2]

    def reach(c):
        return jnp.min(jnp.minimum(c[0][1], c[1][1]))

    def live(state):
        t, r, _ = state
        return jnp.logical_and(t < n_old, r < SB_UNDERFLOW)

    def older(state):
        t, _, c = state
        scores(n_old - 1 - t, spare)
        c = consume(n_old - 1 - t, spare, c, False)
        return t + 1, reach(c), c

    r0 = reach(carry)
    carry = lax.cond(jnp.logical_and(n_old > 0, r0 < SB_UNDERFLOW),
                     lambda c: consume(n_old - 1, first_old, c, False), lambda c: c, carry)
    _, _, carry = lax.while_loop(live, older, (jnp.int32(1), reach(carry), carry))
    o_ref[0] = jnp.concatenate([carry[0][0], carry[1][0]], axis=0).T.astype(o_ref.dtype)


def _sb_attention(qt, k, vt, *, tq, tk):
    bsz, sb_w, n_pos = qt.shape
    return pl.pallas_call(
        functools.partial(_sb_attn_kernel, tq=tq, tk=tk),
        grid=(bsz, sb_w // LANES, n_pos // tq),
        in_specs=[
            pl.BlockSpec((1, LANES, tq), lambda b, h, i: (b, h, i)),
            pl.BlockSpec((1, n_pos, LANES), lambda b, h, i: (b, 0, h)),
            pl.BlockSpec((1, LANES, n_pos), lambda b, h, i: (b, h, 0)),
        ],
        out_specs=pl.BlockSpec((1, tq, LANES), lambda b, h, i: (b, i, h)),
        out_shape=jax.ShapeDtypeStruct((bsz, n_pos, sb_w), BF16),
        scratch_shapes=[pltpu.VMEM((2, tk, tq), F32), pltpu.VMEM((2, tk, tq), F32)],
        compiler_params=_params(("parallel", "parallel", "arbitrary")),
        name="sb_attention",
    )(qt, k, vt)


def _rope_tables(n_pos):
    inv = ROPE_THETA ** (-jnp.arange(0, HEAD_DIM, 2, dtype=F32) / HEAD_DIM)
    ang = jnp.arange(n_pos, dtype=F32)[:, None] * inv[None, :]
    ang = jnp.concatenate([ang, ang], axis=-1)
    return jnp.cos(ang), jnp.sin(ang)


def _lambda_init(layer_idx):
    return 0.8 - 0.6 * math.exp(-0.3 * layer_idx)


def kernel(x, router_w, router_bias, l0_in_proj, l0_conv_w, l0_conv_b, l0_conv_ln_g, l0_conv_ln_b, l0_lambda_q1, l0_lambda_k1, l0_lambda_q2, l0_lambda_k2, l0_subln_g, l0_out_proj, l0_ln_mix_g, l0_ln_mix_b, l0_w_gate, l0_w_up, l0_w_down, l0_ln_ffn_g, l0_ln_ffn_b, l1_in_proj, l1_pool_w, l1_pool_scale, l1_out_proj, l1_ln_mix_g, l1_ln_mix_b, l1_w_gate, l1_w_up, l1_w_down, l1_ln_ffn_g, l1_ln_ffn_b):
    bsz, n_pos, d = x.shape
    n_tok = bsz * n_pos
    x = x.astype(F32)

    conv_ch = l0_conv_w.shape[1]
    dv = l0_subln_g.shape[0]
    n_diff = (l0_out_proj.shape[0] - conv_ch) // dv
    qk_w = n_diff * 2 * HEAD_DIM
    v_w = n_diff * dv

    tm = min(PROJ_ROWS, n_pos)
    tq_diff = min(DIFF_TILE, n_pos)
    tq_sb = min(SB_QUERY_TILE, n_pos)
    tk_sb = min(SB_KEY_TILE, n_pos)
    ts = min(CONV_ROWS, n_pos)
    tmc = min(COMBINE_ROWS, n_pos)
    tmd = min(DISPATCH_ROWS, n_tok)
    tm_out = min(OUTPROJ_ROWS, n_tok)
    assert d == SUBLANES * LANES, "token-tile layout needs one (8, 128) tile per token"
    assert n_pos % max(tm, tq_diff, tq_sb, ts, tmc) == 0 and tq_sb % tk_sb == 0
    assert n_tok % max(tmd, tm_out) == 0 and (2 * n_tok) % MOE_ROW_TILE == 0

    cos, sin = _rope_tables(n_pos)
    sign = jnp.where(jnp.arange(HEAD_DIM) < HEAD_DIM // 2, -1.0, 1.0).astype(F32)
    cos128 = jnp.concatenate([cos, cos], axis=-1)
    sins128 = jnp.concatenate([sin * sign, sin * sign], axis=-1)
    cost = cos.T
    sint = sin.T

    glu, qt, k, vt = _inproj0(x, l0_in_proj.astype(BF16), cos128, sins128, cost, sint,
                              conv_ch=conv_ch, qk_w=qk_w, v_w=v_w, tm=tm)
    a = _conv_module(glu, l0_conv_w.astype(F32), l0_conv_b.astype(F32),
                     l0_conv_ln_g.astype(F32), l0_conv_ln_b.astype(F32), ts=ts)
    lam_init = _lambda_init(0)
    lam = (jnp.exp(jnp.sum(l0_lambda_q1.astype(F32) * l0_lambda_k1.astype(F32)))
           - jnp.exp(jnp.sum(l0_lambda_q2.astype(F32) * l0_lambda_k2.astype(F32))) + lam_init)
    o = _diff_attention(qt, k, vt, lam, l0_subln_g, n_heads=n_diff, lam_init=lam_init,
                        tq=tq_diff, tk=tq_diff)
    xf = x.reshape(n_tok, d)
    y, route, gates, counts = _outproj_ln_route(
        a.reshape(n_tok, conv_ch), o.reshape(n_tok, v_w), l0_out_proj, xf,
        l0_ln_mix_g.astype(F32), l0_ln_mix_b.astype(F32), router_w, router_bias, tm=tm_out)
    xf = _moe_sparse(y, route, gates, counts, l0_w_gate, l0_w_up, l0_w_down,
                     l0_ln_ffn_g.astype(F32), l0_ln_ffn_b.astype(F32), tmd=tmd, tmc=tmc)

    sb_w = l1_in_proj.shape[1] - len(POOL_WINDOWS) * POOL_CH
    sb_w = sb_w // 3
    qt, k, vt, dpool = _inproj1(xf.reshape(bsz, n_pos, d), l1_in_proj.astype(BF16),
                                l1_pool_w.astype(BF16), l1_pool_scale, sb_w=sb_w, tm=tm)
    o = _sb_attention(qt, k, vt, tq=tq_sb, tk=tk_sb)
    y, route, gates, counts = _outproj_ln_route(
        o.reshape(n_tok, sb_w), dpool.reshape(n_tok, -1), l1_out_proj, xf,
        l1_ln_mix_g.astype(F32), l1_ln_mix_b.astype(F32), router_w, router_bias, tm=tm_out)
    xf = _moe_sparse(y, route, gates, counts, l1_w_gate, l1_w_up, l1_w_down,
                     l1_ln_ffn_g.astype(F32), l1_ln_ffn_b.astype(F32), tmd=tmd, tmc=tmc)
    return xf.reshape(bsz, n_pos, d)
```
